```python
import math
import jax, jax.numpy as jnp
from jax import lax
import numpy as np

D_MODEL = 1024
BATCH = 16
SEQ = 2048
DEPTH = 1

N_META = 16
GRID_W = 64
D_FF = 2816
SSM_WIDTH = D_MODEL // 2
SSM_GROUP = 16
SSM_GROUPS = SSM_WIDTH // SSM_GROUP
SSM_STATE = 64
NA_HEAD_DIM = 64
NA_WIDTH = D_MODEL // 2
NA_HEADS = NA_WIDTH // NA_HEAD_DIM
WIN_H = 8
WIN_W = 16
PROJ_SPLITS = (SSM_WIDTH, SSM_WIDTH + NA_WIDTH, SSM_WIDTH + 2 * NA_WIDTH,
               SSM_WIDTH + 3 * NA_WIDTH, SSM_WIDTH + 3 * NA_WIDTH + D_MODEL)
PROJ_WIDTH = SSM_WIDTH + 3 * NA_WIDTH + 2 * D_MODEL
RMS_EPS = 1e-6
DT_MIN = 1e-3
DT_MAX = 1e-1
A_RE_MAX = -1e-4
MASK_VALUE = -1e30

kernel_name = "hybrid_s5_natten_macaron_encoder"


def rms_norm(x, g):
    x32 = x.astype(jnp.float32)
    y = x32 * lax.rsqrt(jnp.mean(x32 * x32, axis=-1, keepdims=True) + RMS_EPS)
    return (y * g.astype(jnp.float32)).astype(x.dtype)


def swiglu_ffn(x, w_in, w_out):
    gate, up = jnp.split(x @ w_in, 2, axis=-1)
    return (jax.nn.silu(gate) * up) @ w_out


def _complex_linear_combine(e1, e2):
    a1r, a1i, b1r, b1i = e1
    a2r, a2i, b2r, b2i = e2
    ar = a2r * a1r - a2i * a1i
    ai = a2r * a1i + a2i * a1r
    br = a2r * b1r - a2i * b1i + b2r
    bi = a2r * b1i + a2i * b1r + b2i
    return (ar, ai, br, bi)


def s5_direction(u_g, a_re, a_im, log_dt, b_re, b_im, reverse):
    dt = jnp.exp(log_dt.astype(jnp.float32))[:, None]
    a_re = jnp.minimum(a_re.astype(jnp.float32), A_RE_MAX)
    a_im = a_im.astype(jnp.float32)
    mag = jnp.exp(a_re * dt)
    lb_re = mag * jnp.cos(a_im * dt)
    lb_im = mag * jnp.sin(a_im * dt)
    den = a_re * a_re + a_im * a_im
    nr = lb_re - 1.0
    ni = lb_im
    f_re = (nr * a_re + ni * a_im) / den
    f_im = (ni * a_re - nr * a_im) / den
    b_re = b_re.astype(jnp.float32)
    b_im = b_im.astype(jnp.float32)
    bb_re = f_re[..., None] * b_re - f_im[..., None] * b_im
    bb_im = f_re[..., None] * b_im + f_im[..., None] * b_re
    bu_re = jnp.einsum('btgp,gnp->btgn', u_g, bb_re)
    bu_im = jnp.einsum('btgp,gnp->btgn', u_g, bb_im)
    seq_len = u_g.shape[1]
    a_r = jnp.broadcast_to(lb_re[None, None], (1, seq_len) + lb_re.shape)
    a_i = jnp.broadcast_to(lb_im[None, None], (1, seq_len) + lb_im.shape)
    _, _, h_re, h_im = lax.associative_scan(
        _complex_linear_combine, (a_r, a_i, bu_re, bu_im), axis=1, reverse=reverse)
    return h_re, h_im


def s5_mixer(u, a_re_f, a_im_f, log_dt_f, a_re_b, a_im_b, log_dt_b,
             b_re, b_im, c_re, c_im, d, w_glu):
    bsz, seq_len, _ = u.shape
    u_g = u.astype(jnp.float32).reshape(bsz, seq_len, SSM_GROUPS, SSM_GROUP)
    hf_re, hf_im = s5_direction(u_g, a_re_f, a_im_f, log_dt_f, b_re, b_im, False)
    hb_re, hb_im = s5_direction(u_g, a_re_b, a_im_b, log_dt_b, b_re, b_im, True)
    h_re = hf_re + hb_re
    h_im = hf_im + hb_im
    y = (jnp.einsum('btgn,gpn->btgp', h_re, c_re.astype(jnp.float32))
         - jnp.einsum('btgn,gpn->btgp', h_im, c_im.astype(jnp.float32))
         + d.astype(jnp.float32) * u_g)
    z = jax.nn.gelu(y.reshape(bsz, seq_len, SSM_WIDTH))
    z = z * jax.nn.sigmoid(z @ w_glu.astype(jnp.float32))
    return z.astype(u.dtype)


def neighbourhood_attention(q, k, v, rpb):
    bsz, seq_len, _ = q.shape
    n_real = seq_len - N_META
    rows = n_real // GRID_W
    kh = min(WIN_H, rows)
    scale = NA_HEAD_DIM ** -0.5
    q = q * scale

    def meta_heads(t):
        return t[:, :N_META].reshape(bsz, N_META, NA_HEADS, NA_HEAD_DIM).transpose(0, 2, 1, 3)

    def grid_heads(t):
        return t[:, N_META:].reshape(bsz, rows, GRID_W, NA_HEADS, NA_HEAD_DIM).transpose(0, 3, 1, 2, 4)

    q_m, k_m, v_m = meta_heads(q), meta_heads(k), meta_heads(v)
    q_g, k_g, v_g = grid_heads(q), grid_heads(k), grid_heads(v)

    cols = jnp.arange(GRID_W)
    col_start = jnp.clip(cols - WIN_W // 2, 0, GRID_W - WIN_W)
    col_valid = (cols[None, :] >= col_start[:, None]) & (cols[None, :] < col_start[:, None] + WIN_W)
    col_idx = jnp.clip(cols[None, :] - cols[:, None] + WIN_W - 1, 0, 2 * WIN_W - 2)
    rpb_c = jnp.take(rpb.astype(jnp.float32), col_idx, axis=2)
    rpb_c = jnp.where(col_valid[None, None], rpb_c, MASK_VALUE)

    def row_block(r):
        row_start = jnp.clip(r - kh // 2, 0, rows - kh)
        k_rows = lax.dynamic_slice_in_dim(k_g, row_start, kh, axis=2)
        v_rows = lax.dynamic_slice_in_dim(v_g, row_start, kh, axis=2)
        q_r = lax.dynamic_index_in_dim(q_g, r, axis=2, keepdims=False)
        row_idx = row_start + jnp.arange(kh) - r + WIN_H - 1
        bias = jnp.take(rpb_c, row_idx, axis=1).transpose(0, 2, 1, 3)
        s_win = jnp.einsum('bhqd,bhrkd->bhqrk', q_r, k_rows).astype(jnp.float32) + bias[None]
        s_meta = jnp.einsum('bhqd,bhmd->bhqm', q_r, k_m).astype(jnp.float32)
        s = jnp.concatenate([s_win.reshape(bsz, NA_HEADS, GRID_W, kh * GRID_W), s_meta], axis=-1)
        p = jax.nn.softmax(s, axis=-1).astype(v.dtype)
        p_win = p[..., :kh * GRID_W].reshape(bsz, NA_HEADS, GRID_W, kh, GRID_W)
        p_meta = p[..., kh * GRID_W:]
        return (jnp.einsum('bhqrk,bhrkd->bhqd', p_win, v_rows)
                + jnp.einsum('bhqm,bhmd->bhqd', p_meta, v_m))

    o_grid = lax.map(row_block, jnp.arange(rows))
    o_real = o_grid.transpose(1, 0, 3, 2, 4).reshape(bsz, n_real, NA_WIDTH)

    s_mm = jnp.einsum('bhmd,bhnd->bhmn', q_m, k_m).astype(jnp.float32)
    p_mm = jax.nn.softmax(s_mm, axis=-1).astype(v.dtype)
    o_meta = jnp.einsum('bhmn,bhnd->bhmd', p_mm, v_m).transpose(0, 2, 1, 3).reshape(bsz, N_META, NA_WIDTH)
    return jnp.concatenate([o_meta, o_real], axis=1)


def hybrid_mixer(hn, w_in, a_re_f, a_im_f, log_dt_f, a_re_b, a_im_b, log_dt_b,
                 b_re, b_im, c_re, c_im, d, w_glu, rpb, w_branch_ssm, w_branch_na, w_out):
    proj = hn @ w_in
    u, q, k, v, g_ssm, g_na = jnp.split(proj, PROJ_SPLITS, axis=-1)
    y_ssm = s5_mixer(u, a_re_f, a_im_f, log_dt_f, a_re_b, a_im_b, log_dt_b,
                     b_re, b_im, c_re, c_im, d, w_glu) @ w_branch_ssm
    y_na = neighbourhood_attention(q, k, v, rpb) @ w_branch_na
    merged = jax.nn.sigmoid(g_ssm) * y_ssm + jax.nn.sigmoid(g_na) * y_na
    return merged @ w_out


def setup_inputs(seed: int = 0) -> dict:
    key = jax.random.key(seed)
    ks = iter(jax.random.split(key, 32))

    def nrm(shape, scale):
        return jax.random.normal(next(ks), shape, jnp.float32) * scale

    def gain(shape):
        return 1.0 + nrm(shape, 0.02)

    L, G, N, P = DEPTH, SSM_GROUPS, SSM_STATE, SSM_GROUP
    x = nrm((BATCH, SEQ, D_MODEL), 1.0)
    meta_tokens = nrm((N_META, D_MODEL), 1.0)
    norm_ffn1 = gain((L, D_MODEL))
    w_ffn1_in = nrm((L, D_MODEL, 2 * D_FF), D_MODEL ** -0.5)
    w_ffn1_out = nrm((L, D_FF, D_MODEL), D_FF ** -0.5)
    norm_mix = gain((L, D_MODEL))
    w_in = nrm((L, D_MODEL, PROJ_WIDTH), D_MODEL ** -0.5)
    n_idx = jnp.arange(N, dtype=jnp.float32)
    ssm_a_re_fwd = -0.5 + nrm((L, G, N), 0.01)
    ssm_a_im_fwd = math.pi * n_idx + nrm((L, G, N), 0.01)
    ssm_log_dt_fwd = jax.random.uniform(next(ks), (L, G), jnp.float32, math.log(DT_MIN), math.log(DT_MAX))
    ssm_a_re_bwd = -0.5 + nrm((L, G, N), 0.01)
    ssm_a_im_bwd = math.pi * n_idx + nrm((L, G, N), 0.01)
    ssm_log_dt_bwd = jax.random.uniform(next(ks), (L, G), jnp.float32, math.log(DT_MIN), math.log(DT_MAX))
    ssm_b_re = nrm((L, G, N, P), (2 * P) ** -0.5)
    ssm_b_im = nrm((L, G, N, P), (2 * P) ** -0.5)
    ssm_c_re = nrm((L, G, P, N), N ** -0.5)
    ssm_c_im = nrm((L, G, P, N), N ** -0.5)
    ssm_d = nrm((L, G, P), 1.0)
    w_glu = nrm((L, SSM_WIDTH, SSM_WIDTH), SSM_WIDTH ** -0.5)
    na_rpb = nrm((L, NA_HEADS, 2 * WIN_H - 1, 2 * WIN_W - 1), 0.02)
    w_branch_ssm = nrm((L, SSM_WIDTH, D_MODEL), SSM_WIDTH ** -0.5)
    w_branch_na = nrm((L, NA_WIDTH, D_MODEL), NA_WIDTH ** -0.5)
    w_out = nrm((L, D_MODEL, D_MODEL), D_MODEL ** -0.5)
    norm_ffn2 = gain((L, D_MODEL))
    w_ffn2_in = nrm((L, D_MODEL, 2 * D_FF), D_MODEL ** -0.5)
    w_ffn2_out = nrm((L, D_FF, D_MODEL), D_FF ** -0.5)
    norm_final = gain((D_MODEL,))
    return {"x": x, "meta_tokens": meta_tokens,
            "norm_ffn1": norm_ffn1, "w_ffn1_in": w_ffn1_in, "w_ffn1_out": w_ffn1_out,
            "norm_mix": norm_mix, "w_in": w_in,
            "ssm_a_re_fwd": ssm_a_re_fwd, "ssm_a_im_fwd": ssm_a_im_fwd, "ssm_log_dt_fwd": ssm_log_dt_fwd,
            "ssm_a_re_bwd": ssm_a_re_bwd, "ssm_a_im_bwd": ssm_a_im_bwd, "ssm_log_dt_bwd": ssm_log_dt_bwd,
            "ssm_b_re": ssm_b_re, "ssm_b_im": ssm_b_im, "ssm_c_re": ssm_c_re, "ssm_c_im": ssm_c_im,
            "ssm_d": ssm_d, "w_glu": w_glu, "na_rpb": na_rpb,
            "w_branch_ssm": w_branch_ssm, "w_branch_na": w_branch_na, "w_out": w_out,
            "norm_ffn2": norm_ffn2, "w_ffn2_in": w_ffn2_in, "w_ffn2_out": w_ffn2_out,
            "norm_final": norm_final}


def reference(x, meta_tokens, norm_ffn1, w_ffn1_in, w_ffn1_out, norm_mix, w_in,
              ssm_a_re_fwd, ssm_a_im_fwd, ssm_log_dt_fwd, ssm_a_re_bwd, ssm_a_im_bwd, ssm_log_dt_bwd,
              ssm_b_re, ssm_b_im, ssm_c_re, ssm_c_im, ssm_d, w_glu, na_rpb,
              w_branch_ssm, w_branch_na, w_out, norm_ffn2, w_ffn2_in, w_ffn2_out, norm_final):
    bsz = x.shape[0]
    meta = jnp.broadcast_to(meta_tokens[None].astype(x.dtype), (bsz, N_META, x.shape[-1]))
    h = jnp.concatenate([meta, x], axis=1)
    for l in range(DEPTH):
        h = h + 0.5 * swiglu_ffn(rms_norm(h, norm_ffn1[l]), w_ffn1_in[l], w_ffn1_out[l])
        h = h + hybrid_mixer(rms_norm(h, norm_mix[l]), w_in[l],
                             ssm_a_re_fwd[l], ssm_a_im_fwd[l], ssm_log_dt_fwd[l],
                             ssm_a_re_bwd[l], ssm_a_im_bwd[l], ssm_log_dt_bwd[l],
                             ssm_b_re[l], ssm_b_im[l], ssm_c_re[l], ssm_c_im[l], ssm_d[l],
                             w_glu[l], na_rpb[l], w_branch_ssm[l], w_branch_na[l], w_out[l])
        h = h + 0.5 * swiglu_ffn(rms_norm(h, norm_ffn2[l]), w_ffn2_in[l], w_ffn2_out[l])
    h = rms_norm(h, norm_final)
    return h[:, N_META:]
```

```python
import functools
import math

import jax
import jax.numpy as jnp
from jax import lax
from jax.experimental import pallas as pl
from jax.experimental.pallas import tpu as pltpu

D_MODEL = 1024
N_META = 16
GRID_W = 64
D_FF = 2816
SSM_WIDTH = 512
SSM_GROUP = 16
SSM_GROUPS = 32
SSM_STATE = 64
N_STATES = SSM_GROUPS * SSM_STATE
NA_HEAD_DIM = 64
NA_WIDTH = 512
NA_HEADS = 8
WIN_H = 8
WIN_W = 16
UQKV_WIDTH = SSM_WIDTH + 3 * NA_WIDTH
RMS_EPS = 1e-6
A_RE_MAX = -1e-4
MASK_VALUE = -1e30

LANES = 128
MXU_COLS = 256
VMEM_LIMIT = 56 * 1024 * 1024

ROW_TILE = 512
FF_CHUNK = 1408
SCAN_COLS = 256
NA_Q_ROWS = 4
NA_K_ROWS = 12
NA_META_PAD = 128

BF16 = jnp.bfloat16
F32 = jnp.float32


def _const_spec(shape):
    zeros = (0,) * len(shape)
    return pl.BlockSpec(shape, lambda *_: zeros, pipeline_mode=pl.Buffered(1))


def _params(n_grid_dims):
    return pltpu.CompilerParams(
        dimension_semantics=("arbitrary",) * n_grid_dims,
        vmem_limit_bytes=VMEM_LIMIT)


def _rms_norm(x, g):
    ms = jnp.mean(x * x, axis=-1, keepdims=True)
    return x * lax.rsqrt(ms + RMS_EPS) * g


def _dot(a, b):
    return jnp.dot(a, b, preferred_element_type=F32)


def _ffn_body(x_ref, g_ref, w_in_ref, w_out_ref, gf_ref, o_ref, *, final_norm):
    x = x_ref[...]
    hn = _rms_norm(x, g_ref[...]).astype(BF16)
    acc = jnp.zeros(x.shape, F32)
    for c in range(D_FF // FF_CHUNK):
        lo = c * FF_CHUNK
        gate = _dot(hn, w_in_ref[:, lo:lo + FF_CHUNK])
        up = _dot(hn, w_in_ref[:, D_FF + lo:D_FF + lo + FF_CHUNK])
        act = (jax.nn.silu(gate) * up).astype(BF16)
        acc = acc + _dot(act, w_out_ref[lo:lo + FF_CHUNK, :])
    y = x + 0.5 * acc
    if final_norm:
        y = _rms_norm(y, gf_ref[...])
    o_ref[...] = y


def _ffn_call(x, g, w_in, w_out, gf, *, grid, x_spec, o_spec, out_shape, final_norm):
    return pl.pallas_call(
        functools.partial(_ffn_body, final_norm=final_norm),
        grid=grid,
        in_specs=[x_spec, _const_spec((1, D_MODEL)), _const_spec((D_MODEL, 2 * D_FF)),
                  _const_spec((D_FF, D_MODEL)), _const_spec((1, D_MODEL))],
        out_specs=o_spec,
        out_shape=jax.ShapeDtypeStruct(out_shape, F32),
        compiler_params=_params(len(grid)),
        name="ffn",
    )(x, g, w_in, w_out, gf)


def _proj_body(h_ref, g_ref, w_ref, u_ref, q_ref, k_ref, v_ref):
    hn = _rms_norm(h_ref[...], g_ref[...]).astype(BF16)
    p = _dot(hn, w_ref[...])
    u_ref[...] = p[:, :SSM_WIDTH]
    q_ref[...] = (p[:, SSM_WIDTH:SSM_WIDTH + NA_WIDTH] * (NA_HEAD_DIM ** -0.5)).astype(BF16)
    k_ref[...] = p[:, SSM_WIDTH + NA_WIDTH:SSM_WIDTH + 2 * NA_WIDTH].astype(BF16)
    v_ref[...] = p[:, SSM_WIDTH + 2 * NA_WIDTH:].astype(BF16)


def _proj_call(h, g, w, tile):
    rows = h.shape[0]
    row_spec = lambda width: pl.BlockSpec((tile, width), lambda i: (i, 0))
    return pl.pallas_call(
        _proj_body,
        grid=(rows // tile,),
        in_specs=[row_spec(D_MODEL), _const_spec((1, D_MODEL)), _const_spec((D_MODEL, UQKV_WIDTH))],
        out_specs=[row_spec(SSM_WIDTH)] * 4,
        out_shape=[jax.ShapeDtypeStruct((rows, SSM_WIDTH), F32)]
        + [jax.ShapeDtypeStruct((rows, NA_WIDTH), BF16)] * 3,
        compiler_params=_params(1),
        name="proj",
    )(h, g, w)


def _disc_body(a_re_ref, a_im_ref, log_dt_ref, b_re_ref, b_im_ref,
               lb_re_ref, lb_im_ref, bb_re_ref, bb_im_ref):
    dt = jnp.exp(log_dt_ref[...])
    a_re = jnp.minimum(a_re_ref[...], A_RE_MAX)
    a_im = a_im_ref[...]
    mag = jnp.exp(a_re * dt)
    lb_re = mag * jnp.cos(a_im * dt)
    lb_im = mag * jnp.sin(a_im * dt)
    den = a_re * a_re + a_im * a_im
    nr = lb_re - 1.0
    ni = lb_im
    f_re = (nr * a_re + ni * a_im) / den
    f_im = (ni * a_re - nr * a_im) / den
    lb_re_ref[...] = lb_re
    lb_im_ref[...] = lb_im
    bb_re_ref[...] = f_re * b_re_ref[...] - f_im * b_im_ref[...]
    bb_im_ref[...] = f_re * b_im_ref[...] + f_im * b_re_ref[...]


def _disc_call(a_re, a_im, log_dt, b_re_t, b_im_t):
    row = jax.ShapeDtypeStruct((1, N_STATES), F32)
    mat = jax.ShapeDtypeStruct((SSM_GROUP, N_STATES), F32)
    return pl.pallas_call(_disc_body, out_shape=[row, row, mat, mat], name="disc")(
        a_re, a_im, log_dt, b_re_t, b_im_t)


SCAN_BLOCKS = N_STATES // SCAN_COLS
Y_COLS = LANES


def _s5_body(*refs, reverse, steps, finish):
    if finish:
        (u_ref, init_ref, lb_re_ref, lb_im_ref, w_re_ref, w_im_ref, c_re_ref, c_im_ref,
         yf_ref, d_ref, w_glu_ref, o_ref, fin_ref, bu_re, bu_im, h_re, h_im) = refs
    else:
        (u_ref, init_ref, lb_re_ref, lb_im_ref, w_re_ref, w_im_ref, c_re_ref, c_im_ref,
         o_ref, fin_ref, bu_re, bu_im, h_re, h_im) = refs
    nb = init_ref.shape[0]

    @pl.when(pl.program_id(0) == 0)
    def _():
        h_re[...] = init_ref[:, :N_STATES]
        h_im[...] = init_ref[:, N_STATES:]

    u = u_ref[...]
    ub = u.astype(BF16)
    y_parts = []
    for j in range(SCAN_BLOCKS):
        cols = slice(j * SCAN_COLS, (j + 1) * SCAN_COLS)
        us = ub[:, (j // 2) * LANES:(j // 2 + 1) * LANES]
        bu_re[...] = _dot(us, w_re_ref[j])
        bu_im[...] = _dot(us, w_im_ref[j])
        lr = jnp.broadcast_to(lb_re_ref[:, cols], (nb, SCAN_COLS))
        li = jnp.broadcast_to(lb_im_ref[:, cols], (nb, SCAN_COLS))

        def step(s, carry):
            hr, hi = carry
            t = steps - 1 - s if reverse else s
            r0 = pl.multiple_of(t * nb, nb)
            nr = lr * hr - li * hi + bu_re[pl.ds(r0, nb), :]
            ni = lr * hi + li * hr + bu_im[pl.ds(r0, nb), :]
            bu_re[pl.ds(r0, nb), :] = nr
            bu_im[pl.ds(r0, nb), :] = ni
            return nr, ni

        hr, hi = lax.fori_loop(0, steps, step, (h_re[:, cols], h_im[:, cols]), unroll=8)
        h_re[:, cols] = hr
        h_im[:, cols] = hi
        part = (_dot(bu_re[...].astype(BF16), c_re_ref[j])
                + _dot(bu_im[...].astype(BF16), c_im_ref[j]))
        if j % 2 == 0:
            y_parts.append(part)
        else:
            y_parts[-1] = y_parts[-1] + part
    y = jnp.concatenate(y_parts, axis=-1)
    fin_ref[:, :N_STATES] = h_re[...]
    fin_ref[:, N_STATES:] = h_im[...]
    if finish:
        y = y + yf_ref[...] + d_ref[...] * u
        z = jax.nn.gelu(y)
        z = z * jax.nn.sigmoid(_dot(z.astype(BF16), w_glu_ref[...]))
        o_ref[...] = z.astype(BF16)
    else:
        o_ref[...] = y


def _s5_call(u, init, lb_re, lb_im, w_re, w_im, c_re, c_im, finish_args, *, tile, reverse):
    rows = u.shape[0]
    nb = init.shape[0]
    n_tiles = rows // tile
    finish = finish_args is not None
    row_map = (lambda i: (n_tiles - 1 - i, 0)) if reverse else (lambda i: (i, 0))
    row_spec = pl.BlockSpec((tile, SSM_WIDTH), row_map)
    in_specs = [row_spec, _const_spec((nb, 2 * N_STATES)),
                _const_spec((1, N_STATES)), _const_spec((1, N_STATES)),
                _const_spec((SCAN_BLOCKS, LANES, SCAN_COLS)), _const_spec((SCAN_BLOCKS, LANES, SCAN_COLS)),
                _const_spec((SCAN_BLOCKS, SCAN_COLS, Y_COLS)), _const_spec((SCAN_BLOCKS, SCAN_COLS, Y_COLS))]
    args = [u, init, lb_re, lb_im, w_re, w_im, c_re, c_im]
    if finish:
        in_specs += [row_spec, _const_spec((1, SSM_WIDTH)), _const_spec((SSM_WIDTH, SSM_WIDTH))]
        args += list(finish_args)
    return pl.pallas_call(
        functools.partial(_s5_body, reverse=reverse, steps=tile // nb, finish=finish),
        grid=(n_tiles,),
        in_specs=in_specs,
        out_specs=[row_spec, pl.BlockSpec((nb, 2 * N_STATES), lambda i: (0, 0))],
        out_shape=[jax.ShapeDtypeStruct((rows, SSM_WIDTH), BF16 if finish else F32),
                   jax.ShapeDtypeStruct((nb, 2 * N_STATES), F32)],
        scratch_shapes=[pltpu.VMEM((tile, SCAN_COLS), F32), pltpu.VMEM((tile, SCAN_COLS), F32),
                        pltpu.VMEM((nb, N_STATES), F32), pltpu.VMEM((nb, N_STATES), F32)],
        compiler_params=_params(1),
        name="s5_bwd" if reverse else "s5_fwd",
    )(*args)


NA_Q = NA_Q_ROWS * GRID_W
NA_K = NA_K_ROWS * GRID_W
NA_KEYS = NA_K + NA_META_PAD


def _na_body(q_ref, k_ref, v_ref, km_ref, vm_ref, bias_ref, o_ref, *, n_blocks, rows):
    first_head = lax.broadcasted_iota(jnp.int32, (1, LANES), 1) < NA_HEAD_DIM
    km = km_ref[...]
    vm = vm_ref[...]

    def block(i, carry):
        key_row = jnp.clip(i * NA_Q_ROWS - WIN_H // 2, 0, rows - NA_K_ROWS)
        k0 = pl.multiple_of(key_row * GRID_W, GRID_W)
        q0 = pl.multiple_of(i * NA_Q, NA_Q)
        kind = jnp.where(i == 0, 0, jnp.where(i == n_blocks - 1, 2, 1))
        q = q_ref[pl.ds(q0, NA_Q), :]
        zero = jnp.zeros_like(q)
        qs = jnp.concatenate([jnp.where(first_head, q, zero), jnp.where(first_head, zero, q)], axis=0)
        keys = jnp.concatenate([k_ref[pl.ds(k0, NA_K), :], km], axis=0)
        vals = jnp.concatenate([v_ref[pl.ds(k0, NA_K), :], vm], axis=0)
        s = lax.dot_general(qs, keys, (((1,), (1,)), ((), ())), preferred_element_type=F32)
        s = s + bias_ref[kind].reshape(2 * NA_Q, NA_KEYS)
        m = jnp.max(s, axis=-1, keepdims=True)
        p = jnp.exp(s - m)
        l = jnp.sum(p, axis=-1, keepdims=True)
        o = _dot(p.astype(BF16), vals) / l
        o_ref[pl.ds(q0, NA_Q), :] = jnp.where(first_head, o[:NA_Q], o[NA_Q:]).astype(BF16)
        return carry

    lax.fori_loop(0, n_blocks, block, 0)


def _na_call(q, k, v, km, vm, bias, *, batch, seq):
    rows = seq // GRID_W
    n_pairs = NA_HEADS // 2
    n_blocks = rows // NA_Q_ROWS
    seq_spec = pl.BlockSpec((seq, LANES), lambda hp, b: (0, b * n_pairs + hp))
    meta_spec = pl.BlockSpec((NA_META_PAD, LANES), lambda hp, b: (0, hp))
    bias_spec = pl.BlockSpec((None, 3, 2, NA_Q, NA_KEYS), lambda hp, b: (hp, 0, 0, 0, 0))
    return pl.pallas_call(
        functools.partial(_na_body, n_blocks=n_blocks, rows=rows),
        grid=(n_pairs, batch),
        in_specs=[seq_spec, seq_spec, seq_spec, meta_spec, meta_spec, bias_spec],
        out_specs=seq_spec,
        out_shape=jax.ShapeDtypeStruct((seq, batch * NA_WIDTH), BF16),
        compiler_params=_params(2),
        name="na",
    )(q, k, v, km, vm, bias)


def _na_bias(rpb, rows):
    kh = min(WIN_H, rows)
    cols = jnp.arange(GRID_W)
    col_start = jnp.clip(cols - WIN_W // 2, 0, GRID_W - WIN_W)
    col_valid = (cols[None, :] >= col_start[:, None]) & (cols[None, :] < col_start[:, None] + WIN_W)
    col_idx = jnp.clip(cols[None, :] - cols[:, None] + WIN_W - 1, 0, 2 * WIN_W - 2)
    rpb_c = jnp.take(rpb.astype(F32), col_idx, axis=2)
    rpb_c = jnp.where(col_valid[None, None], rpb_c, MASK_VALUE)
    n_blocks = rows // NA_Q_ROWS
    kinds = []
    for blk in (0, 1, n_blocks - 1):
        q_rows = blk * NA_Q_ROWS + jnp.arange(NA_Q_ROWS)
        key_row0 = min(max(blk * NA_Q_ROWS - WIN_H // 2, 0), rows - NA_K_ROWS)
        k_rows = key_row0 + jnp.arange(NA_K_ROWS)
        row_start = jnp.clip(q_rows - kh // 2, 0, rows - kh)
        row_valid = (k_rows[None, :] >= row_start[:, None]) & (k_rows[None, :] < row_start[:, None] + kh)
        row_idx = jnp.clip(k_rows[None, :] - q_rows[:, None] + WIN_H - 1, 0, 2 * WIN_H - 2)
        b = rpb_c[:, row_idx]
        b = jnp.where(row_valid[None, :, :, None, None], b, MASK_VALUE)
        b = b.transpose(0, 1, 3, 2, 4).reshape(NA_HEADS, NA_Q, NA_K)
        meta = jnp.where(jnp.arange(NA_META_PAD) < N_META, 0.0, MASK_VALUE)
        meta = jnp.broadcast_to(meta, (NA_HEADS, NA_Q, NA_META_PAD))
        kinds.append(jnp.concatenate([b, meta], axis=-1))
    return jnp.stack(kinds, axis=1)


def _merge_body(h_ref, z_ref, a_ref, g_ref, w_gate_ref, w_ssm_ref, w_na_ref, w_out_ref, o_ref):
    h = h_ref[...]
    hn = _rms_norm(h, g_ref[...]).astype(BF16)
    gates = _dot(hn, w_gate_ref[...])
    y_ssm = _dot(z_ref[...], w_ssm_ref[...])
    y_na = _dot(a_ref[...], w_na_ref[...])
    merged = (jax.nn.sigmoid(gates[:, :D_MODEL]) * y_ssm
              + jax.nn.sigmoid(gates[:, D_MODEL:]) * y_na)
    o_ref[...] = h + _dot(merged.astype(BF16), w_out_ref[...])


def _merge_call(h1, z, a, g, w_gate, w_ssm, w_na, w_out, *, batch, seq):
    n_tiles = seq // ROW_TILE
    col_spec = lambda width: pl.BlockSpec((ROW_TILE, width), lambda b, i: (i, b))
    return pl.pallas_call(
        _merge_body,
        grid=(batch, n_tiles),
        in_specs=[col_spec(D_MODEL), col_spec(SSM_WIDTH), col_spec(NA_WIDTH),
                  _const_spec((1, D_MODEL)), _const_spec((D_MODEL, 2 * D_MODEL)),
                  _const_spec((SSM_WIDTH, D_MODEL)), _const_spec((NA_WIDTH, D_MODEL)),
                  _const_spec((D_MODEL, D_MODEL))],
        out_specs=pl.BlockSpec((None, ROW_TILE, D_MODEL), lambda b, i: (b, i, 0)),
        out_shape=jax.ShapeDtypeStruct((batch, seq, D_MODEL), F32),
        compiler_params=_params(2),
        name="merge",
    )(h1, z, a, g, w_gate, w_ssm, w_na, w_out)


def _group_mask(n_rows_per_group, n_cols_per_group):
    r = jnp.arange(SSM_GROUPS * n_rows_per_group)[:, None] // n_rows_per_group
    c = jnp.arange(SSM_GROUPS * n_cols_per_group)[None, :] // n_cols_per_group
    return r == c


def _s5_weights(bb_re_t, bb_im_t, c_re, c_im):
    in_mask = _group_mask(SSM_GROUP, SSM_STATE)
    out_mask = in_mask.T

    def in_slabs(bb_t):
        dense = jnp.where(in_mask, jnp.tile(bb_t, (SSM_GROUPS, 1)), 0.0)
        return jnp.stack([dense[(j // 2) * LANES:(j // 2 + 1) * LANES, j * SCAN_COLS:(j + 1) * SCAN_COLS]
                          for j in range(SCAN_BLOCKS)]).astype(BF16)

    def out_slabs(c):
        c_t = c.transpose(0, 2, 1).reshape(N_STATES, SSM_GROUP)
        dense = jnp.where(out_mask, jnp.tile(c_t, (1, SSM_GROUPS)), 0.0)
        return jnp.stack([dense[j * SCAN_COLS:(j + 1) * SCAN_COLS, (j // 2) * LANES:(j // 2 + 1) * LANES]
                          for j in range(SCAN_BLOCKS)]).astype(BF16)

    return in_slabs(bb_re_t), in_slabs(bb_im_t), out_slabs(c_re), out_slabs(-c_im)


def kernel(x, meta_tokens, norm_ffn1, w_ffn1_in, w_ffn1_out, norm_mix, w_in, ssm_a_re_fwd, ssm_a_im_fwd, ssm_log_dt_fwd, ssm_a_re_bwd, ssm_a_im_bwd, ssm_log_dt_bwd, ssm_b_re, ssm_b_im, ssm_c_re, ssm_c_im, ssm_d, w_glu, na_rpb, w_branch_ssm, w_branch_na, w_out, norm_ffn2, w_ffn2_in, w_ffn2_out, norm_final):
    batch, seq, d_model = x.shape
    assert d_model == D_MODEL and norm_ffn1.shape[0] == 1, "single-layer block of width 1024"
    assert seq % ROW_TILE == 0 and seq % (GRID_W * NA_Q_ROWS) == 0 and batch % 8 == 0
    rows = seq // GRID_W
    assert rows >= NA_K_ROWS + NA_Q_ROWS

    row = lambda a: a.reshape(1, -1).astype(F32)
    g1, gm, g2, gf = row(norm_ffn1[0]), row(norm_mix[0]), row(norm_ffn2[0]), row(norm_final)
    w1_in, w1_out = w_ffn1_in[0].astype(BF16), w_ffn1_out[0].astype(BF16)
    w2_in, w2_out = w_ffn2_in[0].astype(BF16), w_ffn2_out[0].astype(BF16)
    w_uqkv = w_in[0][:, :UQKV_WIDTH].astype(BF16)
    w_gate = w_in[0][:, UQKV_WIDTH:].astype(BF16)

    n_tiles = seq // ROW_TILE
    h1 = _ffn_call(
        x, g1, w1_in, w1_out, gf, grid=(batch, n_tiles),
        x_spec=pl.BlockSpec((None, ROW_TILE, D_MODEL), lambda b, i: (b, i, 0)),
        o_spec=pl.BlockSpec((ROW_TILE, D_MODEL), lambda b, i: (i, b)),
        out_shape=(seq, batch * D_MODEL), final_norm=False)
    meta_spec = pl.BlockSpec((N_META, D_MODEL), lambda i: (0, 0))
    h1_meta = _ffn_call(meta_tokens.astype(F32), g1, w1_in, w1_out, gf, grid=(1,),
                        x_spec=meta_spec, o_spec=meta_spec, out_shape=(N_META, D_MODEL),
                        final_norm=False)

    u, q, k, v = _proj_call(h1.reshape(seq * batch, D_MODEL), gm, w_uqkv, ROW_TILE)
    u_meta, _, k_meta, v_meta = _proj_call(h1_meta, gm, w_uqkv, N_META)

    states = lambda a: a[0].reshape(1, N_STATES).astype(F32)
    per_state = lambda a: jnp.repeat(a[0].astype(F32), SSM_STATE).reshape(1, N_STATES)
    b_re_t = ssm_b_re[0].reshape(N_STATES, SSM_GROUP).T.astype(F32)
    b_im_t = ssm_b_im[0].reshape(N_STATES, SSM_GROUP).T.astype(F32)
    lbf_re, lbf_im, bbf_re, bbf_im = _disc_call(
        states(ssm_a_re_fwd), states(ssm_a_im_fwd), per_state(ssm_log_dt_fwd), b_re_t, b_im_t)
    lbb_re, lbb_im, bbb_re, bbb_im = _disc_call(
        states(ssm_a_re_bwd), states(ssm_a_im_bwd), per_state(ssm_log_dt_bwd), b_re_t, b_im_t)
    wf_re, wf_im, c_re, c_im = _s5_weights(bbf_re, bbf_im, ssm_c_re[0], ssm_c_im[0])
    wb_re, wb_im, _, _ = _s5_weights(bbb_re, bbb_im, ssm_c_re[0], ssm_c_im[0])

    zero_state = jnp.zeros((batch, 2 * N_STATES), F32)
    u_meta_rows = jnp.repeat(u_meta, batch, axis=0)
    _, state_meta = _s5_call(u_meta_rows, zero_state, lbf_re, lbf_im, wf_re, wf_im, c_re, c_im,
                             None, tile=N_META * batch, reverse=False)
    y_fwd, _ = _s5_call(u, state_meta, lbf_re, lbf_im, wf_re, wf_im, c_re, c_im,
                        None, tile=ROW_TILE, reverse=False)
    z, _ = _s5_call(u, zero_state, lbb_re, lbb_im, wb_re, wb_im, c_re, c_im,
                    (y_fwd, row(ssm_d[0]), w_glu[0].astype(BF16)), tile=ROW_TILE, reverse=True)

    bias = _na_bias(na_rpb[0], rows)
    bias = bias.reshape(NA_HEADS // 2, 2, 3, NA_Q, NA_KEYS).transpose(0, 2, 1, 3, 4)
    pad_meta = lambda a: jnp.pad(a, ((0, NA_META_PAD - N_META), (0, 0)))
    att = _na_call(q.reshape(seq, batch * NA_WIDTH), k.reshape(seq, batch * NA_WIDTH),
                   v.reshape(seq, batch * NA_WIDTH), pad_meta(k_meta), pad_meta(v_meta), bias,
                   batch=batch, seq=seq)

    h2 = _merge_call(h1, z.reshape(seq, batch * SSM_WIDTH), att, gm, w_gate,
                     w_branch_ssm[0].astype(BF16), w_branch_na[0].astype(BF16),
                     w_out[0].astype(BF16), batch=batch, seq=seq)

    tok_spec = pl.BlockSpec((None, ROW_TILE, D_MODEL), lambda b, i: (b, i, 0))
    return _ffn_call(h2, g2, w2_in, w2_out, gf, grid=(batch, n_tiles), x_spec=tok_spec,
                     o_spec=tok_spec, out_shape=(batch, seq, D_MODEL), final_norm=True)
```

```python
import functools
import math

import jax
import jax.numpy as jnp
from jax import lax
from jax.experimental import pallas as pl
from jax.experimental.pallas import tpu as pltpu

D_MODEL = 1024
N_META = 16
GRID_W = 64
D_FF = 2816
SSM_WIDTH = 512
SSM_GROUP = 16
SSM_GROUPS = 32
SSM_STATE = 64
N_STATES = SSM_GROUPS * SSM_STATE
NA_HEAD_DIM = 64
NA_WIDTH = 512
NA_HEADS = 8
WIN_H = 8
WIN_W = 16
UQKV_WIDTH = SSM_WIDTH + 3 * NA_WIDTH
RMS_EPS = 1e-6
A_RE_MAX = -1e-4
MASK_VALUE = -1e30

LANES = 128
MXU_COLS = 256
VMEM_LIMIT = 56 * 1024 * 1024

ROW_TILE = 512
FF_CHUNK = 1408
SCAN_COLS = 256
NA_Q_ROWS = 4
NA_K_ROWS = 12
NA_META_PAD = 128

BF16 = jnp.bfloat16
F32 = jnp.float32


def _const_spec(shape):
    zeros = (0,) * len(shape)
    return pl.BlockSpec(shape, lambda *_: zeros, pipeline_mode=pl.Buffered(1))


def _params(n_grid_dims):
    return pltpu.CompilerParams(
        dimension_semantics=("arbitrary",) * n_grid_dims,
        vmem_limit_bytes=VMEM_LIMIT)


def _rms_norm(x, g):
    ms = jnp.mean(x * x, axis=-1, keepdims=True)
    return x * lax.rsqrt(ms + RMS_EPS) * g


def _dot(a, b):
    return jnp.dot(a, b, preferred_element_type=F32)


def _ffn_body(x_ref, g_ref, w_in_ref, w_out_ref, gf_ref, o_ref, *, final_norm):
    x = x_ref[...]
    hn = _rms_norm(x, g_ref[...]).astype(BF16)
    acc = jnp.zeros(x.shape, F32)
    for c in range(D_FF // FF_CHUNK):
        lo = c * FF_CHUNK
        gate = _dot(hn, w_in_ref[:, lo:lo + FF_CHUNK])
        up = _dot(hn, w_in_ref[:, D_FF + lo:D_FF + lo + FF_CHUNK])
        act = (jax.nn.silu(gate) * up).astype(BF16)
        acc = acc + _dot(act, w_out_ref[lo:lo + FF_CHUNK, :])
    y = x + 0.5 * acc
    if final_norm:
        y = _rms_norm(y, gf_ref[...])
    o_ref[...] = y


def _ffn_call(x, g, w_in, w_out, gf, *, grid, x_spec, o_spec, out_shape, final_norm):
    return pl.pallas_call(
        functools.partial(_ffn_body, final_norm=final_norm),
        grid=grid,
        in_specs=[x_spec, _const_spec((1, D_MODEL)), _const_spec((D_MODEL, 2 * D_FF)),
                  _const_spec((D_FF, D_MODEL)), _const_spec((1, D_MODEL))],
        out_specs=o_spec,
        out_shape=jax.ShapeDtypeStruct(out_shape, F32),
        compiler_params=_params(len(grid)),
        name="ffn",
    )(x, g, w_in, w_out, gf)


def _proj_body(h_ref, g_ref, w_ref, u_ref, q_ref, k_ref, v_ref):
    hn = _rms_norm(h_ref[...], g_ref[...]).astype(BF16)
    p = _dot(hn, w_ref[...])
    u_ref[...] = p[:, :SSM_WIDTH]
    q_ref[...] = (p[:, SSM_WIDTH:SSM_WIDTH + NA_WIDTH] * (NA_HEAD_DIM ** -0.5)).astype(BF16)
    k_ref[...] = p[:, SSM_WIDTH + NA_WIDTH:SSM_WIDTH + 2 * NA_WIDTH].astype(BF16)
    v_ref[...] = p[:, SSM_WIDTH + 2 * NA_WIDTH:].astype(BF16)


def _proj_call(h, g, w, tile):
    rows = h.shape[0]
    row_spec = lambda width: pl.BlockSpec((tile, width), lambda i: (i, 0))
    return pl.pallas_call(
        _proj_body,
        grid=(rows // tile,),
        in_specs=[row_spec(D_MODEL), _const_spec((1, D_MODEL)), _const_spec((D_MODEL, UQKV_WIDTH))],
        out_specs=[row_spec(SSM_WIDTH)] * 4,
        out_shape=[jax.ShapeDtypeStruct((rows, SSM_WIDTH), F32)]
        + [jax.ShapeDtypeStruct((rows, NA_WIDTH), BF16)] * 3,
        compiler_params=_params(1),
        name="proj",
    )(h, g, w)


def _disc_body(a_re_ref, a_im_ref, log_dt_ref, b_re_ref, b_im_ref,
               lb_re_ref, lb_im_ref, bb_re_ref, bb_im_ref):
    dt = jnp.exp(log_dt_ref[...])
    a_re = jnp.minimum(a_re_ref[...], A_RE_MAX)
    a_im = a_im_ref[...]
    mag = jnp.exp(a_re * dt)
    lb_re = mag * jnp.cos(a_im * dt)
    lb_im = mag * jnp.sin(a_im * dt)
    den = a_re * a_re + a_im * a_im
    nr = lb_re - 1.0
    ni = lb_im
    f_re = (nr * a_re + ni * a_im) / den
    f_im = (ni * a_re - nr * a_im) / den
    lb_re_ref[...] = lb_re
    lb_im_ref[...] = lb_im
    bb_re_ref[...] = f_re * b_re_ref[...] - f_im * b_im_ref[...]
    bb_im_ref[...] = f_re * b_im_ref[...] + f_im * b_re_ref[...]


def _disc_call(a_re, a_im, log_dt, b_re_t, b_im_t):
    row = jax.ShapeDtypeStruct((1, N_STATES), F32)
    mat = jax.ShapeDtypeStruct((SSM_GROUP, N_STATES), F32)
    return pl.pallas_call(_disc_body, out_shape=[row, row, mat, mat], name="disc")(
        a_re, a_im, log_dt, b_re_t, b_im_t)


SCAN_BLOCKS = N_STATES // SCAN_COLS
Y_COLS = LANES


def _s5_body(*refs, reverse, steps, finish):
    if finish:
        (u_ref, init_ref, lb_re_ref, lb_im_ref, w_re_ref, w_im_ref, c_re_ref, c_im_ref,
         yf_ref, d_ref, w_glu_ref, o_ref, fin_ref, bu_re, bu_im, h_re, h_im) = refs
    else:
        (u_ref, init_ref, lb_re_ref, lb_im_ref, w_re_ref, w_im_ref, c_re_ref, c_im_ref,
         o_ref, fin_ref, bu_re, bu_im, h_re, h_im) = refs
    nb = init_ref.shape[0]

    @pl.when(pl.program_id(0) == 0)
    def _():
        h_re[...] = init_ref[:, :N_STATES]
        h_im[...] = init_ref[:, N_STATES:]

    u = u_ref[...]
    ub = u.astype(BF16)
    y_parts = []
    for j in range(SCAN_BLOCKS):
        cols = slice(j * SCAN_COLS, (j + 1) * SCAN_COLS)
        us = ub[:, (j // 2) * LANES:(j // 2 + 1) * LANES]
        bu_re[...] = _dot(us, w_re_ref[j])
        bu_im[...] = _dot(us, w_im_ref[j])
        lr = jnp.broadcast_to(lb_re_ref[:, cols], (nb, SCAN_COLS))
        li = jnp.broadcast_to(lb_im_ref[:, cols], (nb, SCAN_COLS))

        def step(s, carry):
            hr, hi = carry
            t = steps - 1 - s if reverse else s
            r0 = pl.multiple_of(t * nb, nb)
            nr = lr * hr - li * hi + bu_re[pl.ds(r0, nb), :]
            ni = lr * hi + li * hr + bu_im[pl.ds(r0, nb), :]
            bu_re[pl.ds(r0, nb), :] = nr
            bu_im[pl.ds(r0, nb), :] = ni
            return nr, ni

        hr, hi = lax.fori_loop(0, steps, step, (h_re[:, cols], h_im[:, cols]), unroll=8)
        h_re[:, cols] = hr
        h_im[:, cols] = hi
        part = (_dot(bu_re[...].astype(BF16), c_re_ref[j])
                + _dot(bu_im[...].astype(BF16), c_im_ref[j]))
        if j % 2 == 0:
            y_parts.append(part)
        else:
            y_parts[-1] = y_parts[-1] + part
    y = jnp.concatenate(y_parts, axis=-1)
    fin_ref[:, :N_STATES] = h_re[...]
    fin_ref[:, N_STATES:] = h_im[...]
    if finish:
        y = y + yf_ref[...] + d_ref[...] * u
        z = jax.nn.gelu(y)
        z = z * jax.nn.sigmoid(_dot(z.astype(BF16), w_glu_ref[...]))
        o_ref[...] = z.astype(BF16)
    else:
        o_ref[...] = y


def _s5_call(u, init, lb_re, lb_im, w_re, w_im, c_re, c_im, finish_args, *, tile, reverse):
    rows = u.shape[0]
    nb = init.shape[0]
    n_tiles = rows // tile
    finish = finish_args is not None
    row_map = (lambda i: (n_tiles - 1 - i, 0)) if reverse else (lambda i: (i, 0))
    row_spec = pl.BlockSpec((tile, SSM_WIDTH), row_map)
    in_specs = [row_spec, _const_spec((nb, 2 * N_STATES)),
                _const_spec((1, N_STATES)), _const_spec((1, N_STATES)),
                _const_spec((SCAN_BLOCKS, LANES, SCAN_COLS)), _const_spec((SCAN_BLOCKS, LANES, SCAN_COLS)),
                _const_spec((SCAN_BLOCKS, SCAN_COLS, Y_COLS)), _const_spec((SCAN_BLOCKS, SCAN_COLS, Y_COLS))]
    args = [u, init, lb_re, lb_im, w_re, w_im, c_re, c_im]
    if finish:
        in_specs += [row_spec, _const_spec((1, SSM_WIDTH)), _const_spec((SSM_WIDTH, SSM_WIDTH))]
        args += list(finish_args)
    return pl.pallas_call(
        functools.partial(_s5_body, reverse=reverse, steps=tile // nb, finish=finish),
        grid=(n_tiles,),
        in_specs=in_specs,
        out_specs=[row_spec, pl.BlockSpec((nb, 2 * N_STATES), lambda i: (0, 0))],
        out_shape=[jax.ShapeDtypeStruct((rows, SSM_WIDTH), BF16 if finish else F32),
                   jax.ShapeDtypeStruct((nb, 2 * N_STATES), F32)],
        scratch_shapes=[pltpu.VMEM((tile, SCAN_COLS), F32), pltpu.VMEM((tile, SCAN_COLS), F32),
                        pltpu.VMEM((nb, N_STATES), F32), pltpu.VMEM((nb, N_STATES), F32)],
        compiler_params=_params(1),
        name="s5_bwd" if reverse else "s5_fwd",
    )(*args)


NA_Q = NA_Q_ROWS * GRID_W
NA_K = NA_K_ROWS * GRID_W
NA_KEYS = NA_K + NA_META_PAD


def _na_body(q_ref, k_ref, v_ref, km_ref, vm_ref, bias_ref, o_ref, *, n_blocks, rows):
    first_head = lax.broadcasted_iota(jnp.int32, (1, LANES), 1) < NA_HEAD_DIM
    km = km_ref[...]
    vm = vm_ref[...]

    def block(i, carry):
        key_row = jnp.clip(i * NA_Q_ROWS - WIN_H // 2, 0, rows - NA_K_ROWS)
        k0 = pl.multiple_of(key_row * GRID_W, GRID_W)
        q0 = pl.multiple_of(i * NA_Q, NA_Q)
        kind = jnp.where(i == 0, 0, jnp.where(i == n_blocks - 1, 2, 1))
        q = q_ref[pl.ds(q0, NA_Q), :]
        zero = jnp.zeros_like(q)
        qs = jnp.concatenate([jnp.where(first_head, q, zero), jnp.where(first_head, zero, q)], axis=0)
        keys = jnp.concatenate([k_ref[pl.ds(k0, NA_K), :], km], axis=0)
        vals = jnp.concatenate([v_ref[pl.ds(k0, NA_K), :], vm], axis=0)
        s = lax.dot_general(qs, keys, (((1,), (1,)), ((), ())), preferred_element_type=F32)
        s = s + bias_ref[kind].reshape(2 * NA_Q, NA_KEYS)
        m = jnp.max(s, axis=-1, keepdims=True)
        p = jnp.exp(s - m)
        l = jnp.sum(p, axis=-1, keepdims=True)
        o = _dot(p.astype(BF16), vals) / l
        o_ref[pl.ds(q0, NA_Q), :] = jnp.where(first_head, o[:NA_Q], o[NA_Q:]).astype(BF16)
        return carry

    lax.fori_loop(0, n_blocks, block, 0)


def _na_call(q, k, v, km, vm, bias, *, batch, seq):
    rows = seq // GRID_W
    n_pairs = NA_HEADS // 2
    n_blocks = rows // NA_Q_ROWS
    seq_spec = pl.BlockSpec((seq, LANES), lambda hp, b: (b, hp))
    meta_spec = pl.BlockSpec((NA_META_PAD, LANES), lambda hp, b: (0, hp))
    bias_spec = pl.BlockSpec((None, 3, 2, NA_Q, NA_KEYS), lambda hp, b: (hp, 0, 0, 0, 0))
    return pl.pallas_call(
        functools.partial(_na_body, n_blocks=n_blocks, rows=rows),
        grid=(n_pairs, batch),
        in_specs=[seq_spec, seq_spec, seq_spec, meta_spec, meta_spec, bias_spec],
        out_specs=seq_spec,
        out_shape=jax.ShapeDtypeStruct((batch * seq, NA_WIDTH), BF16),
        compiler_params=_params(2),
        name="na",
    )(q, k, v, km, vm, bias)


def _na_bias(rpb, rows):
    kh = min(WIN_H, rows)
    cols = jnp.arange(GRID_W)
    col_start = jnp.clip(cols - WIN_W // 2, 0, GRID_W - WIN_W)
    col_valid = (cols[None, :] >= col_start[:, None]) & (cols[None, :] < col_start[:, None] + WIN_W)
    col_idx = jnp.clip(cols[None, :] - cols[:, None] + WIN_W - 1, 0, 2 * WIN_W - 2)
    rpb_c = jnp.take(rpb.astype(F32), col_idx, axis=2)
    rpb_c = jnp.where(col_valid[None, None], rpb_c, MASK_VALUE)
    n_blocks = rows // NA_Q_ROWS
    kinds = []
    for blk in (0, 1, n_blocks - 1):
        q_rows = blk * NA_Q_ROWS + jnp.arange(NA_Q_ROWS)
        key_row0 = min(max(blk * NA_Q_ROWS - WIN_H // 2, 0), rows - NA_K_ROWS)
        k_rows = key_row0 + jnp.arange(NA_K_ROWS)
        row_start = jnp.clip(q_rows - kh // 2, 0, rows - kh)
        row_valid = (k_rows[None, :] >= row_start[:, None]) & (k_rows[None, :] < row_start[:, None] + kh)
        row_idx = jnp.clip(k_rows[None, :] - q_rows[:, None] + WIN_H - 1, 0, 2 * WIN_H - 2)
        b = rpb_c[:, row_idx]
        b = jnp.where(row_valid[None, :, :, None, None], b, MASK_VALUE)
        b = b.transpose(0, 1, 3, 2, 4).reshape(NA_HEADS, NA_Q, NA_K)
        meta = jnp.where(jnp.arange(NA_META_PAD) < N_META, 0.0, MASK_VALUE)
        meta = jnp.broadcast_to(meta, (NA_HEADS, NA_Q, NA_META_PAD))
        kinds.append(jnp.concatenate([b, meta], axis=-1))
    return jnp.stack(kinds, axis=1)


def _merge_body(h_ref, z_ref, a_ref, g_ref, w_gate_ref, w_ssm_ref, w_na_ref, w_out_ref, o_ref):
    h = h_ref[...]
    hn = _rms_norm(h, g_ref[...]).astype(BF16)
    gates = _dot(hn, w_gate_ref[...])
    y_ssm = _dot(z_ref[...], w_ssm_ref[...])
    y_na = _dot(a_ref[...], w_na_ref[...])
    merged = (jax.nn.sigmoid(gates[:, :D_MODEL]) * y_ssm
              + jax.nn.sigmoid(gates[:, D_MODEL:]) * y_na)
    o_ref[...] = h + _dot(merged.astype(BF16), w_out_ref[...])


def _merge_call(h1, z, a, g, w_gate, w_ssm, w_na, w_out):
    rows = h1.shape[0]
    row_spec = lambda width: pl.BlockSpec((ROW_TILE, width), lambda i: (i, 0))
    return pl.pallas_call(
        _merge_body,
        grid=(rows // ROW_TILE,),
        in_specs=[row_spec(D_MODEL), row_spec(SSM_WIDTH), row_spec(NA_WIDTH),
                  _const_spec((1, D_MODEL)), _const_spec((D_MODEL, 2 * D_MODEL)),
                  _const_spec((SSM_WIDTH, D_MODEL)), _const_spec((NA_WIDTH, D_MODEL)),
                  _const_spec((D_MODEL, D_MODEL))],
        out_specs=row_spec(D_MODEL),
        out_shape=jax.ShapeDtypeStruct((rows, D_MODEL), F32),
        compiler_params=_params(1),
        name="merge",
    )(h1, z, a, g, w_gate, w_ssm, w_na, w_out)


def _group_mask(n_rows_per_group, n_cols_per_group):
    r = jnp.arange(SSM_GROUPS * n_rows_per_group)[:, None] // n_rows_per_group
    c = jnp.arange(SSM_GROUPS * n_cols_per_group)[None, :] // n_cols_per_group
    return r == c


def _s5_weights(bb_re_t, bb_im_t, c_re, c_im):
    in_mask = _group_mask(SSM_GROUP, SSM_STATE)
    out_mask = in_mask.T

    def in_slabs(bb_t):
        dense = jnp.where(in_mask, jnp.tile(bb_t, (SSM_GROUPS, 1)), 0.0)
        return jnp.stack([dense[(j // 2) * LANES:(j // 2 + 1) * LANES, j * SCAN_COLS:(j + 1) * SCAN_COLS]
                          for j in range(SCAN_BLOCKS)]).astype(BF16)

    def out_slabs(c):
        c_t = c.transpose(0, 2, 1).reshape(N_STATES, SSM_GROUP)
        dense = jnp.where(out_mask, jnp.tile(c_t, (1, SSM_GROUPS)), 0.0)
        return jnp.stack([dense[j * SCAN_COLS:(j + 1) * SCAN_COLS, (j // 2) * LANES:(j // 2 + 1) * LANES]
                          for j in range(SCAN_BLOCKS)]).astype(BF16)

    return in_slabs(bb_re_t), in_slabs(bb_im_t), out_slabs(c_re), out_slabs(-c_im)


def kernel(x, meta_tokens, norm_ffn1, w_ffn1_in, w_ffn1_out, norm_mix, w_in, ssm_a_re_fwd, ssm_a_im_fwd, ssm_log_dt_fwd, ssm_a_re_bwd, ssm_a_im_bwd, ssm_log_dt_bwd, ssm_b_re, ssm_b_im, ssm_c_re, ssm_c_im, ssm_d, w_glu, na_rpb, w_branch_ssm, w_branch_na, w_out, norm_ffn2, w_ffn2_in, w_ffn2_out, norm_final):
    batch, seq, d_model = x.shape
    assert d_model == D_MODEL and norm_ffn1.shape[0] == 1, "single-layer block of width 1024"
    assert seq % ROW_TILE == 0 and seq % (GRID_W * NA_Q_ROWS) == 0 and batch % 8 == 0
    rows = seq // GRID_W
    assert rows >= NA_K_ROWS + NA_Q_ROWS

    row = lambda a: a.reshape(1, -1).astype(F32)
    g1, gm, g2, gf = row(norm_ffn1[0]), row(norm_mix[0]), row(norm_ffn2[0]), row(norm_final)
    w1_in, w1_out = w_ffn1_in[0].astype(BF16), w_ffn1_out[0].astype(BF16)
    w2_in, w2_out = w_ffn2_in[0].astype(BF16), w_ffn2_out[0].astype(BF16)
    w_uqkv = w_in[0][:, :UQKV_WIDTH].astype(BF16)
    w_gate = w_in[0][:, UQKV_WIDTH:].astype(BF16)

    n_rows = batch * seq
    tok_spec = pl.BlockSpec((ROW_TILE, D_MODEL), lambda i: (i, 0))
    h1 = _ffn_call(x.reshape(n_rows, D_MODEL), g1, w1_in, w1_out, gf, grid=(n_rows // ROW_TILE,),
                   x_spec=tok_spec, o_spec=tok_spec, out_shape=(n_rows, D_MODEL), final_norm=False)
    meta_spec = pl.BlockSpec((N_META, D_MODEL), lambda i: (0, 0))
    h1_meta = _ffn_call(meta_tokens.astype(F32), g1, w1_in, w1_out, gf, grid=(1,),
                        x_spec=meta_spec, o_spec=meta_spec, out_shape=(N_META, D_MODEL),
                        final_norm=False)

    u, q, k, v = _proj_call(h1, gm, w_uqkv, ROW_TILE)
    u_meta, _, k_meta, v_meta = _proj_call(h1_meta, gm, w_uqkv, N_META)
    u = u.reshape(batch, seq, SSM_WIDTH).transpose(1, 0, 2).reshape(n_rows, SSM_WIDTH)

    states = lambda a: a[0].reshape(1, N_STATES).astype(F32)
    per_state = lambda a: jnp.repeat(a[0].astype(F32), SSM_STATE).reshape(1, N_STATES)
    b_re_t = ssm_b_re[0].reshape(N_STATES, SSM_GROUP).T.astype(F32)
    b_im_t = ssm_b_im[0].reshape(N_STATES, SSM_GROUP).T.astype(F32)
    lbf_re, lbf_im, bbf_re, bbf_im = _disc_call(
        states(ssm_a_re_fwd), states(ssm_a_im_fwd), per_state(ssm_log_dt_fwd), b_re_t, b_im_t)
    lbb_re, lbb_im, bbb_re, bbb_im = _disc_call(
        states(ssm_a_re_bwd), states(ssm_a_im_bwd), per_state(ssm_log_dt_bwd), b_re_t, b_im_t)
    wf_re, wf_im, c_re, c_im = _s5_weights(bbf_re, bbf_im, ssm_c_re[0], ssm_c_im[0])
    wb_re, wb_im, _, _ = _s5_weights(bbb_re, bbb_im, ssm_c_re[0], ssm_c_im[0])

    zero_state = jnp.zeros((batch, 2 * N_STATES), F32)
    u_meta_rows = jnp.repeat(u_meta, batch, axis=0)
    _, state_meta = _s5_call(u_meta_rows, zero_state, lbf_re, lbf_im, wf_re, wf_im, c_re, c_im,
                             None, tile=N_META * batch, reverse=False)
    y_fwd, _ = _s5_call(u, state_meta, lbf_re, lbf_im, wf_re, wf_im, c_re, c_im,
                        None, tile=ROW_TILE, reverse=False)
    z, _ = _s5_call(u, zero_state, lbb_re, lbb_im, wb_re, wb_im, c_re, c_im,
                    (y_fwd, row(ssm_d[0]), w_glu[0].astype(BF16)), tile=ROW_TILE, reverse=True)

    bias = _na_bias(na_rpb[0], rows)
    bias = bias.reshape(NA_HEADS // 2, 2, 3, NA_Q, NA_KEYS).transpose(0, 2, 1, 3, 4)
    pad_meta = lambda a: jnp.pad(a, ((0, NA_META_PAD - N_META), (0, 0)))
    att = _na_call(q, k, v, pad_meta(k_meta), pad_meta(v_meta), bias, batch=batch, seq=seq)

    z = z.reshape(seq, batch, SSM_WIDTH).transpose(1, 0, 2).reshape(n_rows, SSM_WIDTH)
    h2 = _merge_call(h1, z, att, gm, w_gate,
                     w_branch_ssm[0].astype(BF16), w_branch_na[0].astype(BF16),
                     w_out[0].astype(BF16))

    out = _ffn_call(h2, g2, w2_in, w2_out, gf, grid=(n_rows // ROW_TILE,), x_spec=tok_spec,
                    o_spec=tok_spec, out_shape=(n_rows, D_MODEL), final_norm=True)
    return out.reshape(batch, seq, D_MODEL)
```

```python
import functools
import math

import jax
import jax.numpy as jnp
from jax import lax
from jax.experimental import pallas as pl
from jax.experimental.pallas import tpu as pltpu

D_MODEL = 1024
N_META = 16
GRID_W = 64
D_FF = 2816
SSM_WIDTH = 512
SSM_GROUP = 16
SSM_GROUPS = 32
SSM_STATE = 64
N_STATES = SSM_GROUPS * SSM_STATE
NA_HEAD_DIM = 64
NA_WIDTH = 512
NA_HEADS = 8
WIN_H = 8
WIN_W = 16
UQKV_WIDTH = SSM_WIDTH + 3 * NA_WIDTH
RMS_EPS = 1e-6
A_RE_MAX = -1e-4
MASK_VALUE = -1e30

LANES = 128
MXU_COLS = 256
VMEM_LIMIT = 56 * 1024 * 1024

ROW_TILE = 512
FF_CHUNK = 1408
SCAN_COLS = 256
NA_Q_ROWS = 4
NA_K_ROWS = 12
NA_META_PAD = 128

BF16 = jnp.bfloat16
F32 = jnp.float32


def _const_spec(shape):
    zeros = (0,) * len(shape)
    return pl.BlockSpec(shape, lambda *_: zeros, pipeline_mode=pl.Buffered(1))


def _params(n_grid_dims):
    return pltpu.CompilerParams(
        dimension_semantics=("arbitrary",) * n_grid_dims,
        vmem_limit_bytes=VMEM_LIMIT)


def _rms_norm(x, g):
    ms = jnp.mean(x * x, axis=-1, keepdims=True)
    return x * lax.rsqrt(ms + RMS_EPS) * g


def _dot(a, b):
    return jnp.dot(a, b, preferred_element_type=F32)


def _ffn_body(x_ref, g_ref, w_in_ref, w_out_ref, gf_ref, o_ref, *, final_norm):
    x = x_ref[...]
    hn = _rms_norm(x, g_ref[...]).astype(BF16)
    acc = jnp.zeros(x.shape, F32)
    for c in range(D_FF // FF_CHUNK):
        lo = c * FF_CHUNK
        gate = _dot(hn, w_in_ref[:, lo:lo + FF_CHUNK])
        up = _dot(hn, w_in_ref[:, D_FF + lo:D_FF + lo + FF_CHUNK])
        act = (jax.nn.silu(gate) * up).astype(BF16)
        acc = acc + _dot(act, w_out_ref[lo:lo + FF_CHUNK, :])
    y = x + 0.5 * acc
    if final_norm:
        y = _rms_norm(y, gf_ref[...])
    o_ref[...] = y


def _ffn_call(x, g, w_in, w_out, gf, *, grid, x_spec, o_spec, out_shape, final_norm):
    return pl.pallas_call(
        functools.partial(_ffn_body, final_norm=final_norm),
        grid=grid,
        in_specs=[x_spec, _const_spec((1, D_MODEL)), _const_spec((D_MODEL, 2 * D_FF)),
                  _const_spec((D_FF, D_MODEL)), _const_spec((1, D_MODEL))],
        out_specs=o_spec,
        out_shape=jax.ShapeDtypeStruct(out_shape, F32),
        compiler_params=_params(len(grid)),
        name="ffn",
    )(x, g, w_in, w_out, gf)


def _proj_body(h_ref, g_ref, w_ref, u_ref, q_ref, k_ref, v_ref):
    hn = _rms_norm(h_ref[...], g_ref[...]).astype(BF16)
    p = _dot(hn, w_ref[...])
    u_ref[...] = p[:, :SSM_WIDTH]
    q_ref[...] = (p[:, SSM_WIDTH:SSM_WIDTH + NA_WIDTH] * (NA_HEAD_DIM ** -0.5)).astype(BF16)
    k_ref[...] = p[:, SSM_WIDTH + NA_WIDTH:SSM_WIDTH + 2 * NA_WIDTH].astype(BF16)
    v_ref[...] = p[:, SSM_WIDTH + 2 * NA_WIDTH:].astype(BF16)


def _proj_call(h, g, w, tile):
    rows = h.shape[0]
    row_spec = lambda width: pl.BlockSpec((tile, width), lambda i: (i, 0))
    return pl.pallas_call(
        _proj_body,
        grid=(rows // tile,),
        in_specs=[row_spec(D_MODEL), _const_spec((1, D_MODEL)), _const_spec((D_MODEL, UQKV_WIDTH))],
        out_specs=[row_spec(SSM_WIDTH)] * 4,
        out_shape=[jax.ShapeDtypeStruct((rows, SSM_WIDTH), F32)]
        + [jax.ShapeDtypeStruct((rows, NA_WIDTH), BF16)] * 3,
        compiler_params=_params(1),
        name="proj",
    )(h, g, w)


def _disc_body(a_re_ref, a_im_ref, log_dt_ref, b_re_ref, b_im_ref,
               lb_re_ref, lb_im_ref, bb_re_ref, bb_im_ref):
    dt = jnp.exp(log_dt_ref[...])
    a_re = jnp.minimum(a_re_ref[...], A_RE_MAX)
    a_im = a_im_ref[...]
    mag = jnp.exp(a_re * dt)
    lb_re = mag * jnp.cos(a_im * dt)
    lb_im = mag * jnp.sin(a_im * dt)
    den = a_re * a_re + a_im * a_im
    nr = lb_re - 1.0
    ni = lb_im
    f_re = (nr * a_re + ni * a_im) / den
    f_im = (ni * a_re - nr * a_im) / den
    lb_re_ref[...] = lb_re
    lb_im_ref[...] = lb_im
    bb_re_ref[...] = f_re * b_re_ref[...] - f_im * b_im_ref[...]
    bb_im_ref[...] = f_re * b_im_ref[...] + f_im * b_re_ref[...]


def _disc_call(a_re, a_im, log_dt, b_re_t, b_im_t):
    row = jax.ShapeDtypeStruct((1, N_STATES), F32)
    mat = jax.ShapeDtypeStruct((SSM_GROUP, N_STATES), F32)
    return pl.pallas_call(_disc_body, out_shape=[row, row, mat, mat], name="disc")(
        a_re, a_im, log_dt, b_re_t, b_im_t)


SCAN_BLOCKS = N_STATES // SCAN_COLS
Y_COLS = LANES


def _s5_body(*refs, reverse, steps, finish):
    if finish:
        (u_ref, init_ref, lb_re_ref, lb_im_ref, w_re_ref, w_im_ref, c_re_ref, c_im_ref,
         yf_ref, d_ref, w_glu_ref, o_ref, fin_ref, bu_re, bu_im, h_re, h_im) = refs
    else:
        (u_ref, init_ref, lb_re_ref, lb_im_ref, w_re_ref, w_im_ref, c_re_ref, c_im_ref,
         o_ref, fin_ref, bu_re, bu_im, h_re, h_im) = refs
    nb = init_ref.shape[0]

    @pl.when(pl.program_id(0) == 0)
    def _():
        h_re[...] = init_ref[:, :N_STATES]
        h_im[...] = init_ref[:, N_STATES:]

    def scan_order(ref):
        if not reverse:
            return ref[...]
        return jnp.concatenate(
            [ref[(steps - 1 - s) * nb:(steps - s) * nb, :] for s in range(steps)], axis=0)

    u = scan_order(u_ref)
    ub = u.astype(BF16)
    y_parts = []
    for j in range(SCAN_BLOCKS):
        cols = slice(j * SCAN_COLS, (j + 1) * SCAN_COLS)
        us = ub[:, (j // 2) * LANES:(j // 2 + 1) * LANES]
        bu_re[j] = _dot(us, w_re_ref[j])
        bu_im[j] = _dot(us, w_im_ref[j])
        lr = jnp.broadcast_to(lb_re_ref[:, cols], (nb, SCAN_COLS))
        li = jnp.broadcast_to(lb_im_ref[:, cols], (nb, SCAN_COLS))
        hr, hi = h_re[:, cols], h_im[:, cols]
        for s in range(steps):
            rows = slice(s * nb, (s + 1) * nb)
            hr, hi = (lr * hr - li * hi + bu_re[j, rows, :],
                      lr * hi + li * hr + bu_im[j, rows, :])
            bu_re[j, rows, :] = hr
            bu_im[j, rows, :] = hi
        h_re[:, cols] = hr
        h_im[:, cols] = hi
        part = (_dot(bu_re[j].astype(BF16), c_re_ref[j])
                + _dot(bu_im[j].astype(BF16), c_im_ref[j]))
        if j % 2 == 0:
            y_parts.append(part)
        else:
            y_parts[-1] = y_parts[-1] + part
    y = jnp.concatenate(y_parts, axis=-1)
    fin_ref[:, :N_STATES] = h_re[...]
    fin_ref[:, N_STATES:] = h_im[...]
    if finish:
        y = y + scan_order(yf_ref) + d_ref[...] * u
        z = jax.nn.gelu(y)
        y = (z * jax.nn.sigmoid(_dot(z.astype(BF16), w_glu_ref[...]))).astype(BF16)
    if reverse:
        for s in range(steps):
            o_ref[(steps - 1 - s) * nb:(steps - s) * nb, :] = y[s * nb:(s + 1) * nb, :]
    else:
        o_ref[...] = y


def _s5_call(u, init, lb_re, lb_im, w_re, w_im, c_re, c_im, finish_args, *, tile, reverse):
    rows = u.shape[0]
    nb = init.shape[0]
    n_tiles = rows // tile
    finish = finish_args is not None
    row_map = (lambda i: (n_tiles - 1 - i, 0)) if reverse else (lambda i: (i, 0))
    row_spec = pl.BlockSpec((tile, SSM_WIDTH), row_map)
    in_specs = [row_spec, _const_spec((nb, 2 * N_STATES)),
                _const_spec((1, N_STATES)), _const_spec((1, N_STATES)),
                _const_spec((SCAN_BLOCKS, LANES, SCAN_COLS)), _const_spec((SCAN_BLOCKS, LANES, SCAN_COLS)),
                _const_spec((SCAN_BLOCKS, SCAN_COLS, Y_COLS)), _const_spec((SCAN_BLOCKS, SCAN_COLS, Y_COLS))]
    args = [u, init, lb_re, lb_im, w_re, w_im, c_re, c_im]
    if finish:
        in_specs += [row_spec, _const_spec((1, SSM_WIDTH)), _const_spec((SSM_WIDTH, SSM_WIDTH))]
        args += list(finish_args)
    return pl.pallas_call(
        functools.partial(_s5_body, reverse=reverse, steps=tile // nb, finish=finish),
        grid=(n_tiles,),
        in_specs=in_specs,
        out_specs=[row_spec, pl.BlockSpec((nb, 2 * N_STATES), lambda i: (0, 0))],
        out_shape=[jax.ShapeDtypeStruct((rows, SSM_WIDTH), BF16 if finish else F32),
                   jax.ShapeDtypeStruct((nb, 2 * N_STATES), F32)],
        scratch_shapes=[pltpu.VMEM((SCAN_BLOCKS, tile, SCAN_COLS), F32),
                        pltpu.VMEM((SCAN_BLOCKS, tile, SCAN_COLS), F32),
                        pltpu.VMEM((nb, N_STATES), F32), pltpu.VMEM((nb, N_STATES), F32)],
        compiler_params=_params(1),
        name="s5_bwd" if reverse else "s5_fwd",
    )(*args)


NA_Q = NA_Q_ROWS * GRID_W
NA_K = NA_K_ROWS * GRID_W
NA_KEYS = NA_K + NA_META_PAD


def _na_body(q_ref, k_ref, v_ref, km_ref, vm_ref, bias_ref, o_ref, *, n_blocks, rows):
    first_head = lax.broadcasted_iota(jnp.int32, (1, LANES), 1) < NA_HEAD_DIM
    km = km_ref[...]
    vm = vm_ref[...]

    def block(i, carry):
        key_row = jnp.clip(i * NA_Q_ROWS - WIN_H // 2, 0, rows - NA_K_ROWS)
        k0 = pl.multiple_of(key_row * GRID_W, GRID_W)
        q0 = pl.multiple_of(i * NA_Q, NA_Q)
        kind = jnp.where(i == 0, 0, jnp.where(i == n_blocks - 1, 2, 1))
        q = q_ref[pl.ds(q0, NA_Q), :]
        zero = jnp.zeros_like(q)
        qs = jnp.concatenate([jnp.where(first_head, q, zero), jnp.where(first_head, zero, q)], axis=0)
        keys = jnp.concatenate([k_ref[pl.ds(k0, NA_K), :], km], axis=0)
        vals = jnp.concatenate([v_ref[pl.ds(k0, NA_K), :], vm], axis=0)
        s = lax.dot_general(qs, keys, (((1,), (1,)), ((), ())), preferred_element_type=F32)
        s = s + bias_ref[kind].reshape(2 * NA_Q, NA_KEYS)
        m = jnp.max(s, axis=-1, keepdims=True)
        p = jnp.exp(s - m)
        l = jnp.sum(p, axis=-1, keepdims=True)
        o = _dot(p.astype(BF16), vals) / l
        o_ref[pl.ds(q0, NA_Q), :] = jnp.where(first_head, o[:NA_Q], o[NA_Q:]).astype(BF16)
        return carry

    lax.fori_loop(0, n_blocks, block, 0, unroll=2)


def _na_call(q, k, v, km, vm, bias, *, batch, seq):
    rows = seq // GRID_W
    n_pairs = NA_HEADS // 2
    n_blocks = rows // NA_Q_ROWS
    seq_spec = pl.BlockSpec((seq, LANES), lambda hp, b: (b, hp))
    meta_spec = pl.BlockSpec((NA_META_PAD, LANES), lambda hp, b: (0, hp))
    bias_spec = pl.BlockSpec((None, 3, 2, NA_Q, NA_KEYS), lambda hp, b: (hp, 0, 0, 0, 0))
    return pl.pallas_call(
        functools.partial(_na_body, n_blocks=n_blocks, rows=rows),
        grid=(n_pairs, batch),
        in_specs=[seq_spec, seq_spec, seq_spec, meta_spec, meta_spec, bias_spec],
        out_specs=seq_spec,
        out_shape=jax.ShapeDtypeStruct((batch * seq, NA_WIDTH), BF16),
        compiler_params=_params(2),
        name="na",
    )(q, k, v, km, vm, bias)


def _na_bias(rpb, rows):
    kh = min(WIN_H, rows)
    cols = jnp.arange(GRID_W)
    col_start = jnp.clip(cols - WIN_W // 2, 0, GRID_W - WIN_W)
    col_valid = (cols[None, :] >= col_start[:, None]) & (cols[None, :] < col_start[:, None] + WIN_W)
    col_idx = jnp.clip(cols[None, :] - cols[:, None] + WIN_W - 1, 0, 2 * WIN_W - 2)
    rpb_c = jnp.take(rpb.astype(F32), col_idx, axis=2)
    rpb_c = jnp.where(col_valid[None, None], rpb_c, MASK_VALUE)
    n_blocks = rows // NA_Q_ROWS
    kinds = []
    for blk in (0, 1, n_blocks - 1):
        q_rows = blk * NA_Q_ROWS + jnp.arange(NA_Q_ROWS)
        key_row0 = min(max(blk * NA_Q_ROWS - WIN_H // 2, 0), rows - NA_K_ROWS)
        k_rows = key_row0 + jnp.arange(NA_K_ROWS)
        row_start = jnp.clip(q_rows - kh // 2, 0, rows - kh)
        row_valid = (k_rows[None, :] >= row_start[:, None]) & (k_rows[None, :] < row_start[:, None] + kh)
        row_idx = jnp.clip(k_rows[None, :] - q_rows[:, None] + WIN_H - 1, 0, 2 * WIN_H - 2)
        b = rpb_c[:, row_idx]
        b = jnp.where(row_valid[None, :, :, None, None], b, MASK_VALUE)
        b = b.transpose(0, 1, 3, 2, 4).reshape(NA_HEADS, NA_Q, NA_K)
        meta = jnp.where(jnp.arange(NA_META_PAD) < N_META, 0.0, MASK_VALUE)
        meta = jnp.broadcast_to(meta, (NA_HEADS, NA_Q, NA_META_PAD))
        kinds.append(jnp.concatenate([b, meta], axis=-1))
    return jnp.stack(kinds, axis=1)


def _merge_body(h_ref, z_ref, a_ref, g_ref, w_gate_ref, w_ssm_ref, w_na_ref, w_out_ref, o_ref):
    h = h_ref[...]
    hn = _rms_norm(h, g_ref[...]).astype(BF16)
    gates = _dot(hn, w_gate_ref[...])
    y_ssm = _dot(z_ref[...], w_ssm_ref[...])
    y_na = _dot(a_ref[...], w_na_ref[...])
    merged = (jax.nn.sigmoid(gates[:, :D_MODEL]) * y_ssm
              + jax.nn.sigmoid(gates[:, D_MODEL:]) * y_na)
    o_ref[...] = h + _dot(merged.astype(BF16), w_out_ref[...])


def _merge_call(h1, z, a, g, w_gate, w_ssm, w_na, w_out):
    rows = h1.shape[0]
    row_spec = lambda width: pl.BlockSpec((ROW_TILE, width), lambda i: (i, 0))
    return pl.pallas_call(
        _merge_body,
        grid=(rows // ROW_TILE,),
        in_specs=[row_spec(D_MODEL), row_spec(SSM_WIDTH), row_spec(NA_WIDTH),
                  _const_spec((1, D_MODEL)), _const_spec((D_MODEL, 2 * D_MODEL)),
                  _const_spec((SSM_WIDTH, D_MODEL)), _const_spec((NA_WIDTH, D_MODEL)),
                  _const_spec((D_MODEL, D_MODEL))],
        out_specs=row_spec(D_MODEL),
        out_shape=jax.ShapeDtypeStruct((rows, D_MODEL), F32),
        compiler_params=_params(1),
        name="merge",
    )(h1, z, a, g, w_gate, w_ssm, w_na, w_out)


def _group_mask(n_rows_per_group, n_cols_per_group):
    r = jnp.arange(SSM_GROUPS * n_rows_per_group)[:, None] // n_rows_per_group
    c = jnp.arange(SSM_GROUPS * n_cols_per_group)[None, :] // n_cols_per_group
    return r == c


def _s5_weights(bb_re_t, bb_im_t, c_re, c_im):
    in_mask = _group_mask(SSM_GROUP, SSM_STATE)
    out_mask = in_mask.T

    def in_slabs(bb_t):
        dense = jnp.where(in_mask, jnp.tile(bb_t, (SSM_GROUPS, 1)), 0.0)
        return jnp.stack([dense[(j // 2) * LANES:(j // 2 + 1) * LANES, j * SCAN_COLS:(j + 1) * SCAN_COLS]
                          for j in range(SCAN_BLOCKS)]).astype(BF16)

    def out_slabs(c):
        c_t = c.transpose(0, 2, 1).reshape(N_STATES, SSM_GROUP)
        dense = jnp.where(out_mask, jnp.tile(c_t, (1, SSM_GROUPS)), 0.0)
        return jnp.stack([dense[j * SCAN_COLS:(j + 1) * SCAN_COLS, (j // 2) * LANES:(j // 2 + 1) * LANES]
                          for j in range(SCAN_BLOCKS)]).astype(BF16)

    return in_slabs(bb_re_t), in_slabs(bb_im_t), out_slabs(c_re), out_slabs(-c_im)


def kernel(x, meta_tokens, norm_ffn1, w_ffn1_in, w_ffn1_out, norm_mix, w_in, ssm_a_re_fwd, ssm_a_im_fwd, ssm_log_dt_fwd, ssm_a_re_bwd, ssm_a_im_bwd, ssm_log_dt_bwd, ssm_b_re, ssm_b_im, ssm_c_re, ssm_c_im, ssm_d, w_glu, na_rpb, w_branch_ssm, w_branch_na, w_out, norm_ffn2, w_ffn2_in, w_ffn2_out, norm_final):
    batch, seq, d_model = x.shape
    assert d_model == D_MODEL and norm_ffn1.shape[0] == 1, "single-layer block of width 1024"
    assert seq % ROW_TILE == 0 and seq % (GRID_W * NA_Q_ROWS) == 0 and batch % 8 == 0
    rows = seq // GRID_W
    assert rows >= NA_K_ROWS + NA_Q_ROWS

    row = lambda a: a.reshape(1, -1).astype(F32)
    g1, gm, g2, gf = row(norm_ffn1[0]), row(norm_mix[0]), row(norm_ffn2[0]), row(norm_final)
    w1_in, w1_out = w_ffn1_in[0].astype(BF16), w_ffn1_out[0].astype(BF16)
    w2_in, w2_out = w_ffn2_in[0].astype(BF16), w_ffn2_out[0].astype(BF16)
    w_uqkv = w_in[0][:, :UQKV_WIDTH].astype(BF16)
    w_gate = w_in[0][:, UQKV_WIDTH:].astype(BF16)

    n_rows = batch * seq
    tok_spec = pl.BlockSpec((ROW_TILE, D_MODEL), lambda i: (i, 0))
    h1 = _ffn_call(x.reshape(n_rows, D_MODEL), g1, w1_in, w1_out, gf, grid=(n_rows // ROW_TILE,),
                   x_spec=tok_spec, o_spec=tok_spec, out_shape=(n_rows, D_MODEL), final_norm=False)
    meta_spec = pl.BlockSpec((N_META, D_MODEL), lambda i: (0, 0))
    h1_meta = _ffn_call(meta_tokens.astype(F32), g1, w1_in, w1_out, gf, grid=(1,),
                        x_spec=meta_spec, o_spec=meta_spec, out_shape=(N_META, D_MODEL),
                        final_norm=False)

    u, q, k, v = _proj_call(h1, gm, w_uqkv, ROW_TILE)
    u_meta, _, k_meta, v_meta = _proj_call(h1_meta, gm, w_uqkv, N_META)
    u = u.reshape(batch, seq, SSM_WIDTH).transpose(1, 0, 2).reshape(n_rows, SSM_WIDTH)

    states = lambda a: a[0].reshape(1, N_STATES).astype(F32)
    per_state = lambda a: jnp.repeat(a[0].astype(F32), SSM_STATE).reshape(1, N_STATES)
    b_re_t = ssm_b_re[0].reshape(N_STATES, SSM_GROUP).T.astype(F32)
    b_im_t = ssm_b_im[0].reshape(N_STATES, SSM_GROUP).T.astype(F32)
    lbf_re, lbf_im, bbf_re, bbf_im = _disc_call(
        states(ssm_a_re_fwd), states(ssm_a_im_fwd), per_state(ssm_log_dt_fwd), b_re_t, b_im_t)
    lbb_re, lbb_im, bbb_re, bbb_im = _disc_call(
        states(ssm_a_re_bwd), states(ssm_a_im_bwd), per_state(ssm_log_dt_bwd), b_re_t, b_im_t)
    wf_re, wf_im, c_re, c_im = _s5_weights(bbf_re, bbf_im, ssm_c_re[0], ssm_c_im[0])
    wb_re, wb_im, _, _ = _s5_weights(bbb_re, bbb_im, ssm_c_re[0], ssm_c_im[0])

    zero_state = jnp.zeros((batch, 2 * N_STATES), F32)
    u_meta_rows = jnp.repeat(u_meta, batch, axis=0)
    _, state_meta = _s5_call(u_meta_rows, zero_state, lbf_re, lbf_im, wf_re, wf_im, c_re, c_im,
                             None, tile=N_META * batch, reverse=False)
    y_fwd, _ = _s5_call(u, state_meta, lbf_re, lbf_im, wf_re, wf_im, c_re, c_im,
                        None, tile=ROW_TILE, reverse=False)
    z, _ = _s5_call(u, zero_state, lbb_re, lbb_im, wb_re, wb_im, c_re, c_im,
                    (y_fwd, row(ssm_d[0]), w_glu[0].astype(BF16)), tile=ROW_TILE, reverse=True)

    bias = _na_bias(na_rpb[0], rows)
    bias = bias.reshape(NA_HEADS // 2, 2, 3, NA_Q, NA_KEYS).transpose(0, 2, 1, 3, 4)
    pad_meta = lambda a: jnp.pad(a, ((0, NA_META_PAD - N_META), (0, 0)))
    att = _na_call(q, k, v, pad_meta(k_meta), pad_meta(v_meta), bias, batch=batch, seq=seq)

    z = z.reshape(seq, batch, SSM_WIDTH).transpose(1, 0, 2).reshape(n_rows, SSM_WIDTH)
    h2 = _merge_call(h1, z, att, gm, w_gate,
                     w_branch_ssm[0].astype(BF16), w_branch_na[0].astype(BF16),
                     w_out[0].astype(BF16))

    out = _ffn_call(h2, g2, w2_in, w2_out, gf, grid=(n_rows // ROW_TILE,), x_spec=tok_spec,
                    o_spec=tok_spec, out_shape=(n_rows, D_MODEL), final_norm=True)
    return out.reshape(batch, seq, D_MODEL)
```

```python
import functools
import math

import jax
import jax.numpy as jnp
from jax import lax
from jax.experimental import pallas as pl
from jax.experimental.pallas import tpu as pltpu

D_MODEL = 1024
N_META = 16
GRID_W = 64
D_FF = 2816
SSM_WIDTH = 512
SSM_GROUP = 16
SSM_GROUPS = 32
SSM_STATE = 64
N_STATES = SSM_GROUPS * SSM_STATE
NA_HEAD_DIM = 64
NA_WIDTH = 512
NA_HEADS = 8
WIN_H = 8
WIN_W = 16
UQKV_WIDTH = SSM_WIDTH + 3 * NA_WIDTH
RMS_EPS = 1e-6
A_RE_MAX = -1e-4
MASK_VALUE = -1e30

LANES = 128
MXU_COLS = 256
VMEM_LIMIT = 56 * 1024 * 1024

ROW_TILE = 512
FF_CHUNK = 2816
SCAN_COLS = 256
NA_Q_ROWS = 4
NA_K_ROWS = NA_Q_ROWS + WIN_H - 1
NA_META_PAD = GRID_W

BF16 = jnp.bfloat16
F32 = jnp.float32


def _const_spec(shape):
    zeros = (0,) * len(shape)
    return pl.BlockSpec(shape, lambda *_: zeros, pipeline_mode=pl.Buffered(1))


def _params(n_grid_dims):
    return pltpu.CompilerParams(
        dimension_semantics=("arbitrary",) * n_grid_dims,
        vmem_limit_bytes=VMEM_LIMIT)


def _rms_norm(x, g):
    ms = jnp.mean(x * x, axis=-1, keepdims=True)
    return x * lax.rsqrt(ms + RMS_EPS) * g


def _dot(a, b):
    return jnp.dot(a, b, preferred_element_type=F32)


def _ffn_body(x_ref, g_ref, w_in_ref, w_out_ref, gf_ref, o_ref, *, final_norm):
    x = x_ref[...]
    hn = _rms_norm(x, g_ref[...]).astype(BF16)
    acc = jnp.zeros(x.shape, F32)
    for c in range(D_FF // FF_CHUNK):
        lo = c * FF_CHUNK
        gate = _dot(hn, w_in_ref[:, lo:lo + FF_CHUNK])
        up = _dot(hn, w_in_ref[:, D_FF + lo:D_FF + lo + FF_CHUNK])
        act = (jax.nn.silu(gate) * up).astype(BF16)
        acc = acc + _dot(act, w_out_ref[lo:lo + FF_CHUNK, :])
    y = x + 0.5 * acc
    if final_norm:
        y = _rms_norm(y, gf_ref[...])
    o_ref[...] = y


def _ffn_call(x, g, w_in, w_out, gf, *, grid, x_spec, o_spec, out_shape, final_norm):
    return pl.pallas_call(
        functools.partial(_ffn_body, final_norm=final_norm),
        grid=grid,
        in_specs=[x_spec, _const_spec((1, D_MODEL)), _const_spec((D_MODEL, 2 * D_FF)),
                  _const_spec((D_FF, D_MODEL)), _const_spec((1, D_MODEL))],
        out_specs=o_spec,
        out_shape=jax.ShapeDtypeStruct(out_shape, F32),
        compiler_params=_params(len(grid)),
        name="ffn",
    )(x, g, w_in, w_out, gf)


def _proj_body(h_ref, g_ref, w_ref, u_ref, q_ref, k_ref, v_ref):
    hn = _rms_norm(h_ref[...], g_ref[...]).astype(BF16)
    p = _dot(hn, w_ref[...])
    u_ref[...] = p[:, :SSM_WIDTH]
    q_ref[...] = (p[:, SSM_WIDTH:SSM_WIDTH + NA_WIDTH] * (NA_HEAD_DIM ** -0.5)).astype(BF16)
    k_ref[...] = p[:, SSM_WIDTH + NA_WIDTH:SSM_WIDTH + 2 * NA_WIDTH].astype(BF16)
    v_ref[...] = p[:, SSM_WIDTH + 2 * NA_WIDTH:].astype(BF16)


def _proj_call(h, g, w, tile):
    rows = h.shape[0]
    row_spec = lambda width: pl.BlockSpec((tile, width), lambda i: (i, 0))
    return pl.pallas_call(
        _proj_body,
        grid=(rows // tile,),
        in_specs=[row_spec(D_MODEL), _const_spec((1, D_MODEL)), _const_spec((D_MODEL, UQKV_WIDTH))],
        out_specs=[row_spec(SSM_WIDTH)] * 4,
        out_shape=[jax.ShapeDtypeStruct((rows, SSM_WIDTH), F32)]
        + [jax.ShapeDtypeStruct((rows, NA_WIDTH), BF16)] * 3,
        compiler_params=_params(1),
        name="proj",
    )(h, g, w)


def _disc_body(a_re_ref, a_im_ref, log_dt_ref, b_re_ref, b_im_ref,
               lb_re_ref, lb_im_ref, bb_re_ref, bb_im_ref):
    dt = jnp.exp(log_dt_ref[...])
    a_re = jnp.minimum(a_re_ref[...], A_RE_MAX)
    a_im = a_im_ref[...]
    mag = jnp.exp(a_re * dt)
    lb_re = mag * jnp.cos(a_im * dt)
    lb_im = mag * jnp.sin(a_im * dt)
    den = a_re * a_re + a_im * a_im
    nr = lb_re - 1.0
    ni = lb_im
    f_re = (nr * a_re + ni * a_im) / den
    f_im = (ni * a_re - nr * a_im) / den
    lb_re_ref[...] = lb_re
    lb_im_ref[...] = lb_im
    bb_re_ref[...] = f_re * b_re_ref[...] - f_im * b_im_ref[...]
    bb_im_ref[...] = f_re * b_im_ref[...] + f_im * b_re_ref[...]


def _disc_call(a_re, a_im, log_dt, b_re_t, b_im_t):
    row = jax.ShapeDtypeStruct((1, N_STATES), F32)
    mat = jax.ShapeDtypeStruct((SSM_GROUP, N_STATES), F32)
    return pl.pallas_call(_disc_body, out_shape=[row, row, mat, mat], name="disc")(
        a_re, a_im, log_dt, b_re_t, b_im_t)


SCAN_BLOCKS = N_STATES // SCAN_COLS
Y_COLS = LANES


def _s5_body(*refs, reverse, steps, finish):
    if finish:
        (u_ref, init_ref, lb_re_ref, lb_im_ref, w_re_ref, w_im_ref, c_re_ref, c_im_ref,
         yf_ref, d_ref, w_glu_ref, o_ref, fin_ref, bu_re, bu_im, h_re, h_im) = refs
    else:
        (u_ref, init_ref, lb_re_ref, lb_im_ref, w_re_ref, w_im_ref, c_re_ref, c_im_ref,
         o_ref, fin_ref, bu_re, bu_im, h_re, h_im) = refs
    nb = init_ref.shape[0]

    @pl.when(pl.program_id(0) == 0)
    def _():
        h_re[...] = init_ref[:, :N_STATES]
        h_im[...] = init_ref[:, N_STATES:]

    def scan_order(ref):
        if not reverse:
            return ref[...]
        return jnp.concatenate(
            [ref[(steps - 1 - s) * nb:(steps - s) * nb, :] for s in range(steps)], axis=0)

    u = scan_order(u_ref)
    ub = u.astype(BF16)
    y_parts = []
    for j in range(SCAN_BLOCKS):
        cols = slice(j * SCAN_COLS, (j + 1) * SCAN_COLS)
        us = ub[:, (j // 2) * LANES:(j // 2 + 1) * LANES]
        bu_re[j] = _dot(us, w_re_ref[j])
        bu_im[j] = _dot(us, w_im_ref[j])
        lr = jnp.broadcast_to(lb_re_ref[:, cols], (nb, SCAN_COLS))
        li = jnp.broadcast_to(lb_im_ref[:, cols], (nb, SCAN_COLS))
        hr, hi = h_re[:, cols], h_im[:, cols]
        for s in range(steps):
            rows = slice(s * nb, (s + 1) * nb)
            hr, hi = (lr * hr - li * hi + bu_re[j, rows, :],
                      lr * hi + li * hr + bu_im[j, rows, :])
            bu_re[j, rows, :] = hr
            bu_im[j, rows, :] = hi
        h_re[:, cols] = hr
        h_im[:, cols] = hi
        part = (_dot(bu_re[j].astype(BF16), c_re_ref[j])
                + _dot(bu_im[j].astype(BF16), c_im_ref[j]))
        if j % 2 == 0:
            y_parts.append(part)
        else:
            y_parts[-1] = y_parts[-1] + part
    y = jnp.concatenate(y_parts, axis=-1)
    fin_ref[:, :N_STATES] = h_re[...]
    fin_ref[:, N_STATES:] = h_im[...]
    if finish:
        y = y + scan_order(yf_ref) + d_ref[...] * u
        z = jax.nn.gelu(y)
        y = (z * jax.nn.sigmoid(_dot(z.astype(BF16), w_glu_ref[...]))).astype(BF16)
    if reverse:
        for s in range(steps):
            o_ref[(steps - 1 - s) * nb:(steps - s) * nb, :] = y[s * nb:(s + 1) * nb, :]
    else:
        o_ref[...] = y


def _s5_call(u, init, lb_re, lb_im, w_re, w_im, c_re, c_im, finish_args, *, tile, reverse):
    rows = u.shape[0]
    nb = init.shape[0]
    n_tiles = rows // tile
    finish = finish_args is not None
    row_map = (lambda i: (n_tiles - 1 - i, 0)) if reverse else (lambda i: (i, 0))
    row_spec = pl.BlockSpec((tile, SSM_WIDTH), row_map)
    in_specs = [row_spec, _const_spec((nb, 2 * N_STATES)),
                _const_spec((1, N_STATES)), _const_spec((1, N_STATES)),
                _const_spec((SCAN_BLOCKS, LANES, SCAN_COLS)), _const_spec((SCAN_BLOCKS, LANES, SCAN_COLS)),
                _const_spec((SCAN_BLOCKS, SCAN_COLS, Y_COLS)), _const_spec((SCAN_BLOCKS, SCAN_COLS, Y_COLS))]
    args = [u, init, lb_re, lb_im, w_re, w_im, c_re, c_im]
    if finish:
        in_specs += [row_spec, _const_spec((1, SSM_WIDTH)), _const_spec((SSM_WIDTH, SSM_WIDTH))]
        args += list(finish_args)
    return pl.pallas_call(
        functools.partial(_s5_body, reverse=reverse, steps=tile // nb, finish=finish),
        grid=(n_tiles,),
        in_specs=in_specs,
        out_specs=[row_spec, pl.BlockSpec((nb, 2 * N_STATES), lambda i: (0, 0))],
        out_shape=[jax.ShapeDtypeStruct((rows, SSM_WIDTH), BF16 if finish else F32),
                   jax.ShapeDtypeStruct((nb, 2 * N_STATES), F32)],
        scratch_shapes=[pltpu.VMEM((SCAN_BLOCKS, tile, SCAN_COLS), F32),
                        pltpu.VMEM((SCAN_BLOCKS, tile, SCAN_COLS), F32),
                        pltpu.VMEM((nb, N_STATES), F32), pltpu.VMEM((nb, N_STATES), F32)],
        compiler_params=_params(1),
        name="s5_bwd" if reverse else "s5_fwd",
    )(*args)


NA_Q = NA_Q_ROWS * GRID_W
NA_K = NA_K_ROWS * GRID_W
NA_KEYS = NA_K + NA_META_PAD


def _na_body(q_ref, k_ref, v_ref, km_ref, vm_ref, bias_ref, o_ref, *, n_blocks, rows):
    first_head = lax.broadcasted_iota(jnp.int32, (1, LANES), 1) < NA_HEAD_DIM
    km = km_ref[...]
    vm = vm_ref[...]

    def block(i, carry):
        key_row = jnp.clip(i * NA_Q_ROWS - WIN_H // 2, 0, rows - NA_K_ROWS)
        k0 = pl.multiple_of(key_row * GRID_W, GRID_W)
        q0 = pl.multiple_of(i * NA_Q, NA_Q)
        kind = jnp.where(i == 0, 0, jnp.where(i == n_blocks - 1, 2, 1))
        q = q_ref[pl.ds(q0, NA_Q), :]
        zero = jnp.zeros_like(q)
        qs = jnp.concatenate([jnp.where(first_head, q, zero), jnp.where(first_head, zero, q)], axis=0)
        keys = jnp.concatenate([k_ref[pl.ds(k0, NA_K), :], km], axis=0)
        vals = jnp.concatenate([v_ref[pl.ds(k0, NA_K), :], vm], axis=0)
        s = lax.dot_general(qs, keys, (((1,), (1,)), ((), ())), preferred_element_type=F32)
        s = s + bias_ref[kind].reshape(2 * NA_Q, NA_KEYS)
        m = jnp.max(s, axis=-1, keepdims=True)
        p = jnp.exp(s - m)
        l = jnp.sum(p, axis=-1, keepdims=True)
        o = _dot(p.astype(BF16), vals) / l
        o_ref[pl.ds(q0, NA_Q), :] = jnp.where(first_head, o[:NA_Q], o[NA_Q:]).astype(BF16)
        return carry

    lax.fori_loop(0, n_blocks, block, 0, unroll=2)


def _na_call(q, k, v, km, vm, bias, *, batch, seq):
    rows = seq // GRID_W
    n_pairs = NA_HEADS // 2
    n_blocks = rows // NA_Q_ROWS
    seq_spec = pl.BlockSpec((seq, LANES), lambda hp, b: (b, hp))
    meta_spec = pl.BlockSpec((NA_META_PAD, LANES), lambda hp, b: (0, hp))
    bias_spec = pl.BlockSpec((None, 3, 2, NA_Q, NA_KEYS), lambda hp, b: (hp, 0, 0, 0, 0))
    return pl.pallas_call(
        functools.partial(_na_body, n_blocks=n_blocks, rows=rows),
        grid=(n_pairs, batch),
        in_specs=[seq_spec, seq_spec, seq_spec, meta_spec, meta_spec, bias_spec],
        out_specs=seq_spec,
        out_shape=jax.ShapeDtypeStruct((batch * seq, NA_WIDTH), BF16),
        compiler_params=_params(2),
        name="na",
    )(q, k, v, km, vm, bias)


def _na_bias(rpb, rows):
    kh = min(WIN_H, rows)
    cols = jnp.arange(GRID_W)
    col_start = jnp.clip(cols - WIN_W // 2, 0, GRID_W - WIN_W)
    col_valid = (cols[None, :] >= col_start[:, None]) & (cols[None, :] < col_start[:, None] + WIN_W)
    col_idx = jnp.clip(cols[None, :] - cols[:, None] + WIN_W - 1, 0, 2 * WIN_W - 2)
    rpb_c = jnp.take(rpb.astype(F32), col_idx, axis=2)
    rpb_c = jnp.where(col_valid[None, None], rpb_c, MASK_VALUE)
    n_blocks = rows // NA_Q_ROWS
    kinds = []
    for blk in (0, 1, n_blocks - 1):
        q_rows = blk * NA_Q_ROWS + jnp.arange(NA_Q_ROWS)
        key_row0 = min(max(blk * NA_Q_ROWS - WIN_H // 2, 0), rows - NA_K_ROWS)
        k_rows = key_row0 + jnp.arange(NA_K_ROWS)
        row_start = jnp.clip(q_rows - kh // 2, 0, rows - kh)
        row_valid = (k_rows[None, :] >= row_start[:, None]) & (k_rows[None, :] < row_start[:, None] + kh)
        row_idx = jnp.clip(k_rows[None, :] - q_rows[:, None] + WIN_H - 1, 0, 2 * WIN_H - 2)
        b = rpb_c[:, row_idx]
        b = jnp.where(row_valid[None, :, :, None, None], b, MASK_VALUE)
        b = b.transpose(0, 1, 3, 2, 4).reshape(NA_HEADS, NA_Q, NA_K)
        meta = jnp.where(jnp.arange(NA_META_PAD) < N_META, 0.0, MASK_VALUE)
        meta = jnp.broadcast_to(meta, (NA_HEADS, NA_Q, NA_META_PAD))
        kinds.append(jnp.concatenate([b, meta], axis=-1))
    return jnp.stack(kinds, axis=1)


def _merge_body(h_ref, z_ref, a_ref, g_ref, w_gate_ref, w_ssm_ref, w_na_ref, w_out_ref, o_ref):
    h = h_ref[...]
    hn = _rms_norm(h, g_ref[...]).astype(BF16)
    gates = _dot(hn, w_gate_ref[...])
    y_ssm = _dot(z_ref[...], w_ssm_ref[...])
    y_na = _dot(a_ref[...], w_na_ref[...])
    merged = (jax.nn.sigmoid(gates[:, :D_MODEL]) * y_ssm
              + jax.nn.sigmoid(gates[:, D_MODEL:]) * y_na)
    o_ref[...] = h + _dot(merged.astype(BF16), w_out_ref[...])


def _merge_call(h1, z, a, g, w_gate, w_ssm, w_na, w_out):
    rows = h1.shape[0]
    row_spec = lambda width: pl.BlockSpec((ROW_TILE, width), lambda i: (i, 0))
    return pl.pallas_call(
        _merge_body,
        grid=(rows // ROW_TILE,),
        in_specs=[row_spec(D_MODEL), row_spec(SSM_WIDTH), row_spec(NA_WIDTH),
                  _const_spec((1, D_MODEL)), _const_spec((D_MODEL, 2 * D_MODEL)),
                  _const_spec((SSM_WIDTH, D_MODEL)), _const_spec((NA_WIDTH, D_MODEL)),
                  _const_spec((D_MODEL, D_MODEL))],
        out_specs=row_spec(D_MODEL),
        out_shape=jax.ShapeDtypeStruct((rows, D_MODEL), F32),
        compiler_params=_params(1),
        name="merge",
    )(h1, z, a, g, w_gate, w_ssm, w_na, w_out)


def _group_mask(n_rows_per_group, n_cols_per_group):
    r = jnp.arange(SSM_GROUPS * n_rows_per_group)[:, None] // n_rows_per_group
    c = jnp.arange(SSM_GROUPS * n_cols_per_group)[None, :] // n_cols_per_group
    return r == c


def _s5_weights(bb_re_t, bb_im_t, c_re, c_im):
    in_mask = _group_mask(SSM_GROUP, SSM_STATE)
    out_mask = in_mask.T

    def in_slabs(bb_t):
        dense = jnp.where(in_mask, jnp.tile(bb_t, (SSM_GROUPS, 1)), 0.0)
        return jnp.stack([dense[(j // 2) * LANES:(j // 2 + 1) * LANES, j * SCAN_COLS:(j + 1) * SCAN_COLS]
                          for j in range(SCAN_BLOCKS)]).astype(BF16)

    def out_slabs(c):
        c_t = c.transpose(0, 2, 1).reshape(N_STATES, SSM_GROUP)
        dense = jnp.where(out_mask, jnp.tile(c_t, (1, SSM_GROUPS)), 0.0)
        return jnp.stack([dense[j * SCAN_COLS:(j + 1) * SCAN_COLS, (j // 2) * LANES:(j // 2 + 1) * LANES]
                          for j in range(SCAN_BLOCKS)]).astype(BF16)

    return in_slabs(bb_re_t), in_slabs(bb_im_t), out_slabs(c_re), out_slabs(-c_im)


def kernel(x, meta_tokens, norm_ffn1, w_ffn1_in, w_ffn1_out, norm_mix, w_in, ssm_a_re_fwd, ssm_a_im_fwd, ssm_log_dt_fwd, ssm_a_re_bwd, ssm_a_im_bwd, ssm_log_dt_bwd, ssm_b_re, ssm_b_im, ssm_c_re, ssm_c_im, ssm_d, w_glu, na_rpb, w_branch_ssm, w_branch_na, w_out, norm_ffn2, w_ffn2_in, w_ffn2_out, norm_final):
    batch, seq, d_model = x.shape
    assert d_model == D_MODEL and norm_ffn1.shape[0] == 1, "single-layer block of width 1024"
    assert seq % ROW_TILE == 0 and seq % (GRID_W * NA_Q_ROWS) == 0 and batch % 8 == 0
    rows = seq // GRID_W
    assert rows >= NA_K_ROWS + NA_Q_ROWS

    row = lambda a: a.reshape(1, -1).astype(F32)
    g1, gm, g2, gf = row(norm_ffn1[0]), row(norm_mix[0]), row(norm_ffn2[0]), row(norm_final)
    w1_in, w1_out = w_ffn1_in[0].astype(BF16), w_ffn1_out[0].astype(BF16)
    w2_in, w2_out = w_ffn2_in[0].astype(BF16), w_ffn2_out[0].astype(BF16)
    w_uqkv = w_in[0][:, :UQKV_WIDTH].astype(BF16)
    w_gate = w_in[0][:, UQKV_WIDTH:].astype(BF16)

    n_rows = batch * seq
    tok_spec = pl.BlockSpec((ROW_TILE, D_MODEL), lambda i: (i, 0))
    h1 = _ffn_call(x.reshape(n_rows, D_MODEL), g1, w1_in, w1_out, gf, grid=(n_rows // ROW_TILE,),
                   x_spec=tok_spec, o_spec=tok_spec, out_shape=(n_rows, D_MODEL), final_norm=False)
    meta_spec = pl.BlockSpec((N_META, D_MODEL), lambda i: (0, 0))
    h1_meta = _ffn_call(meta_tokens.astype(F32), g1, w1_in, w1_out, gf, grid=(1,),
                        x_spec=meta_spec, o_spec=meta_spec, out_shape=(N_META, D_MODEL),
                        final_norm=False)

    u, q, k, v = _proj_call(h1, gm, w_uqkv, ROW_TILE)
    u_meta, _, k_meta, v_meta = _proj_call(h1_meta, gm, w_uqkv, N_META)
    u = u.reshape(batch, seq, SSM_WIDTH).transpose(1, 0, 2).reshape(n_rows, SSM_WIDTH)

    states = lambda a: a[0].reshape(1, N_STATES).astype(F32)
    per_state = lambda a: jnp.repeat(a[0].astype(F32), SSM_STATE).reshape(1, N_STATES)
    b_re_t = ssm_b_re[0].reshape(N_STATES, SSM_GROUP).T.astype(F32)
    b_im_t = ssm_b_im[0].reshape(N_STATES, SSM_GROUP).T.astype(F32)
    lbf_re, lbf_im, bbf_re, bbf_im = _disc_call(
        states(ssm_a_re_fwd), states(ssm_a_im_fwd), per_state(ssm_log_dt_fwd), b_re_t, b_im_t)
    lbb_re, lbb_im, bbb_re, bbb_im = _disc_call(
        states(ssm_a_re_bwd), states(ssm_a_im_bwd), per_state(ssm_log_dt_bwd), b_re_t, b_im_t)
    wf_re, wf_im, c_re, c_im = _s5_weights(bbf_re, bbf_im, ssm_c_re[0], ssm_c_im[0])
    wb_re, wb_im, _, _ = _s5_weights(bbb_re, bbb_im, ssm_c_re[0], ssm_c_im[0])

    zero_state = jnp.zeros((batch, 2 * N_STATES), F32)
    u_meta_rows = jnp.repeat(u_meta, batch, axis=0)
    _, state_meta = _s5_call(u_meta_rows, zero_state, lbf_re, lbf_im, wf_re, wf_im, c_re, c_im,
                             None, tile=N_META * batch, reverse=False)
    y_fwd, _ = _s5_call(u, state_meta, lbf_re, lbf_im, wf_re, wf_im, c_re, c_im,
                        None, tile=ROW_TILE, reverse=False)
    z, _ = _s5_call(u, zero_state, lbb_re, lbb_im, wb_re, wb_im, c_re, c_im,
                    (y_fwd, row(ssm_d[0]), w_glu[0].astype(BF16)), tile=ROW_TILE, reverse=True)

    bias = _na_bias(na_rpb[0], rows)
    bias = bias.reshape(NA_HEADS // 2, 2, 3, NA_Q, NA_KEYS).transpose(0, 2, 1, 3, 4)
    pad_meta = lambda a: jnp.pad(a, ((0, NA_META_PAD - N_META), (0, 0)))
    att = _na_call(q, k, v, pad_meta(k_meta), pad_meta(v_meta), bias, batch=batch, seq=seq)

    z = z.reshape(seq, batch, SSM_WIDTH).transpose(1, 0, 2).reshape(n_rows, SSM_WIDTH)
    h2 = _merge_call(h1, z, att, gm, w_gate,
                     w_branch_ssm[0].astype(BF16), w_branch_na[0].astype(BF16),
                     w_out[0].astype(BF16))

    out = _ffn_call(h2, g2, w2_in, w2_out, gf, grid=(n_rows // ROW_TILE,), x_spec=tok_spec,
                    o_spec=tok_spec, out_shape=(n_rows, D_MODEL), final_norm=True)
    return out.reshape(batch, seq, D_MODEL)
```

```python
import functools
import math

import jax
import jax.numpy as jnp
from jax import lax
from jax.experimental import pallas as pl
from jax.experimental.pallas import tpu as pltpu

D_MODEL = 1024
N_META = 16
GRID_W = 64
D_FF = 2816
SSM_WIDTH = 512
SSM_GROUP = 16
SSM_GROUPS = 32
SSM_STATE = 64
N_STATES = SSM_GROUPS * SSM_STATE
NA_HEAD_DIM = 64
NA_WIDTH = 512
NA_HEADS = 8
WIN_H = 8
WIN_W = 16
UQKV_WIDTH = SSM_WIDTH + 3 * NA_WIDTH
RMS_EPS = 1e-6
A_RE_MAX = -1e-4
MASK_VALUE = -1e30

LANES = 128
MXU_COLS = 256
VMEM_LIMIT = 56 * 1024 * 1024

ROW_TILE = 512
FF_CHUNK = 2816
SCAN_COLS = 256
NA_Q_ROWS = 4
NA_K_ROWS = NA_Q_ROWS + WIN_H - 1
NA_META_PAD = GRID_W

BF16 = jnp.bfloat16
F32 = jnp.float32


def _const_spec(shape):
    zeros = (0,) * len(shape)
    return pl.BlockSpec(shape, lambda *_: zeros, pipeline_mode=pl.Buffered(1))


def _params(n_grid_dims):
    return pltpu.CompilerParams(
        dimension_semantics=("arbitrary",) * n_grid_dims,
        vmem_limit_bytes=VMEM_LIMIT)


def _rms_norm(x, g):
    ms = jnp.mean(x * x, axis=-1, keepdims=True)
    return x * lax.rsqrt(ms + RMS_EPS) * g


def _dot(a, b):
    return jnp.dot(a, b, preferred_element_type=F32)


def _dot_nt(a, b):
    return lax.dot_general(a, b, (((1,), (1,)), ((), ())), preferred_element_type=F32)


def _ffn_body(x_ref, g_ref, w_in_ref, w_out_ref, gf_ref, o_ref, *, final_norm):
    x = x_ref[...]
    hn = _rms_norm(x, g_ref[...]).astype(BF16)
    acc = jnp.zeros(x.shape, F32)
    for c in range(D_FF // FF_CHUNK):
        lo = c * FF_CHUNK
        gate = _dot(hn, w_in_ref[:, lo:lo + FF_CHUNK])
        up = _dot(hn, w_in_ref[:, D_FF + lo:D_FF + lo + FF_CHUNK])
        act = (jax.nn.silu(gate) * up).astype(BF16)
        acc = acc + _dot(act, w_out_ref[lo:lo + FF_CHUNK, :])
    y = x + 0.5 * acc
    if final_norm:
        y = _rms_norm(y, gf_ref[...])
    o_ref[...] = y


def _ffn_call(x, g, w_in, w_out, gf, *, grid, x_spec, o_spec, out_shape, final_norm):
    return pl.pallas_call(
        functools.partial(_ffn_body, final_norm=final_norm),
        grid=grid,
        in_specs=[x_spec, _const_spec((1, D_MODEL)), _const_spec((D_MODEL, 2 * D_FF)),
                  _const_spec((D_FF, D_MODEL)), _const_spec((1, D_MODEL))],
        out_specs=o_spec,
        out_shape=jax.ShapeDtypeStruct(out_shape, F32),
        compiler_params=_params(len(grid)),
        name="ffn",
    )(x, g, w_in, w_out, gf)


def _proj_body(h_ref, g_ref, w_ref, u_ref, q_ref, k_ref, v_ref):
    hn = _rms_norm(h_ref[...], g_ref[...]).astype(BF16)
    p = _dot(hn, w_ref[...])
    u_ref[...] = p[:, :SSM_WIDTH]
    q_ref[...] = (p[:, SSM_WIDTH:SSM_WIDTH + NA_WIDTH] * (NA_HEAD_DIM ** -0.5)).astype(BF16)
    k_ref[...] = p[:, SSM_WIDTH + NA_WIDTH:SSM_WIDTH + 2 * NA_WIDTH].astype(BF16)
    v_ref[...] = p[:, SSM_WIDTH + 2 * NA_WIDTH:].astype(BF16)


def _proj_call(h, g, w, tile):
    rows = h.shape[0]
    row_spec = lambda width: pl.BlockSpec((tile, width), lambda i: (i, 0))
    return pl.pallas_call(
        _proj_body,
        grid=(rows // tile,),
        in_specs=[row_spec(D_MODEL), _const_spec((1, D_MODEL)), _const_spec((D_MODEL, UQKV_WIDTH))],
        out_specs=[row_spec(SSM_WIDTH)] * 4,
        out_shape=[jax.ShapeDtypeStruct((rows, SSM_WIDTH), F32)]
        + [jax.ShapeDtypeStruct((rows, NA_WIDTH), BF16)] * 3,
        compiler_params=_params(1),
        name="proj",
    )(h, g, w)


def _same_group(shape, ch_axis, ch0, state0):
    ch = lax.broadcasted_iota(jnp.int32, shape, ch_axis) + ch0
    st = lax.broadcasted_iota(jnp.int32, shape, 1 - ch_axis) + state0
    return (ch >> int(math.log2(SSM_GROUP))) == (st >> int(math.log2(SSM_STATE)))


def _disc_body(a_re_ref, a_im_ref, log_dt_ref, b_re_ref, b_im_ref,
               lb_re_ref, lb_im_ref, w_re_ref, w_im_ref):
    dt = jnp.exp(log_dt_ref[...])
    a_re = jnp.minimum(a_re_ref[...], A_RE_MAX)
    a_im = a_im_ref[...]
    mag = jnp.exp(a_re * dt)
    lb_re = mag * jnp.cos(a_im * dt)
    lb_im = mag * jnp.sin(a_im * dt)
    den = a_re * a_re + a_im * a_im
    nr = lb_re - 1.0
    ni = lb_im
    f_re = (nr * a_re + ni * a_im) / den
    f_im = (ni * a_re - nr * a_im) / den
    lb_re_ref[...] = lb_re
    lb_im_ref[...] = lb_im
    bb_re = f_re * b_re_ref[...] - f_im * b_im_ref[...]
    bb_im = f_re * b_im_ref[...] + f_im * b_re_ref[...]
    for j in range(SCAN_BLOCKS):
        same = _same_group((LANES, SCAN_COLS), 0, (j // 2) * LANES, j * SCAN_COLS)
        for bb, w_ref in ((bb_re, w_re_ref), (bb_im, w_im_ref)):
            rows = jnp.concatenate([bb[:, j * SCAN_COLS:(j + 1) * SCAN_COLS]] * (LANES // SSM_GROUP), axis=0)
            w_ref[j] = jnp.where(same, rows, 0.0).astype(BF16)


def _disc_call(a_re, a_im, log_dt, b_re_t, b_im_t):
    row = jax.ShapeDtypeStruct((1, N_STATES), F32)
    slabs = jax.ShapeDtypeStruct((SCAN_BLOCKS, LANES, SCAN_COLS), BF16)
    return pl.pallas_call(_disc_body, out_shape=[row, row, slabs, slabs], name="disc")(
        a_re, a_im, log_dt, b_re_t, b_im_t)


def _cslab_body(c_re_ref, c_im_ref, ct_re_ref, ct_im_ref):
    for j in range(SCAN_BLOCKS):
        same = _same_group((LANES, SCAN_COLS), 0, (j // 2) * LANES, j * SCAN_COLS)
        for c_ref, ct_ref, sign in ((c_re_ref, ct_re_ref, 1.0), (c_im_ref, ct_im_ref, -1.0)):
            c = c_ref[(j // 2) * LANES:(j // 2 + 1) * LANES, :] * sign
            cols = jnp.concatenate([c] * (SCAN_COLS // SSM_STATE), axis=1)
            ct_ref[j] = jnp.where(same, cols, 0.0).astype(BF16)


def _cslab_call(c_re, c_im):
    slabs = jax.ShapeDtypeStruct((SCAN_BLOCKS, LANES, SCAN_COLS), BF16)
    return pl.pallas_call(_cslab_body, out_shape=[slabs, slabs], name="cslab")(c_re, c_im)


SCAN_BLOCKS = N_STATES // SCAN_COLS
Y_COLS = LANES


def _s5_body(*refs, reverse, steps, finish):
    if finish:
        (u_ref, init_ref, lb_re_ref, lb_im_ref, w_re_ref, w_im_ref, c_re_ref, c_im_ref,
         yf_ref, d_ref, w_glu_ref, o_ref, fin_ref, bu_re, bu_im, h_re, h_im) = refs
    else:
        (u_ref, init_ref, lb_re_ref, lb_im_ref, w_re_ref, w_im_ref, c_re_ref, c_im_ref,
         o_ref, fin_ref, bu_re, bu_im, h_re, h_im) = refs
    nb = init_ref.shape[0]

    @pl.when(pl.program_id(0) == 0)
    def _():
        h_re[...] = init_ref[:, :N_STATES]
        h_im[...] = init_ref[:, N_STATES:]

    def scan_order(ref):
        if not reverse:
            return ref[...]
        return jnp.concatenate(
            [ref[(steps - 1 - s) * nb:(steps - s) * nb, :] for s in range(steps)], axis=0)

    u = scan_order(u_ref)
    ub = u.astype(BF16)
    y_parts = []
    for j in range(SCAN_BLOCKS):
        cols = slice(j * SCAN_COLS, (j + 1) * SCAN_COLS)
        us = ub[:, (j // 2) * LANES:(j // 2 + 1) * LANES]
        bu_re[j] = _dot(us, w_re_ref[j])
        bu_im[j] = _dot(us, w_im_ref[j])
        lr = jnp.broadcast_to(lb_re_ref[:, cols], (nb, SCAN_COLS))
        li = jnp.broadcast_to(lb_im_ref[:, cols], (nb, SCAN_COLS))
        hr, hi = h_re[:, cols], h_im[:, cols]
        for s in range(steps):
            rows = slice(s * nb, (s + 1) * nb)
            hr, hi = (lr * hr - li * hi + bu_re[j, rows, :],
                      lr * hi + li * hr + bu_im[j, rows, :])
            bu_re[j, rows, :] = hr
            bu_im[j, rows, :] = hi
        h_re[:, cols] = hr
        h_im[:, cols] = hi
        part = (_dot_nt(bu_re[j].astype(BF16), c_re_ref[j])
                + _dot_nt(bu_im[j].astype(BF16), c_im_ref[j]))
        if j % 2 == 0:
            y_parts.append(part)
        else:
            y_parts[-1] = y_parts[-1] + part
    y = jnp.concatenate(y_parts, axis=-1)
    fin_ref[:, :N_STATES] = h_re[...]
    fin_ref[:, N_STATES:] = h_im[...]
    if finish:
        y = y + scan_order(yf_ref) + d_ref[...] * u
        z = jax.nn.gelu(y)
        y = (z * jax.nn.sigmoid(_dot(z.astype(BF16), w_glu_ref[...]))).astype(BF16)
    if reverse:
        for s in range(steps):
            o_ref[(steps - 1 - s) * nb:(steps - s) * nb, :] = y[s * nb:(s + 1) * nb, :]
    else:
        o_ref[...] = y


def _s5_call(u, init, lb_re, lb_im, w_re, w_im, c_re, c_im, finish_args, *, tile, reverse):
    rows = u.shape[0]
    nb = init.shape[0]
    n_tiles = rows // tile
    finish = finish_args is not None
    row_map = (lambda i: (n_tiles - 1 - i, 0)) if reverse else (lambda i: (i, 0))
    row_spec = pl.BlockSpec((tile, SSM_WIDTH), row_map)
    in_specs = [row_spec, _const_spec((nb, 2 * N_STATES)),
                _const_spec((1, N_STATES)), _const_spec((1, N_STATES)),
                _const_spec((SCAN_BLOCKS, LANES, SCAN_COLS)), _const_spec((SCAN_BLOCKS, LANES, SCAN_COLS)),
                _const_spec((SCAN_BLOCKS, LANES, SCAN_COLS)), _const_spec((SCAN_BLOCKS, LANES, SCAN_COLS))]
    args = [u, init, lb_re, lb_im, w_re, w_im, c_re, c_im]
    if finish:
        in_specs += [row_spec, _const_spec((1, SSM_WIDTH)), _const_spec((SSM_WIDTH, SSM_WIDTH))]
        args += list(finish_args)
    return pl.pallas_call(
        functools.partial(_s5_body, reverse=reverse, steps=tile // nb, finish=finish),
        grid=(n_tiles,),
        in_specs=in_specs,
        out_specs=[row_spec, pl.BlockSpec((nb, 2 * N_STATES), lambda i: (0, 0))],
        out_shape=[jax.ShapeDtypeStruct((rows, SSM_WIDTH), BF16 if finish else F32),
                   jax.ShapeDtypeStruct((nb, 2 * N_STATES), F32)],
        scratch_shapes=[pltpu.VMEM((SCAN_BLOCKS, tile, SCAN_COLS), F32),
                        pltpu.VMEM((SCAN_BLOCKS, tile, SCAN_COLS), F32),
                        pltpu.VMEM((nb, N_STATES), F32), pltpu.VMEM((nb, N_STATES), F32)],
        compiler_params=_params(1),
        name="s5_bwd" if reverse else "s5_fwd",
    )(*args)


NA_Q = NA_Q_ROWS * GRID_W
NA_K = NA_K_ROWS * GRID_W
NA_KEYS = NA_K + NA_META_PAD


def _block_key_row(blk, rows):
    return min(max(blk * NA_Q_ROWS - WIN_H // 2, 0), rows - NA_K_ROWS)


def _na_assemble_bias(t_ref, bias, n_blocks, rows):
    kh = min(WIN_H, rows)
    masked = jnp.full((GRID_W, GRID_W), MASK_VALUE, F32)
    meta = jnp.where(lax.broadcasted_iota(jnp.int32, (GRID_W, GRID_W), 1) < N_META, 0.0, MASK_VALUE)
    for kind, blk in enumerate((0, 1, n_blocks - 1)):
        for head in range(2):
            for qr in range(NA_Q_ROWS):
                r = blk * NA_Q_ROWS + qr
                row_start = min(max(r - kh // 2, 0), rows - kh)
                tiles = []
                for slot in range(NA_K_ROWS):
                    kr = _block_key_row(blk, rows) + slot
                    valid = row_start <= kr < row_start + kh
                    tiles.append(t_ref[head, kr - r + WIN_H - 1] if valid else masked)
                tiles.append(meta)
                for c in range(0, len(tiles), 2):
                    bias[kind, head, qr * GRID_W:(qr + 1) * GRID_W, c * GRID_W:(c + 2) * GRID_W] = (
                        jnp.concatenate(tiles[c:c + 2], axis=1))


def _na_body(q_ref, k_ref, v_ref, km_ref, vm_ref, t_ref, o_ref, bias, *, n_blocks, rows):
    @pl.when(pl.program_id(1) == 0)
    def _():
        _na_assemble_bias(t_ref, bias, n_blocks, rows)

    first_head = lax.broadcasted_iota(jnp.int32, (1, LANES), 1) < NA_HEAD_DIM
    km = km_ref[...]
    vm = vm_ref[...]

    def block(i, carry):
        key_row = jnp.clip(i * NA_Q_ROWS - WIN_H // 2, 0, rows - NA_K_ROWS)
        k0 = pl.multiple_of(key_row * GRID_W, GRID_W)
        q0 = pl.multiple_of(i * NA_Q, NA_Q)
        kind = jnp.where(i == 0, 0, jnp.where(i == n_blocks - 1, 2, 1))
        q = q_ref[pl.ds(q0, NA_Q), :]
        zero = jnp.zeros_like(q)
        qs = jnp.concatenate([jnp.where(first_head, q, zero), jnp.where(first_head, zero, q)], axis=0)
        keys = jnp.concatenate([k_ref[pl.ds(k0, NA_K), :], km], axis=0)
        vals = jnp.concatenate([v_ref[pl.ds(k0, NA_K), :], vm], axis=0)
        s = _dot_nt(qs, keys) + bias[kind].reshape(2 * NA_Q, NA_KEYS)
        m = jnp.max(s, axis=-1, keepdims=True)
        p = jnp.exp(s - m)
        l = jnp.sum(p, axis=-1, keepdims=True)
        o = _dot(p.astype(BF16), vals) / l
        o_ref[pl.ds(q0, NA_Q), :] = jnp.where(first_head, o[:NA_Q], o[NA_Q:]).astype(BF16)
        return carry

    lax.fori_loop(0, n_blocks, block, 0, unroll=2)


def _na_call(q, k, v, km, vm, col_bias, *, batch, seq):
    rows = seq // GRID_W
    n_pairs = NA_HEADS // 2
    n_blocks = rows // NA_Q_ROWS
    assert NA_META_PAD == GRID_W and (NA_K_ROWS + 1) % 2 == 0
    seq_spec = pl.BlockSpec((seq, LANES), lambda hp, b: (b, hp))
    meta_spec = pl.BlockSpec((NA_META_PAD, LANES), lambda hp, b: (0, hp))
    table_spec = pl.BlockSpec((2, 2 * WIN_H - 1, GRID_W, GRID_W), lambda hp, b: (hp, 0, 0, 0))
    return pl.pallas_call(
        functools.partial(_na_body, n_blocks=n_blocks, rows=rows),
        grid=(n_pairs, batch),
        in_specs=[seq_spec, seq_spec, seq_spec, meta_spec, meta_spec, table_spec],
        out_specs=seq_spec,
        out_shape=jax.ShapeDtypeStruct((batch * seq, NA_WIDTH), BF16),
        scratch_shapes=[pltpu.VMEM((3, 2, NA_Q, NA_KEYS), F32)],
        compiler_params=_params(2),
        name="na",
    )(q, k, v, km, vm, col_bias)


def _na_col_bias(rpb):
    cols = jnp.arange(GRID_W)
    col_start = jnp.clip(cols - WIN_W // 2, 0, GRID_W - WIN_W)
    col_valid = (cols[None, :] >= col_start[:, None]) & (cols[None, :] < col_start[:, None] + WIN_W)
    col_idx = jnp.clip(cols[None, :] - cols[:, None] + WIN_W - 1, 0, 2 * WIN_W - 2)
    rpb_c = jnp.take(rpb.astype(F32), col_idx, axis=2)
    return jnp.where(col_valid[None, None], rpb_c, MASK_VALUE)


def _merge_body(h_ref, z_ref, a_ref, g_ref, w_gate_ref, w_ssm_ref, w_na_ref, w_out_ref, o_ref):
    h = h_ref[...]
    hn = _rms_norm(h, g_ref[...]).astype(BF16)
    gates = _dot(hn, w_gate_ref[...])
    y_ssm = _dot(z_ref[...], w_ssm_ref[...])
    y_na = _dot(a_ref[...], w_na_ref[...])
    merged = (jax.nn.sigmoid(gates[:, :D_MODEL]) * y_ssm
              + jax.nn.sigmoid(gates[:, D_MODEL:]) * y_na)
    o_ref[...] = h + _dot(merged.astype(BF16), w_out_ref[...])


def _merge_call(h1, z, a, g, w_gate, w_ssm, w_na, w_out):
    rows = h1.shape[0]
    row_spec = lambda width: pl.BlockSpec((ROW_TILE, width), lambda i: (i, 0))
    return pl.pallas_call(
        _merge_body,
        grid=(rows // ROW_TILE,),
        in_specs=[row_spec(D_MODEL), row_spec(SSM_WIDTH), row_spec(NA_WIDTH),
                  _const_spec((1, D_MODEL)), _const_spec((D_MODEL, 2 * D_MODEL)),
                  _const_spec((SSM_WIDTH, D_MODEL)), _const_spec((NA_WIDTH, D_MODEL)),
                  _const_spec((D_MODEL, D_MODEL))],
        out_specs=row_spec(D_MODEL),
        out_shape=jax.ShapeDtypeStruct((rows, D_MODEL), F32),
        compiler_params=_params(1),
        name="merge",
    )(h1, z, a, g, w_gate, w_ssm, w_na, w_out)


def kernel(x, meta_tokens, norm_ffn1, w_ffn1_in, w_ffn1_out, norm_mix, w_in, ssm_a_re_fwd, ssm_a_im_fwd, ssm_log_dt_fwd, ssm_a_re_bwd, ssm_a_im_bwd, ssm_log_dt_bwd, ssm_b_re, ssm_b_im, ssm_c_re, ssm_c_im, ssm_d, w_glu, na_rpb, w_branch_ssm, w_branch_na, w_out, norm_ffn2, w_ffn2_in, w_ffn2_out, norm_final):
    batch, seq, d_model = x.shape
    assert d_model == D_MODEL and norm_ffn1.shape[0] == 1, "single-layer block of width 1024"
    assert seq % ROW_TILE == 0 and seq % (GRID_W * NA_Q_ROWS) == 0 and batch % 8 == 0
    rows = seq // GRID_W
    assert rows >= NA_K_ROWS + NA_Q_ROWS

    row = lambda a: a.reshape(1, -1).astype(F32)
    g1, gm, g2, gf = row(norm_ffn1[0]), row(norm_mix[0]), row(norm_ffn2[0]), row(norm_final)
    w1_in, w1_out = w_ffn1_in[0].astype(BF16), w_ffn1_out[0].astype(BF16)
    w2_in, w2_out = w_ffn2_in[0].astype(BF16), w_ffn2_out[0].astype(BF16)
    w_uqkv = w_in[0][:, :UQKV_WIDTH].astype(BF16)
    w_gate = w_in[0][:, UQKV_WIDTH:].astype(BF16)

    n_rows = batch * seq
    tok_spec = pl.BlockSpec((ROW_TILE, D_MODEL), lambda i: (i, 0))
    h1 = _ffn_call(x.reshape(n_rows, D_MODEL), g1, w1_in, w1_out, gf, grid=(n_rows // ROW_TILE,),
                   x_spec=tok_spec, o_spec=tok_spec, out_shape=(n_rows, D_MODEL), final_norm=False)
    meta_spec = pl.BlockSpec((N_META, D_MODEL), lambda i: (0, 0))
    h1_meta = _ffn_call(meta_tokens.astype(F32), g1, w1_in, w1_out, gf, grid=(1,),
                        x_spec=meta_spec, o_spec=meta_spec, out_shape=(N_META, D_MODEL),
                        final_norm=False)

    u, q, k, v = _proj_call(h1, gm, w_uqkv, ROW_TILE)
    u_meta, _, k_meta, v_meta = _proj_call(h1_meta, gm, w_uqkv, N_META)
    u = u.reshape(batch, seq, SSM_WIDTH).transpose(1, 0, 2).reshape(n_rows, SSM_WIDTH)

    states = lambda a: a[0].reshape(1, N_STATES).astype(F32)
    per_state = lambda a: jnp.repeat(a[0].astype(F32), SSM_STATE).reshape(1, N_STATES)
    b_re_t = ssm_b_re[0].reshape(N_STATES, SSM_GROUP).T.astype(F32)
    b_im_t = ssm_b_im[0].reshape(N_STATES, SSM_GROUP).T.astype(F32)
    lbf_re, lbf_im, wf_re, wf_im = _disc_call(
        states(ssm_a_re_fwd), states(ssm_a_im_fwd), per_state(ssm_log_dt_fwd), b_re_t, b_im_t)
    lbb_re, lbb_im, wb_re, wb_im = _disc_call(
        states(ssm_a_re_bwd), states(ssm_a_im_bwd), per_state(ssm_log_dt_bwd), b_re_t, b_im_t)
    c_re, c_im = _cslab_call(ssm_c_re[0].reshape(SSM_WIDTH, SSM_STATE).astype(F32),
                             ssm_c_im[0].reshape(SSM_WIDTH, SSM_STATE).astype(F32))

    zero_state = jnp.zeros((batch, 2 * N_STATES), F32)
    u_meta_rows = jnp.repeat(u_meta, batch, axis=0)
    _, state_meta = _s5_call(u_meta_rows, zero_state, lbf_re, lbf_im, wf_re, wf_im, c_re, c_im,
                             None, tile=N_META * batch, reverse=False)
    y_fwd, _ = _s5_call(u, state_meta, lbf_re, lbf_im, wf_re, wf_im, c_re, c_im,
                        None, tile=ROW_TILE, reverse=False)
    z, _ = _s5_call(u, zero_state, lbb_re, lbb_im, wb_re, wb_im, c_re, c_im,
                    (y_fwd, row(ssm_d[0]), w_glu[0].astype(BF16)), tile=ROW_TILE, reverse=True)

    pad_meta = lambda a: jnp.pad(a, ((0, NA_META_PAD - N_META), (0, 0)))
    att = _na_call(q, k, v, pad_meta(k_meta), pad_meta(v_meta), _na_col_bias(na_rpb[0]),
                   batch=batch, seq=seq)

    z = z.reshape(seq, batch, SSM_WIDTH).transpose(1, 0, 2).reshape(n_rows, SSM_WIDTH)
    h2 = _merge_call(h1, z, att, gm, w_gate,
                     w_branch_ssm[0].astype(BF16), w_branch_na[0].astype(BF16),
                     w_out[0].astype(BF16))

    out = _ffn_call(h2, g2, w2_in, w2_out, gf, grid=(n_rows // ROW_TILE,), x_spec=tok_spec,
                    o_spec=tok_spec, out_shape=(n_rows, D_MODEL), final_norm=True)
    return out.reshape(batch, seq, D_MODEL)
```

```python
import functools
import math

import jax
import jax.numpy as jnp
from jax import lax
from jax.experimental import pallas as pl
from jax.experimental.pallas import tpu as pltpu

D_MODEL = 1024
N_META = 16
GRID_W = 64
D_FF = 2816
SSM_WIDTH = 512
SSM_GROUP = 16
SSM_GROUPS = 32
SSM_STATE = 64
N_STATES = SSM_GROUPS * SSM_STATE
NA_HEAD_DIM = 64
NA_WIDTH = 512
NA_HEADS = 8
WIN_H = 8
WIN_W = 16
UQKV_WIDTH = SSM_WIDTH + 3 * NA_WIDTH
RMS_EPS = 1e-6
A_RE_MAX = -1e-4
MASK_VALUE = -1e30

LANES = 128
MXU_COLS = 256
VMEM_LIMIT = 56 * 1024 * 1024

ROW_TILE = 512
FF_CHUNK = 2816
SCAN_COLS = 256
SCAN_TILE = 1024
NA_Q_ROWS = 4
NA_K_ROWS = NA_Q_ROWS + WIN_H - 1
NA_META_PAD = GRID_W

BF16 = jnp.bfloat16
F32 = jnp.float32


def _const_spec(shape):
    zeros = (0,) * len(shape)
    return pl.BlockSpec(shape, lambda *_: zeros, pipeline_mode=pl.Buffered(1))


def _params(n_grid_dims):
    return pltpu.CompilerParams(
        dimension_semantics=("arbitrary",) * n_grid_dims,
        vmem_limit_bytes=VMEM_LIMIT)


def _rms_norm(x, g):
    ms = jnp.mean(x * x, axis=-1, keepdims=True)
    return x * lax.rsqrt(ms + RMS_EPS) * g


def _dot(a, b):
    return jnp.dot(a, b, preferred_element_type=F32)


def _dot_nt(a, b):
    return lax.dot_general(a, b, (((1,), (1,)), ((), ())), preferred_element_type=F32)


def _ffn_body(x_ref, g_ref, w_in_ref, w_out_ref, gf_ref, o_ref, *, final_norm):
    x = x_ref[...]
    hn = _rms_norm(x, g_ref[...]).astype(BF16)
    acc = jnp.zeros(x.shape, F32)
    for c in range(D_FF // FF_CHUNK):
        lo = c * FF_CHUNK
        gate = _dot(hn, w_in_ref[:, lo:lo + FF_CHUNK])
        up = _dot(hn, w_in_ref[:, D_FF + lo:D_FF + lo + FF_CHUNK])
        act = (jax.nn.silu(gate) * up).astype(BF16)
        acc = acc + _dot(act, w_out_ref[lo:lo + FF_CHUNK, :])
    y = x + 0.5 * acc
    if final_norm:
        y = _rms_norm(y, gf_ref[...])
    o_ref[...] = y


def _ffn_call(x, g, w_in, w_out, gf, *, grid, x_spec, o_spec, out_shape, final_norm):
    return pl.pallas_call(
        functools.partial(_ffn_body, final_norm=final_norm),
        grid=grid,
        in_specs=[x_spec, _const_spec((1, D_MODEL)), _const_spec((D_MODEL, 2 * D_FF)),
                  _const_spec((D_FF, D_MODEL)), _const_spec((1, D_MODEL))],
        out_specs=o_spec,
        out_shape=jax.ShapeDtypeStruct(out_shape, F32),
        compiler_params=_params(len(grid)),
        name="ffn",
    )(x, g, w_in, w_out, gf)


def _proj_body(h_ref, g_ref, w_ref, u_ref, q_ref, k_ref, v_ref):
    hn = _rms_norm(h_ref[...], g_ref[...]).astype(BF16)
    p = _dot(hn, w_ref[...])
    u_ref[...] = p[:, :SSM_WIDTH]
    q_ref[...] = (p[:, SSM_WIDTH:SSM_WIDTH + NA_WIDTH] * (NA_HEAD_DIM ** -0.5)).astype(BF16)
    k_ref[...] = p[:, SSM_WIDTH + NA_WIDTH:SSM_WIDTH + 2 * NA_WIDTH].astype(BF16)
    v_ref[...] = p[:, SSM_WIDTH + 2 * NA_WIDTH:].astype(BF16)


def _proj_call(h, g, w, tile):
    rows = h.shape[0]
    row_spec = lambda width: pl.BlockSpec((tile, width), lambda i: (i, 0))
    return pl.pallas_call(
        _proj_body,
        grid=(rows // tile,),
        in_specs=[row_spec(D_MODEL), _const_spec((1, D_MODEL)), _const_spec((D_MODEL, UQKV_WIDTH))],
        out_specs=[row_spec(SSM_WIDTH)] * 4,
        out_shape=[jax.ShapeDtypeStruct((rows, SSM_WIDTH), F32)]
        + [jax.ShapeDtypeStruct((rows, NA_WIDTH), BF16)] * 3,
        compiler_params=_params(1),
        name="proj",
    )(h, g, w)


def _same_group(shape, ch_axis, ch0, state0):
    ch = lax.broadcasted_iota(jnp.int32, shape, ch_axis) + ch0
    st = lax.broadcasted_iota(jnp.int32, shape, 1 - ch_axis) + state0
    return (ch >> int(math.log2(SSM_GROUP))) == (st >> int(math.log2(SSM_STATE)))


def _disc_body(a_re_ref, a_im_ref, log_dt_ref, b_re_ref, b_im_ref,
               lb_re_ref, lb_im_ref, w_re_ref, w_im_ref):
    dt = jnp.exp(log_dt_ref[...])
    a_re = jnp.minimum(a_re_ref[...], A_RE_MAX)
    a_im = a_im_ref[...]
    mag = jnp.exp(a_re * dt)
    lb_re = mag * jnp.cos(a_im * dt)
    lb_im = mag * jnp.sin(a_im * dt)
    den = a_re * a_re + a_im * a_im
    nr = lb_re - 1.0
    ni = lb_im
    f_re = (nr * a_re + ni * a_im) / den
    f_im = (ni * a_re - nr * a_im) / den
    lb_re_ref[...] = lb_re
    lb_im_ref[...] = lb_im
    bb_re = f_re * b_re_ref[...] - f_im * b_im_ref[...]
    bb_im = f_re * b_im_ref[...] + f_im * b_re_ref[...]
    for j in range(SCAN_BLOCKS):
        same = _same_group((LANES, SCAN_COLS), 0, (j // 2) * LANES, j * SCAN_COLS)
        for bb, w_ref in ((bb_re, w_re_ref), (bb_im, w_im_ref)):
            rows = jnp.concatenate([bb[:, j * SCAN_COLS:(j + 1) * SCAN_COLS]] * (LANES // SSM_GROUP), axis=0)
            w_ref[j] = jnp.where(same, rows, 0.0).astype(BF16)


def _disc_call(a_re, a_im, log_dt, b_re_t, b_im_t):
    row = jax.ShapeDtypeStruct((1, N_STATES), F32)
    slabs = jax.ShapeDtypeStruct((SCAN_BLOCKS, LANES, SCAN_COLS), BF16)
    return pl.pallas_call(_disc_body, out_shape=[row, row, slabs, slabs], name="disc")(
        a_re, a_im, log_dt, b_re_t, b_im_t)


def _cslab_body(c_re_ref, c_im_ref, ct_re_ref, ct_im_ref):
    for j in range(SCAN_BLOCKS):
        same = _same_group((LANES, SCAN_COLS), 0, (j // 2) * LANES, j * SCAN_COLS)
        for c_ref, ct_ref, sign in ((c_re_ref, ct_re_ref, 1.0), (c_im_ref, ct_im_ref, -1.0)):
            c = c_ref[(j // 2) * LANES:(j // 2 + 1) * LANES, :] * sign
            cols = jnp.concatenate([c] * (SCAN_COLS // SSM_STATE), axis=1)
            ct_ref[j] = jnp.where(same, cols, 0.0).astype(BF16)


def _cslab_call(c_re, c_im):
    slabs = jax.ShapeDtypeStruct((SCAN_BLOCKS, LANES, SCAN_COLS), BF16)
    return pl.pallas_call(_cslab_body, out_shape=[slabs, slabs], name="cslab")(c_re, c_im)


SCAN_BLOCKS = N_STATES // SCAN_COLS
Y_COLS = LANES


def _s5_body(*refs, reverse, steps, finish):
    if finish:
        (u_ref, init_ref, lb_re_ref, lb_im_ref, w_re_ref, w_im_ref, c_re_ref, c_im_ref,
         yf_ref, d_ref, w_glu_ref, o_ref, fin_ref, bu_re, bu_im, h_re, h_im) = refs
    else:
        (u_ref, init_ref, lb_re_ref, lb_im_ref, w_re_ref, w_im_ref, c_re_ref, c_im_ref,
         o_ref, fin_ref, bu_re, bu_im, h_re, h_im) = refs
    nb = init_ref.shape[0]

    @pl.when(pl.program_id(0) == 0)
    def _():
        h_re[...] = init_ref[:, :N_STATES]
        h_im[...] = init_ref[:, N_STATES:]

    def scan_order(ref):
        if not reverse:
            return ref[...]
        return jnp.concatenate(
            [ref[(steps - 1 - s) * nb:(steps - s) * nb, :] for s in range(steps)], axis=0)

    u = scan_order(u_ref)
    ub = u.astype(BF16)
    y_parts = []
    for j in range(SCAN_BLOCKS):
        cols = slice(j * SCAN_COLS, (j + 1) * SCAN_COLS)
        us = ub[:, (j // 2) * LANES:(j // 2 + 1) * LANES]
        bu_re[j] = _dot(us, w_re_ref[j])
        bu_im[j] = _dot(us, w_im_ref[j])
        lr = jnp.broadcast_to(lb_re_ref[:, cols], (nb, SCAN_COLS))
        li = jnp.broadcast_to(lb_im_ref[:, cols], (nb, SCAN_COLS))
        hr, hi = h_re[:, cols], h_im[:, cols]
        for s in range(steps):
            rows = slice(s * nb, (s + 1) * nb)
            hr, hi = (lr * hr - li * hi + bu_re[j, rows, :],
                      lr * hi + li * hr + bu_im[j, rows, :])
            bu_re[j, rows, :] = hr
            bu_im[j, rows, :] = hi
        h_re[:, cols] = hr
        h_im[:, cols] = hi
        part = (_dot_nt(bu_re[j].astype(BF16), c_re_ref[j])
                + _dot_nt(bu_im[j].astype(BF16), c_im_ref[j]))
        if j % 2 == 0:
            y_parts.append(part)
        else:
            y_parts[-1] = y_parts[-1] + part
    y = jnp.concatenate(y_parts, axis=-1)
    fin_ref[:, :N_STATES] = h_re[...]
    fin_ref[:, N_STATES:] = h_im[...]
    if finish:
        y = y + scan_order(yf_ref) + d_ref[...] * u
        z = jax.nn.gelu(y)
        y = (z * jax.nn.sigmoid(_dot(z.astype(BF16), w_glu_ref[...]))).astype(BF16)
    if reverse:
        for s in range(steps):
            o_ref[(steps - 1 - s) * nb:(steps - s) * nb, :] = y[s * nb:(s + 1) * nb, :]
    else:
        o_ref[...] = y


def _s5_call(u, init, lb_re, lb_im, w_re, w_im, c_re, c_im, finish_args, *, tile, reverse):
    rows = u.shape[0]
    nb = init.shape[0]
    n_tiles = rows // tile
    finish = finish_args is not None
    row_map = (lambda i: (n_tiles - 1 - i, 0)) if reverse else (lambda i: (i, 0))
    row_spec = pl.BlockSpec((tile, SSM_WIDTH), row_map)
    in_specs = [row_spec, _const_spec((nb, 2 * N_STATES)),
                _const_spec((1, N_STATES)), _const_spec((1, N_STATES)),
                _const_spec((SCAN_BLOCKS, LANES, SCAN_COLS)), _const_spec((SCAN_BLOCKS, LANES, SCAN_COLS)),
                _const_spec((SCAN_BLOCKS, LANES, SCAN_COLS)), _const_spec((SCAN_BLOCKS, LANES, SCAN_COLS))]
    args = [u, init, lb_re, lb_im, w_re, w_im, c_re, c_im]
    if finish:
        in_specs += [row_spec, _const_spec((1, SSM_WIDTH)), _const_spec((SSM_WIDTH, SSM_WIDTH))]
        args += list(finish_args)
    return pl.pallas_call(
        functools.partial(_s5_body, reverse=reverse, steps=tile // nb, finish=finish),
        grid=(n_tiles,),
        in_specs=in_specs,
        out_specs=[row_spec, pl.BlockSpec((nb, 2 * N_STATES), lambda i: (0, 0))],
        out_shape=[jax.ShapeDtypeStruct((rows, SSM_WIDTH), BF16 if finish else F32),
                   jax.ShapeDtypeStruct((nb, 2 * N_STATES), F32)],
        scratch_shapes=[pltpu.VMEM((SCAN_BLOCKS, tile, SCAN_COLS), F32),
                        pltpu.VMEM((SCAN_BLOCKS, tile, SCAN_COLS), F32),
                        pltpu.VMEM((nb, N_STATES), F32), pltpu.VMEM((nb, N_STATES), F32)],
        compiler_params=_params(1),
        name="s5_bwd" if reverse else "s5_fwd",
    )(*args)


NA_Q = NA_Q_ROWS * GRID_W
NA_K = NA_K_ROWS * GRID_W
NA_KEYS = NA_K + NA_META_PAD


def _block_key_row(blk, rows):
    return min(max(blk * NA_Q_ROWS - WIN_H // 2, 0), rows - NA_K_ROWS)


def _na_assemble_bias(t_ref, bias, n_blocks, rows):
    kh = min(WIN_H, rows)
    masked = jnp.full((GRID_W, GRID_W), MASK_VALUE, F32)
    meta = jnp.where(lax.broadcasted_iota(jnp.int32, (GRID_W, GRID_W), 1) < N_META, 0.0, MASK_VALUE)
    for kind, blk in enumerate((0, 1, n_blocks - 1)):
        for head in range(2):
            for qr in range(NA_Q_ROWS):
                r = blk * NA_Q_ROWS + qr
                row_start = min(max(r - kh // 2, 0), rows - kh)
                tiles = []
                for slot in range(NA_K_ROWS):
                    kr = _block_key_row(blk, rows) + slot
                    valid = row_start <= kr < row_start + kh
                    tiles.append(t_ref[head, kr - r + WIN_H - 1] if valid else masked)
                tiles.append(meta)
                for c in range(0, len(tiles), 2):
                    bias[kind, head, qr * GRID_W:(qr + 1) * GRID_W, c * GRID_W:(c + 2) * GRID_W] = (
                        jnp.concatenate(tiles[c:c + 2], axis=1))


def _na_body(q_ref, k_ref, v_ref, km_ref, vm_ref, t_ref, o_ref, bias, *, n_blocks, rows):
    @pl.when(pl.program_id(1) == 0)
    def _():
        _na_assemble_bias(t_ref, bias, n_blocks, rows)

    first_head = lax.broadcasted_iota(jnp.int32, (1, LANES), 1) < NA_HEAD_DIM
    km = km_ref[...]
    vm = vm_ref[...]

    def block(i, carry):
        key_row = jnp.clip(i * NA_Q_ROWS - WIN_H // 2, 0, rows - NA_K_ROWS)
        k0 = pl.multiple_of(key_row * GRID_W, GRID_W)
        q0 = pl.multiple_of(i * NA_Q, NA_Q)
        kind = jnp.where(i == 0, 0, jnp.where(i == n_blocks - 1, 2, 1))
        q = q_ref[pl.ds(q0, NA_Q), :]
        zero = jnp.zeros_like(q)
        qs = jnp.concatenate([jnp.where(first_head, q, zero), jnp.where(first_head, zero, q)], axis=0)
        keys = jnp.concatenate([k_ref[pl.ds(k0, NA_K), :], km], axis=0)
        vals = jnp.concatenate([v_ref[pl.ds(k0, NA_K), :], vm], axis=0)
        s = _dot_nt(qs, keys) + bias[kind].reshape(2 * NA_Q, NA_KEYS)
        m = jnp.max(s, axis=-1, keepdims=True)
        p = jnp.exp(s - m)
        l = jnp.sum(p, axis=-1, keepdims=True)
        o = _dot(p.astype(BF16), vals) / l
        o_ref[pl.ds(q0, NA_Q), :] = jnp.where(first_head, o[:NA_Q], o[NA_Q:]).astype(BF16)
        return carry

    lax.fori_loop(0, n_blocks, block, 0, unroll=8)


def _na_call(q, k, v, km, vm, col_bias, *, batch, seq):
    rows = seq // GRID_W
    n_pairs = NA_HEADS // 2
    n_blocks = rows // NA_Q_ROWS
    assert NA_META_PAD == GRID_W and (NA_K_ROWS + 1) % 2 == 0
    seq_spec = pl.BlockSpec((seq, LANES), lambda hp, b: (b, hp))
    meta_spec = pl.BlockSpec((NA_META_PAD, LANES), lambda hp, b: (0, hp))
    table_spec = pl.BlockSpec((2, 2 * WIN_H - 1, GRID_W, GRID_W), lambda hp, b: (hp, 0, 0, 0))
    return pl.pallas_call(
        functools.partial(_na_body, n_blocks=n_blocks, rows=rows),
        grid=(n_pairs, batch),
        in_specs=[seq_spec, seq_spec, seq_spec, meta_spec, meta_spec, table_spec],
        out_specs=seq_spec,
        out_shape=jax.ShapeDtypeStruct((batch * seq, NA_WIDTH), BF16),
        scratch_shapes=[pltpu.VMEM((3, 2, NA_Q, NA_KEYS), F32)],
        compiler_params=_params(2),
        name="na",
    )(q, k, v, km, vm, col_bias)


def _na_col_bias(rpb):
    cols = jnp.arange(GRID_W)
    col_start = jnp.clip(cols - WIN_W // 2, 0, GRID_W - WIN_W)
    col_valid = (cols[None, :] >= col_start[:, None]) & (cols[None, :] < col_start[:, None] + WIN_W)
    period = 2 * GRID_W
    n_head, n_off, _ = rpb.shape
    ring = jnp.zeros((n_head, n_off, period), F32)
    ring = ring.at[..., :WIN_W].set(rpb[..., WIN_W - 1:].astype(F32))
    ring = ring.at[..., period - (WIN_W - 1):].set(rpb[..., :WIN_W - 1].astype(F32))
    flat = jnp.tile(ring, (1, 1, GRID_W))[..., :GRID_W * (period - 1)]
    rpb_c = flat.reshape(n_head, n_off, GRID_W, period - 1)[..., :GRID_W]
    return jnp.where(col_valid[None, None], rpb_c, MASK_VALUE)


def _merge_body(h_ref, z_ref, a_ref, g_ref, w_gate_ref, w_ssm_ref, w_na_ref, w_out_ref, o_ref):
    h = h_ref[...]
    hn = _rms_norm(h, g_ref[...]).astype(BF16)
    gates = _dot(hn, w_gate_ref[...])
    y_ssm = _dot(z_ref[...], w_ssm_ref[...])
    y_na = _dot(a_ref[...], w_na_ref[...])
    merged = (jax.nn.sigmoid(gates[:, :D_MODEL]) * y_ssm
              + jax.nn.sigmoid(gates[:, D_MODEL:]) * y_na)
    o_ref[...] = h + _dot(merged.astype(BF16), w_out_ref[...])


def _merge_call(h1, z, a, g, w_gate, w_ssm, w_na, w_out):
    rows = h1.shape[0]
    row_spec = lambda width: pl.BlockSpec((ROW_TILE, width), lambda i: (i, 0))
    return pl.pallas_call(
        _merge_body,
        grid=(rows // ROW_TILE,),
        in_specs=[row_spec(D_MODEL), row_spec(SSM_WIDTH), row_spec(NA_WIDTH),
                  _const_spec((1, D_MODEL)), _const_spec((D_MODEL, 2 * D_MODEL)),
                  _const_spec((SSM_WIDTH, D_MODEL)), _const_spec((NA_WIDTH, D_MODEL)),
                  _const_spec((D_MODEL, D_MODEL))],
        out_specs=row_spec(D_MODEL),
        out_shape=jax.ShapeDtypeStruct((rows, D_MODEL), F32),
        compiler_params=_params(1),
        name="merge",
    )(h1, z, a, g, w_gate, w_ssm, w_na, w_out)


def kernel(x, meta_tokens, norm_ffn1, w_ffn1_in, w_ffn1_out, norm_mix, w_in, ssm_a_re_fwd, ssm_a_im_fwd, ssm_log_dt_fwd, ssm_a_re_bwd, ssm_a_im_bwd, ssm_log_dt_bwd, ssm_b_re, ssm_b_im, ssm_c_re, ssm_c_im, ssm_d, w_glu, na_rpb, w_branch_ssm, w_branch_na, w_out, norm_ffn2, w_ffn2_in, w_ffn2_out, norm_final):
    batch, seq, d_model = x.shape
    assert d_model == D_MODEL and norm_ffn1.shape[0] == 1, "single-layer block of width 1024"
    assert seq % ROW_TILE == 0 and seq % (GRID_W * NA_Q_ROWS) == 0 and batch % 8 == 0
    rows = seq // GRID_W
    assert rows >= NA_K_ROWS + NA_Q_ROWS

    row = lambda a: a.reshape(1, -1).astype(F32)
    g1, gm, g2, gf = row(norm_ffn1[0]), row(norm_mix[0]), row(norm_ffn2[0]), row(norm_final)
    w1_in, w1_out = w_ffn1_in[0].astype(BF16), w_ffn1_out[0].astype(BF16)
    w2_in, w2_out = w_ffn2_in[0].astype(BF16), w_ffn2_out[0].astype(BF16)
    w_uqkv = w_in[0][:, :UQKV_WIDTH].astype(BF16)
    w_gate = w_in[0][:, UQKV_WIDTH:].astype(BF16)

    n_rows = batch * seq
    tok_spec = pl.BlockSpec((ROW_TILE, D_MODEL), lambda i: (i, 0))
    h1 = _ffn_call(x.reshape(n_rows, D_MODEL), g1, w1_in, w1_out, gf, grid=(n_rows // ROW_TILE,),
                   x_spec=tok_spec, o_spec=tok_spec, out_shape=(n_rows, D_MODEL), final_norm=False)
    meta_spec = pl.BlockSpec((N_META, D_MODEL), lambda i: (0, 0))
    h1_meta = _ffn_call(meta_tokens.astype(F32), g1, w1_in, w1_out, gf, grid=(1,),
                        x_spec=meta_spec, o_spec=meta_spec, out_shape=(N_META, D_MODEL),
                        final_norm=False)

    u, q, k, v = _proj_call(h1, gm, w_uqkv, ROW_TILE)
    u_meta, _, k_meta, v_meta = _proj_call(h1_meta, gm, w_uqkv, N_META)
    u = u.reshape(batch, seq, SSM_WIDTH).transpose(1, 0, 2).reshape(n_rows, SSM_WIDTH)

    states = lambda a: a[0].reshape(1, N_STATES).astype(F32)
    per_state = lambda a: jnp.repeat(a[0].astype(F32), SSM_STATE).reshape(1, N_STATES)
    b_re_t = ssm_b_re[0].reshape(N_STATES, SSM_GROUP).T.astype(F32)
    b_im_t = ssm_b_im[0].reshape(N_STATES, SSM_GROUP).T.astype(F32)
    lbf_re, lbf_im, wf_re, wf_im = _disc_call(
        states(ssm_a_re_fwd), states(ssm_a_im_fwd), per_state(ssm_log_dt_fwd), b_re_t, b_im_t)
    lbb_re, lbb_im, wb_re, wb_im = _disc_call(
        states(ssm_a_re_bwd), states(ssm_a_im_bwd), per_state(ssm_log_dt_bwd), b_re_t, b_im_t)
    c_re, c_im = _cslab_call(ssm_c_re[0].reshape(SSM_WIDTH, SSM_STATE).astype(F32),
                             ssm_c_im[0].reshape(SSM_WIDTH, SSM_STATE).astype(F32))

    zero_state = jnp.zeros((batch, 2 * N_STATES), F32)
    u_meta_rows = jnp.repeat(u_meta, batch, axis=0)
    _, state_meta = _s5_call(u_meta_rows, zero_state, lbf_re, lbf_im, wf_re, wf_im, c_re, c_im,
                             None, tile=N_META * batch, reverse=False)
    y_fwd, _ = _s5_call(u, state_meta, lbf_re, lbf_im, wf_re, wf_im, c_re, c_im,
                        None, tile=SCAN_TILE, reverse=False)
    z, _ = _s5_call(u, zero_state, lbb_re, lbb_im, wb_re, wb_im, c_re, c_im,
                    (y_fwd, row(ssm_d[0]), w_glu[0].astype(BF16)), tile=SCAN_TILE, reverse=True)

    pad_meta = lambda a: jnp.pad(a, ((0, NA_META_PAD - N_META), (0, 0)))
    att = _na_call(q, k, v, pad_meta(k_meta), pad_meta(v_meta), _na_col_bias(na_rpb[0]),
                   batch=batch, seq=seq)

    z = z.reshape(seq, batch, SSM_WIDTH).transpose(1, 0, 2).reshape(n_rows, SSM_WIDTH)
    h2 = _merge_call(h1, z, att, gm, w_gate,
                     w_branch_ssm[0].astype(BF16), w_branch_na[0].astype(BF16),
                     w_out[0].astype(BF16))

    out = _ffn_call(h2, g2, w2_in, w2_out, gf, grid=(n_rows // ROW_TILE,), x_spec=tok_spec,
                    o_spec=tok_spec, out_shape=(n_rows, D_MODEL), final_norm=True)
    return out.reshape(batch, seq, D_MODEL)
```

```python
import functools
import math

import jax
import jax.numpy as jnp
from jax import lax
from jax.experimental import pallas as pl
from jax.experimental.pallas import tpu as pltpu

D_MODEL = 1024
N_META = 16
GRID_W = 64
D_FF = 2816
SSM_WIDTH = 512
SSM_GROUP = 16
SSM_GROUPS = 32
SSM_STATE = 64
N_STATES = SSM_GROUPS * SSM_STATE
NA_HEAD_DIM = 64
NA_WIDTH = 512
NA_HEADS = 8
WIN_H = 8
WIN_W = 16
UQKV_WIDTH = SSM_WIDTH + 3 * NA_WIDTH
RMS_EPS = 1e-6
A_RE_MAX = -1e-4
MASK_VALUE = -1e30

LANES = 128
MXU_COLS = 256
VMEM_LIMIT = 56 * 1024 * 1024

ROW_TILE = 512
WIDE_ROW_TILE = 1024
SCAN_COLS = 256
SCAN_TILE = 1024
NA_Q_ROWS = 4
NA_K_ROWS = NA_Q_ROWS + WIN_H - 1
NA_META_PAD = GRID_W

BF16 = jnp.bfloat16
F32 = jnp.float32


def _const_spec(shape):
    zeros = (0,) * len(shape)
    return pl.BlockSpec(shape, lambda *_: zeros, pipeline_mode=pl.Buffered(1))


def _params(n_grid_dims):
    return pltpu.CompilerParams(
        dimension_semantics=("arbitrary",) * n_grid_dims,
        vmem_limit_bytes=VMEM_LIMIT)


def _rms_scale(x):
    return lax.rsqrt(jnp.mean(x * x, axis=-1, keepdims=True) + RMS_EPS)


def _rms_norm(x, g):
    return x * _rms_scale(x) * g


def _dot(a, b):
    return jnp.dot(a, b, preferred_element_type=F32)


def _dot_nt(a, b):
    return lax.dot_general(a, b, (((1,), (1,)), ((), ())), preferred_element_type=F32)


def _ffn_body(x_ref, g_ref, w_gate_ref, w_up_ref, w_out_ref, gf_ref, o_ref, *, final_norm):
    x = x_ref[...]
    r = _rms_scale(x)
    xg = (x * g_ref[...]).astype(BF16)
    gate = _dot(xg, w_gate_ref[...]) * r
    up = _dot(xg, w_up_ref[...]) * r
    act = (jax.nn.silu(gate) * up).astype(BF16)
    y = x + 0.5 * _dot(act, w_out_ref[...])
    if final_norm:
        y = _rms_norm(y, gf_ref[...])
    o_ref[...] = y


def _ffn_call(x, g, w_in, w_out, gf, *, grid, x_spec, o_spec, out_shape, final_norm):
    return pl.pallas_call(
        functools.partial(_ffn_body, final_norm=final_norm),
        grid=grid,
        in_specs=[x_spec, _const_spec((1, D_MODEL)), _const_spec((D_MODEL, D_FF)),
                  _const_spec((D_MODEL, D_FF)), _const_spec((D_FF, D_MODEL)), _const_spec((1, D_MODEL))],
        out_specs=o_spec,
        out_shape=jax.ShapeDtypeStruct(out_shape, F32),
        compiler_params=_params(len(grid)),
        name="ffn",
    )(x, g, w_in[:, :D_FF], w_in[:, D_FF:], w_out, gf)


def _proj_body(h_ref, g_ref, w_ref, u_ref, q_ref, k_ref, v_ref):
    hn = _rms_norm(h_ref[...], g_ref[...]).astype(BF16)
    p = _dot(hn, w_ref[...])
    u_ref[...] = p[:, :SSM_WIDTH]
    q_ref[...] = (p[:, SSM_WIDTH:SSM_WIDTH + NA_WIDTH] * (NA_HEAD_DIM ** -0.5)).astype(BF16)
    k_ref[...] = p[:, SSM_WIDTH + NA_WIDTH:SSM_WIDTH + 2 * NA_WIDTH].astype(BF16)
    v_ref[...] = p[:, SSM_WIDTH + 2 * NA_WIDTH:].astype(BF16)


def _proj_call(h, g, w, tile):
    rows = h.shape[0]
    row_spec = lambda width: pl.BlockSpec((tile, width), lambda i: (i, 0))
    return pl.pallas_call(
        _proj_body,
        grid=(rows // tile,),
        in_specs=[row_spec(D_MODEL), _const_spec((1, D_MODEL)), _const_spec((D_MODEL, UQKV_WIDTH))],
        out_specs=[row_spec(SSM_WIDTH)] * 4,
        out_shape=[jax.ShapeDtypeStruct((rows, SSM_WIDTH), F32)]
        + [jax.ShapeDtypeStruct((rows, NA_WIDTH), BF16)] * 3,
        compiler_params=_params(1),
        name="proj",
    )(h, g, w)


def _same_group(shape, ch_axis, ch0, state0):
    ch = lax.broadcasted_iota(jnp.int32, shape, ch_axis) + ch0
    st = lax.broadcasted_iota(jnp.int32, shape, 1 - ch_axis) + state0
    return (ch >> int(math.log2(SSM_GROUP))) == (st >> int(math.log2(SSM_STATE)))


def _disc_body(a_re_ref, a_im_ref, log_dt_ref, b_re_ref, b_im_ref,
               lb_re_ref, lb_im_ref, w_re_ref, w_im_ref):
    dt = jnp.exp(log_dt_ref[...])
    a_re = jnp.minimum(a_re_ref[...], A_RE_MAX)
    a_im = a_im_ref[...]
    mag = jnp.exp(a_re * dt)
    lb_re = mag * jnp.cos(a_im * dt)
    lb_im = mag * jnp.sin(a_im * dt)
    den = a_re * a_re + a_im * a_im
    nr = lb_re - 1.0
    ni = lb_im
    f_re = (nr * a_re + ni * a_im) / den
    f_im = (ni * a_re - nr * a_im) / den
    lb_re_ref[...] = lb_re
    lb_im_ref[...] = lb_im
    bb_re = f_re * b_re_ref[...] - f_im * b_im_ref[...]
    bb_im = f_re * b_im_ref[...] + f_im * b_re_ref[...]
    for j in range(SCAN_BLOCKS):
        same = _same_group((LANES, SCAN_COLS), 0, (j // 2) * LANES, j * SCAN_COLS)
        for bb, w_ref in ((bb_re, w_re_ref), (bb_im, w_im_ref)):
            rows = jnp.concatenate([bb[:, j * SCAN_COLS:(j + 1) * SCAN_COLS]] * (LANES // SSM_GROUP), axis=0)
            w_ref[j] = jnp.where(same, rows, 0.0).astype(BF16)


def _disc_call(a_re, a_im, log_dt, b_re_t, b_im_t):
    row = jax.ShapeDtypeStruct((1, N_STATES), F32)
    slabs = jax.ShapeDtypeStruct((SCAN_BLOCKS, LANES, SCAN_COLS), BF16)
    return pl.pallas_call(_disc_body, out_shape=[row, row, slabs, slabs], name="disc")(
        a_re, a_im, log_dt, b_re_t, b_im_t)


def _cslab_body(c_re_ref, c_im_ref, ct_re_ref, ct_im_ref):
    for j in range(SCAN_BLOCKS):
        same = _same_group((LANES, SCAN_COLS), 0, (j // 2) * LANES, j * SCAN_COLS)
        for c_ref, ct_ref, sign in ((c_re_ref, ct_re_ref, 1.0), (c_im_ref, ct_im_ref, -1.0)):
            c = c_ref[(j // 2) * LANES:(j // 2 + 1) * LANES, :] * sign
            cols = jnp.concatenate([c] * (SCAN_COLS // SSM_STATE), axis=1)
            ct_ref[j] = jnp.where(same, cols, 0.0).astype(BF16)


def _cslab_call(c_re, c_im):
    slabs = jax.ShapeDtypeStruct((SCAN_BLOCKS, LANES, SCAN_COLS), BF16)
    return pl.pallas_call(_cslab_body, out_shape=[slabs, slabs], name="cslab")(c_re, c_im)


SCAN_BLOCKS = N_STATES // SCAN_COLS
Y_COLS = LANES


def _s5_body(*refs, reverse, steps, finish):
    if finish:
        (u_ref, init_ref, lb_re_ref, lb_im_ref, w_re_ref, w_im_ref, c_re_ref, c_im_ref,
         yf_ref, d_ref, w_glu_ref, o_ref, fin_ref, bu_re, bu_im, h_re, h_im) = refs
    else:
        (u_ref, init_ref, lb_re_ref, lb_im_ref, w_re_ref, w_im_ref, c_re_ref, c_im_ref,
         o_ref, fin_ref, bu_re, bu_im, h_re, h_im) = refs
    nb = init_ref.shape[0]

    @pl.when(pl.program_id(0) == 0)
    def _():
        h_re[...] = init_ref[:, :N_STATES]
        h_im[...] = init_ref[:, N_STATES:]

    def scan_order(ref):
        if not reverse:
            return ref[...]
        return jnp.concatenate(
            [ref[(steps - 1 - s) * nb:(steps - s) * nb, :] for s in range(steps)], axis=0)

    u = scan_order(u_ref)
    ub = u.astype(BF16)
    y_parts = []
    for j in range(SCAN_BLOCKS):
        cols = slice(j * SCAN_COLS, (j + 1) * SCAN_COLS)
        us = ub[:, (j // 2) * LANES:(j // 2 + 1) * LANES]
        bu_re[j] = _dot(us, w_re_ref[j])
        bu_im[j] = _dot(us, w_im_ref[j])
        lr = jnp.broadcast_to(lb_re_ref[:, cols], (nb, SCAN_COLS))
        li = jnp.broadcast_to(lb_im_ref[:, cols], (nb, SCAN_COLS))
        hr, hi = h_re[:, cols], h_im[:, cols]
        for s in range(steps):
            rows = slice(s * nb, (s + 1) * nb)
            hr, hi = (lr * hr - li * hi + bu_re[j, rows, :],
                      lr * hi + li * hr + bu_im[j, rows, :])
            bu_re[j, rows, :] = hr
            bu_im[j, rows, :] = hi
        h_re[:, cols] = hr
        h_im[:, cols] = hi
        part = (_dot_nt(bu_re[j].astype(BF16), c_re_ref[j])
                + _dot_nt(bu_im[j].astype(BF16), c_im_ref[j]))
        if j % 2 == 0:
            y_parts.append(part)
        else:
            y_parts[-1] = y_parts[-1] + part
    y = jnp.concatenate(y_parts, axis=-1)
    fin_ref[:, :N_STATES] = h_re[...]
    fin_ref[:, N_STATES:] = h_im[...]
    if finish:
        y = y + scan_order(yf_ref) + d_ref[...] * u
        z = jax.nn.gelu(y)
        y = (z * jax.nn.sigmoid(_dot(z.astype(BF16), w_glu_ref[...]))).astype(BF16)
    if reverse:
        for s in range(steps):
            o_ref[(steps - 1 - s) * nb:(steps - s) * nb, :] = y[s * nb:(s + 1) * nb, :]
    else:
        o_ref[...] = y


def _s5_call(u, init, lb_re, lb_im, w_re, w_im, c_re, c_im, finish_args, *, tile, reverse):
    rows = u.shape[0]
    nb = init.shape[0]
    n_tiles = rows // tile
    finish = finish_args is not None
    row_map = (lambda i: (n_tiles - 1 - i, 0)) if reverse else (lambda i: (i, 0))
    row_spec = pl.BlockSpec((tile, SSM_WIDTH), row_map)
    in_specs = [row_spec, _const_spec((nb, 2 * N_STATES)),
                _const_spec((1, N_STATES)), _const_spec((1, N_STATES)),
                _const_spec((SCAN_BLOCKS, LANES, SCAN_COLS)), _const_spec((SCAN_BLOCKS, LANES, SCAN_COLS)),
                _const_spec((SCAN_BLOCKS, LANES, SCAN_COLS)), _const_spec((SCAN_BLOCKS, LANES, SCAN_COLS))]
    args = [u, init, lb_re, lb_im, w_re, w_im, c_re, c_im]
    if finish:
        in_specs += [row_spec, _const_spec((1, SSM_WIDTH)), _const_spec((SSM_WIDTH, SSM_WIDTH))]
        args += list(finish_args)
    return pl.pallas_call(
        functools.partial(_s5_body, reverse=reverse, steps=tile // nb, finish=finish),
        grid=(n_tiles,),
        in_specs=in_specs,
        out_specs=[row_spec, pl.BlockSpec((nb, 2 * N_STATES), lambda i: (0, 0))],
        out_shape=[jax.ShapeDtypeStruct((rows, SSM_WIDTH), BF16 if finish else F32),
                   jax.ShapeDtypeStruct((nb, 2 * N_STATES), F32)],
        scratch_shapes=[pltpu.VMEM((SCAN_BLOCKS, tile, SCAN_COLS), F32),
                        pltpu.VMEM((SCAN_BLOCKS, tile, SCAN_COLS), F32),
                        pltpu.VMEM((nb, N_STATES), F32), pltpu.VMEM((nb, N_STATES), F32)],
        compiler_params=_params(1),
        name="s5_bwd" if reverse else "s5_fwd",
    )(*args)


NA_Q = NA_Q_ROWS * GRID_W
NA_K = NA_K_ROWS * GRID_W
NA_KEYS = NA_K + NA_META_PAD


def _block_key_row(blk, rows):
    return min(max(blk * NA_Q_ROWS - WIN_H // 2, 0), rows - NA_K_ROWS)


def _na_assemble_bias(t_ref, bias, n_blocks, rows):
    kh = min(WIN_H, rows)
    masked = jnp.full((GRID_W, GRID_W), MASK_VALUE, F32)
    meta = jnp.where(lax.broadcasted_iota(jnp.int32, (GRID_W, GRID_W), 1) < N_META, 0.0, MASK_VALUE)
    for kind, blk in enumerate((0, 1, n_blocks - 1)):
        for head in range(2):
            for qr in range(NA_Q_ROWS):
                r = blk * NA_Q_ROWS + qr
                row_start = min(max(r - kh // 2, 0), rows - kh)
                tiles = []
                for slot in range(NA_K_ROWS):
                    kr = _block_key_row(blk, rows) + slot
                    valid = row_start <= kr < row_start + kh
                    tiles.append(t_ref[head, kr - r + WIN_H - 1] if valid else masked)
                tiles.append(meta)
                for c in range(0, len(tiles), 2):
                    bias[kind, head, qr * GRID_W:(qr + 1) * GRID_W, c * GRID_W:(c + 2) * GRID_W] = (
                        jnp.concatenate(tiles[c:c + 2], axis=1))


def _na_body(q_ref, k_ref, v_ref, km_ref, vm_ref, t_ref, o_ref, bias, *, n_blocks, rows):
    @pl.when(pl.program_id(1) == 0)
    def _():
        _na_assemble_bias(t_ref, bias, n_blocks, rows)

    first_head = lax.broadcasted_iota(jnp.int32, (1, LANES), 1) < NA_HEAD_DIM
    km = km_ref[...]
    vm = vm_ref[...]

    def block(i, carry):
        key_row = jnp.clip(i * NA_Q_ROWS - WIN_H // 2, 0, rows - NA_K_ROWS)
        k0 = pl.multiple_of(key_row * GRID_W, GRID_W)
        q0 = pl.multiple_of(i * NA_Q, NA_Q)
        kind = jnp.where(i == 0, 0, jnp.where(i == n_blocks - 1, 2, 1))
        q = q_ref[pl.ds(q0, NA_Q), :]
        zero = jnp.zeros_like(q)
        qs = jnp.concatenate([jnp.where(first_head, q, zero), jnp.where(first_head, zero, q)], axis=0)
        keys = jnp.concatenate([k_ref[pl.ds(k0, NA_K), :], km], axis=0)
        vals = jnp.concatenate([v_ref[pl.ds(k0, NA_K), :], vm], axis=0)
        s = _dot_nt(qs, keys) + bias[kind].reshape(2 * NA_Q, NA_KEYS)
        m = jnp.max(s, axis=-1, keepdims=True)
        p = jnp.exp(s - m)
        l = jnp.sum(p, axis=-1, keepdims=True)
        o = _dot(p.astype(BF16), vals) / l
        o_ref[pl.ds(q0, NA_Q), :] = jnp.where(first_head, o[:NA_Q], o[NA_Q:]).astype(BF16)
        return carry

    lax.fori_loop(0, n_blocks, block, 0, unroll=8)


def _na_call(q, k, v, km, vm, col_bias, *, batch, seq):
    rows = seq // GRID_W
    n_pairs = NA_HEADS // 2
    n_blocks = rows // NA_Q_ROWS
    assert NA_META_PAD == GRID_W and (NA_K_ROWS + 1) % 2 == 0
    seq_spec = pl.BlockSpec((seq, LANES), lambda hp, b: (b, hp))
    meta_spec = pl.BlockSpec((NA_META_PAD, LANES), lambda hp, b: (0, hp))
    table_spec = pl.BlockSpec((2, 2 * WIN_H - 1, GRID_W, GRID_W), lambda hp, b: (hp, 0, 0, 0))
    return pl.pallas_call(
        functools.partial(_na_body, n_blocks=n_blocks, rows=rows),
        grid=(n_pairs, batch),
        in_specs=[seq_spec, seq_spec, seq_spec, meta_spec, meta_spec, table_spec],
        out_specs=seq_spec,
        out_shape=jax.ShapeDtypeStruct((batch * seq, NA_WIDTH), BF16),
        scratch_shapes=[pltpu.VMEM((3, 2, NA_Q, NA_KEYS), F32)],
        compiler_params=_params(2),
        name="na",
    )(q, k, v, km, vm, col_bias)


def _na_col_bias(rpb):
    cols = jnp.arange(GRID_W)
    col_start = jnp.clip(cols - WIN_W // 2, 0, GRID_W - WIN_W)
    col_valid = (cols[None, :] >= col_start[:, None]) & (cols[None, :] < col_start[:, None] + WIN_W)
    period = 2 * GRID_W
    n_head, n_off, _ = rpb.shape
    ring = jnp.zeros((n_head, n_off, period), F32)
    ring = ring.at[..., :WIN_W].set(rpb[..., WIN_W - 1:].astype(F32))
    ring = ring.at[..., period - (WIN_W - 1):].set(rpb[..., :WIN_W - 1].astype(F32))
    flat = jnp.tile(ring, (1, 1, GRID_W))[..., :GRID_W * (period - 1)]
    rpb_c = flat.reshape(n_head, n_off, GRID_W, period - 1)[..., :GRID_W]
    return jnp.where(col_valid[None, None], rpb_c, MASK_VALUE)


def _merge_body(h_ref, z_ref, a_ref, g_ref, w_gate_ref, w_ssm_ref, w_na_ref, w_out_ref, o_ref):
    h = h_ref[...]
    hn = _rms_norm(h, g_ref[...]).astype(BF16)
    gates = _dot(hn, w_gate_ref[...])
    y_ssm = _dot(z_ref[...], w_ssm_ref[...])
    y_na = _dot(a_ref[...], w_na_ref[...])
    merged = (jax.nn.sigmoid(gates[:, :D_MODEL]) * y_ssm
              + jax.nn.sigmoid(gates[:, D_MODEL:]) * y_na)
    o_ref[...] = h + _dot(merged.astype(BF16), w_out_ref[...])


def _merge_call(h1, z, a, g, w_gate, w_ssm, w_na, w_out):
    rows = h1.shape[0]
    row_spec = lambda width: pl.BlockSpec((WIDE_ROW_TILE, width), lambda i: (i, 0))
    return pl.pallas_call(
        _merge_body,
        grid=(rows // WIDE_ROW_TILE,),
        in_specs=[row_spec(D_MODEL), row_spec(SSM_WIDTH), row_spec(NA_WIDTH),
                  _const_spec((1, D_MODEL)), _const_spec((D_MODEL, 2 * D_MODEL)),
                  _const_spec((SSM_WIDTH, D_MODEL)), _const_spec((NA_WIDTH, D_MODEL)),
                  _const_spec((D_MODEL, D_MODEL))],
        out_specs=row_spec(D_MODEL),
        out_shape=jax.ShapeDtypeStruct((rows, D_MODEL), F32),
        compiler_params=_params(1),
        name="merge",
    )(h1, z, a, g, w_gate, w_ssm, w_na, w_out)


def kernel(x, meta_tokens, norm_ffn1, w_ffn1_in, w_ffn1_out, norm_mix, w_in, ssm_a_re_fwd, ssm_a_im_fwd, ssm_log_dt_fwd, ssm_a_re_bwd, ssm_a_im_bwd, ssm_log_dt_bwd, ssm_b_re, ssm_b_im, ssm_c_re, ssm_c_im, ssm_d, w_glu, na_rpb, w_branch_ssm, w_branch_na, w_out, norm_ffn2, w_ffn2_in, w_ffn2_out, norm_final):
    batch, seq, d_model = x.shape
    assert d_model == D_MODEL and norm_ffn1.shape[0] == 1, "single-layer block of width 1024"
    assert seq % ROW_TILE == 0 and seq % (GRID_W * NA_Q_ROWS) == 0 and batch % 8 == 0
    rows = seq // GRID_W
    assert rows >= NA_K_ROWS + NA_Q_ROWS

    row = lambda a: a.reshape(1, -1).astype(F32)
    g1, gm, g2, gf = row(norm_ffn1[0]), row(norm_mix[0]), row(norm_ffn2[0]), row(norm_final)
    w1_in, w1_out = w_ffn1_in[0].astype(BF16), w_ffn1_out[0].astype(BF16)
    w2_in, w2_out = w_ffn2_in[0].astype(BF16), w_ffn2_out[0].astype(BF16)
    w_uqkv = w_in[0][:, :UQKV_WIDTH].astype(BF16)
    w_gate = w_in[0][:, UQKV_WIDTH:].astype(BF16)

    n_rows = batch * seq
    tok_spec = pl.BlockSpec((ROW_TILE, D_MODEL), lambda i: (i, 0))
    h1 = _ffn_call(x.reshape(n_rows, D_MODEL), g1, w1_in, w1_out, gf, grid=(n_rows // ROW_TILE,),
                   x_spec=tok_spec, o_spec=tok_spec, out_shape=(n_rows, D_MODEL), final_norm=False)
    meta_spec = pl.BlockSpec((N_META, D_MODEL), lambda i: (0, 0))
    h1_meta = _ffn_call(meta_tokens.astype(F32), g1, w1_in, w1_out, gf, grid=(1,),
                        x_spec=meta_spec, o_spec=meta_spec, out_shape=(N_META, D_MODEL),
                        final_norm=False)

    u, q, k, v = _proj_call(h1, gm, w_uqkv, WIDE_ROW_TILE)
    u_meta, _, k_meta, v_meta = _proj_call(h1_meta, gm, w_uqkv, N_META)
    u = u.reshape(batch, seq, SSM_WIDTH).transpose(1, 0, 2).reshape(n_rows, SSM_WIDTH)

    states = lambda a: a[0].reshape(1, N_STATES).astype(F32)
    per_state = lambda a: jnp.repeat(a[0].astype(F32), SSM_STATE).reshape(1, N_STATES)
    b_re_t = ssm_b_re[0].reshape(N_STATES, SSM_GROUP).T.astype(F32)
    b_im_t = ssm_b_im[0].reshape(N_STATES, SSM_GROUP).T.astype(F32)
    lbf_re, lbf_im, wf_re, wf_im = _disc_call(
        states(ssm_a_re_fwd), states(ssm_a_im_fwd), per_state(ssm_log_dt_fwd), b_re_t, b_im_t)
    lbb_re, lbb_im, wb_re, wb_im = _disc_call(
        states(ssm_a_re_bwd), states(ssm_a_im_bwd), per_state(ssm_log_dt_bwd), b_re_t, b_im_t)
    c_re, c_im = _cslab_call(ssm_c_re[0].reshape(SSM_WIDTH, SSM_STATE).astype(F32),
                             ssm_c_im[0].reshape(SSM_WIDTH, SSM_STATE).astype(F32))

    zero_state = jnp.zeros((batch, 2 * N_STATES), F32)
    u_meta_rows = jnp.repeat(u_meta, batch, axis=0)
    _, state_meta = _s5_call(u_meta_rows, zero_state, lbf_re, lbf_im, wf_re, wf_im, c_re, c_im,
                             None, tile=N_META * batch, reverse=False)
    y_fwd, _ = _s5_call(u, state_meta, lbf_re, lbf_im, wf_re, wf_im, c_re, c_im,
                        None, tile=SCAN_TILE, reverse=False)
    z, _ = _s5_call(u, zero_state, lbb_re, lbb_im, wb_re, wb_im, c_re, c_im,
                    (y_fwd, row(ssm_d[0]), w_glu[0].astype(BF16)), tile=SCAN_TILE, reverse=True)

    pad_meta = lambda a: jnp.pad(a, ((0, NA_META_PAD - N_META), (0, 0)))
    att = _na_call(q, k, v, pad_meta(k_meta), pad_meta(v_meta), _na_col_bias(na_rpb[0]),
                   batch=batch, seq=seq)

    z = z.reshape(seq, batch, SSM_WIDTH).transpose(1, 0, 2).reshape(n_rows, SSM_WIDTH)
    h2 = _merge_call(h1, z, att, gm, w_gate,
                     w_branch_ssm[0].astype(BF16), w_branch_na[0].astype(BF16),
                     w_out[0].astype(BF16))

    out = _ffn_call(h2, g2, w2_in, w2_out, gf, grid=(n_rows // ROW_TILE,), x_spec=tok_spec,
                    o_spec=tok_spec, out_shape=(n_rows, D_MODEL), final_norm=True)
    return out.reshape(batch, seq, D_MODEL)
```

```python
import functools
import math

import jax
import jax.numpy as jnp
from jax import lax
from jax.experimental import pallas as pl
from jax.experimental.pallas import tpu as pltpu

D_MODEL = 1024
N_META = 16
GRID_W = 64
D_FF = 2816
SSM_WIDTH = 512
SSM_GROUP = 16
SSM_GROUPS = 32
SSM_STATE = 64
N_STATES = SSM_GROUPS * SSM_STATE
NA_HEAD_DIM = 64
NA_WIDTH = 512
NA_HEADS = 8
WIN_H = 8
WIN_W = 16
UQKV_WIDTH = SSM_WIDTH + 3 * NA_WIDTH
RMS_EPS = 1e-6
A_RE_MAX = -1e-4
MASK_VALUE = -1e30

LANES = 128
MXU_COLS = 256
VMEM_LIMIT = 56 * 1024 * 1024

ROW_TILE = 512
WIDE_ROW_TILE = 1024
SCAN_COLS = 256
SCAN_TILE = 1024
NA_Q_ROWS = 4
NA_K_ROWS = NA_Q_ROWS + WIN_H - 1
NA_META_PAD = GRID_W

BF16 = jnp.bfloat16
F32 = jnp.float32


def _const_spec(shape):
    zeros = (0,) * len(shape)
    return pl.BlockSpec(shape, lambda *_: zeros, pipeline_mode=pl.Buffered(1))


def _params(n_grid_dims):
    return pltpu.CompilerParams(
        dimension_semantics=("arbitrary",) * n_grid_dims,
        vmem_limit_bytes=VMEM_LIMIT)


def _rms_scale(x):
    return lax.rsqrt(jnp.mean(x * x, axis=-1, keepdims=True) + RMS_EPS)


def _rms_norm(x, g):
    return x * _rms_scale(x) * g


def _dot(a, b):
    return jnp.dot(a, b, preferred_element_type=F32)


def _dot_nt(a, b):
    return lax.dot_general(a, b, (((1,), (1,)), ((), ())), preferred_element_type=F32)


def _ffn_rows(x, g_ref, w_gate_ref, w_up_ref, w_out_ref):
    r = _rms_scale(x)
    xg = (x * g_ref[...]).astype(BF16)
    gate = _dot(xg, w_gate_ref[...]) * r
    up = _dot(xg, w_up_ref[...]) * r
    act = (jax.nn.silu(gate) * up).astype(BF16)
    return x + 0.5 * _dot(act, w_out_ref[...])


def _ffn_body(x_ref, g_ref, w_gate_ref, w_up_ref, w_out_ref, gf_ref, o_ref, *, final_norm):
    y = _ffn_rows(x_ref[...], g_ref, w_gate_ref, w_up_ref, w_out_ref)
    if final_norm:
        y = _rms_norm(y, gf_ref[...])
    o_ref[...] = y


def _proj_rows(h, g_ref, w_ref, u_ref, q_ref, k_ref, v_ref):
    hn = _rms_norm(h, g_ref[...]).astype(BF16)
    p = _dot(hn, w_ref[...])
    u_ref[...] = p[:, :SSM_WIDTH]
    q_ref[...] = (p[:, SSM_WIDTH:SSM_WIDTH + NA_WIDTH] * (NA_HEAD_DIM ** -0.5)).astype(BF16)
    k_ref[...] = p[:, SSM_WIDTH + NA_WIDTH:SSM_WIDTH + 2 * NA_WIDTH].astype(BF16)
    v_ref[...] = p[:, SSM_WIDTH + 2 * NA_WIDTH:].astype(BF16)


def _ffn_proj_body(x_ref, xm_ref, g_ref, w_gate_ref, w_up_ref, w_out_ref, gp_ref, wp_ref,
                   h_ref, u_ref, q_ref, k_ref, v_ref, um_ref, qm_ref, km_ref, vm_ref):
    h = _ffn_rows(x_ref[...], g_ref, w_gate_ref, w_up_ref, w_out_ref)
    h_ref[...] = h
    _proj_rows(h, gp_ref, wp_ref, u_ref, q_ref, k_ref, v_ref)

    @pl.when(pl.program_id(0) == 0)
    def _():
        hm = _ffn_rows(xm_ref[...], g_ref, w_gate_ref, w_up_ref, w_out_ref)
        _proj_rows(hm, gp_ref, wp_ref, um_ref, qm_ref, km_ref, vm_ref)


def _ffn_proj_call(x, x_meta, g, w_in, w_out, g_proj, w_proj):
    rows = x.shape[0]
    row_spec = lambda width: pl.BlockSpec((ROW_TILE, width), lambda i: (i, 0))
    meta_spec = lambda width: pl.BlockSpec((N_META, width), lambda i: (0, 0))
    wide = lambda n, dt: jax.ShapeDtypeStruct((n, SSM_WIDTH), dt)
    return pl.pallas_call(
        _ffn_proj_body,
        grid=(rows // ROW_TILE,),
        in_specs=[row_spec(D_MODEL), meta_spec(D_MODEL), _const_spec((1, D_MODEL)),
                  _const_spec((D_MODEL, D_FF)), _const_spec((D_MODEL, D_FF)), _const_spec((D_FF, D_MODEL)),
                  _const_spec((1, D_MODEL)), _const_spec((D_MODEL, UQKV_WIDTH))],
        out_specs=[row_spec(D_MODEL)] + [row_spec(SSM_WIDTH)] * 4 + [meta_spec(SSM_WIDTH)] * 4,
        out_shape=[jax.ShapeDtypeStruct((rows, D_MODEL), F32), wide(rows, F32)] + [wide(rows, BF16)] * 3
        + [wide(N_META, F32)] + [wide(N_META, BF16)] * 3,
        compiler_params=_params(1),
        name="ffn_proj",
    )(x, x_meta, g, w_in[:, :D_FF], w_in[:, D_FF:], w_out, g_proj, w_proj)


def _ffn_call(x, g, w_in, w_out, gf, *, grid, x_spec, o_spec, out_shape, final_norm):
    return pl.pallas_call(
        functools.partial(_ffn_body, final_norm=final_norm),
        grid=grid,
        in_specs=[x_spec, _const_spec((1, D_MODEL)), _const_spec((D_MODEL, D_FF)),
                  _const_spec((D_MODEL, D_FF)), _const_spec((D_FF, D_MODEL)), _const_spec((1, D_MODEL))],
        out_specs=o_spec,
        out_shape=jax.ShapeDtypeStruct(out_shape, F32),
        compiler_params=_params(len(grid)),
        name="ffn",
    )(x, g, w_in[:, :D_FF], w_in[:, D_FF:], w_out, gf)


def _same_group(shape, ch_axis, ch0, state0):
    ch = lax.broadcasted_iota(jnp.int32, shape, ch_axis) + ch0
    st = lax.broadcasted_iota(jnp.int32, shape, 1 - ch_axis) + state0
    return (ch >> int(math.log2(SSM_GROUP))) == (st >> int(math.log2(SSM_STATE)))


def _disc_body(a_re_ref, a_im_ref, log_dt_ref, b_re_ref, b_im_ref,
               lb_re_ref, lb_im_ref, w_re_ref, w_im_ref):
    dt = jnp.exp(log_dt_ref[...])
    a_re = jnp.minimum(a_re_ref[...], A_RE_MAX)
    a_im = a_im_ref[...]
    mag = jnp.exp(a_re * dt)
    lb_re = mag * jnp.cos(a_im * dt)
    lb_im = mag * jnp.sin(a_im * dt)
    den = a_re * a_re + a_im * a_im
    nr = lb_re - 1.0
    ni = lb_im
    f_re = (nr * a_re + ni * a_im) / den
    f_im = (ni * a_re - nr * a_im) / den
    lb_re_ref[...] = lb_re
    lb_im_ref[...] = lb_im
    bb_re = f_re * b_re_ref[...] - f_im * b_im_ref[...]
    bb_im = f_re * b_im_ref[...] + f_im * b_re_ref[...]
    for j in range(SCAN_BLOCKS):
        same = _same_group((LANES, SCAN_COLS), 0, (j // 2) * LANES, j * SCAN_COLS)
        for bb, w_ref in ((bb_re, w_re_ref), (bb_im, w_im_ref)):
            rows = jnp.concatenate([bb[:, j * SCAN_COLS:(j + 1) * SCAN_COLS]] * (LANES // SSM_GROUP), axis=0)
            w_ref[j] = jnp.where(same, rows, 0.0).astype(BF16)


def _disc_call(a_re, a_im, log_dt, b_re_t, b_im_t):
    row = jax.ShapeDtypeStruct((1, N_STATES), F32)
    slabs = jax.ShapeDtypeStruct((SCAN_BLOCKS, LANES, SCAN_COLS), BF16)
    return pl.pallas_call(_disc_body, out_shape=[row, row, slabs, slabs], name="disc")(
        a_re, a_im, log_dt, b_re_t, b_im_t)


def _cslab_body(c_re_ref, c_im_ref, ct_re_ref, ct_im_ref):
    for j in range(SCAN_BLOCKS):
        same = _same_group((LANES, SCAN_COLS), 0, (j // 2) * LANES, j * SCAN_COLS)
        for c_ref, ct_ref, sign in ((c_re_ref, ct_re_ref, 1.0), (c_im_ref, ct_im_ref, -1.0)):
            c = c_ref[(j // 2) * LANES:(j // 2 + 1) * LANES, :] * sign
            cols = jnp.concatenate([c] * (SCAN_COLS // SSM_STATE), axis=1)
            ct_ref[j] = jnp.where(same, cols, 0.0).astype(BF16)


def _cslab_call(c_re, c_im):
    slabs = jax.ShapeDtypeStruct((SCAN_BLOCKS, LANES, SCAN_COLS), BF16)
    return pl.pallas_call(_cslab_body, out_shape=[slabs, slabs], name="cslab")(c_re, c_im)


SCAN_BLOCKS = N_STATES // SCAN_COLS
Y_COLS = LANES


def _s5_body(*refs, reverse, steps, finish):
    if finish:
        (u_ref, init_ref, lb_re_ref, lb_im_ref, w_re_ref, w_im_ref, c_re_ref, c_im_ref,
         yf_ref, d_ref, w_glu_ref, o_ref, fin_ref, bu_re, bu_im, h_re, h_im) = refs
    else:
        (u_ref, init_ref, lb_re_ref, lb_im_ref, w_re_ref, w_im_ref, c_re_ref, c_im_ref,
         o_ref, fin_ref, bu_re, bu_im, h_re, h_im) = refs
    nb = init_ref.shape[0]

    @pl.when(pl.program_id(0) == 0)
    def _():
        h_re[...] = init_ref[:, :N_STATES]
        h_im[...] = init_ref[:, N_STATES:]

    def scan_order(ref):
        if not reverse:
            return ref[...]
        return jnp.concatenate(
            [ref[(steps - 1 - s) * nb:(steps - s) * nb, :] for s in range(steps)], axis=0)

    u = scan_order(u_ref)
    ub = u.astype(BF16)
    y_parts = []
    for j in range(SCAN_BLOCKS):
        cols = slice(j * SCAN_COLS, (j + 1) * SCAN_COLS)
        us = ub[:, (j // 2) * LANES:(j // 2 + 1) * LANES]
        bu_re[j] = _dot(us, w_re_ref[j])
        bu_im[j] = _dot(us, w_im_ref[j])
        lr = jnp.broadcast_to(lb_re_ref[:, cols], (nb, SCAN_COLS))
        li = jnp.broadcast_to(lb_im_ref[:, cols], (nb, SCAN_COLS))
        hr, hi = h_re[:, cols], h_im[:, cols]
        for s in range(steps):
            rows = slice(s * nb, (s + 1) * nb)
            hr, hi = (lr * hr - li * hi + bu_re[j, rows, :],
                      lr * hi + li * hr + bu_im[j, rows, :])
            bu_re[j, rows, :] = hr
            bu_im[j, rows, :] = hi
        h_re[:, cols] = hr
        h_im[:, cols] = hi
        part = (_dot_nt(bu_re[j].astype(BF16), c_re_ref[j])
                + _dot_nt(bu_im[j].astype(BF16), c_im_ref[j]))
        if j % 2 == 0:
            y_parts.append(part)
        else:
            y_parts[-1] = y_parts[-1] + part
    y = jnp.concatenate(y_parts, axis=-1)
    fin_ref[:, :N_STATES] = h_re[...]
    fin_ref[:, N_STATES:] = h_im[...]
    if finish:
        y = y + scan_order(yf_ref) + d_ref[...] * u
        z = jax.nn.gelu(y)
        y = (z * jax.nn.sigmoid(_dot(z.astype(BF16), w_glu_ref[...]))).astype(BF16)
    if reverse:
        for s in range(steps):
            o_ref[(steps - 1 - s) * nb:(steps - s) * nb, :] = y[s * nb:(s + 1) * nb, :]
    else:
        o_ref[...] = y


def _s5_call(u, init, lb_re, lb_im, w_re, w_im, c_re, c_im, finish_args, *, tile, reverse):
    rows = u.shape[0]
    nb = init.shape[0]
    n_tiles = rows // tile
    finish = finish_args is not None
    row_map = (lambda i: (n_tiles - 1 - i, 0)) if reverse else (lambda i: (i, 0))
    row_spec = pl.BlockSpec((tile, SSM_WIDTH), row_map)
    in_specs = [row_spec, _const_spec((nb, 2 * N_STATES)),
                _const_spec((1, N_STATES)), _const_spec((1, N_STATES)),
                _const_spec((SCAN_BLOCKS, LANES, SCAN_COLS)), _const_spec((SCAN_BLOCKS, LANES, SCAN_COLS)),
                _const_spec((SCAN_BLOCKS, LANES, SCAN_COLS)), _const_spec((SCAN_BLOCKS, LANES, SCAN_COLS))]
    args = [u, init, lb_re, lb_im, w_re, w_im, c_re, c_im]
    if finish:
        in_specs += [row_spec, _const_spec((1, SSM_WIDTH)), _const_spec((SSM_WIDTH, SSM_WIDTH))]
        args += list(finish_args)
    return pl.pallas_call(
        functools.partial(_s5_body, reverse=reverse, steps=tile // nb, finish=finish),
        grid=(n_tiles,),
        in_specs=in_specs,
        out_specs=[row_spec, pl.BlockSpec((nb, 2 * N_STATES), lambda i: (0, 0))],
        out_shape=[jax.ShapeDtypeStruct((rows, SSM_WIDTH), BF16 if finish else F32),
                   jax.ShapeDtypeStruct((nb, 2 * N_STATES), F32)],
        scratch_shapes=[pltpu.VMEM((SCAN_BLOCKS, tile, SCAN_COLS), F32),
                        pltpu.VMEM((SCAN_BLOCKS, tile, SCAN_COLS), F32),
                        pltpu.VMEM((nb, N_STATES), F32), pltpu.VMEM((nb, N_STATES), F32)],
        compiler_params=_params(1),
        name="s5_bwd" if reverse else "s5_fwd",
    )(*args)


NA_Q = NA_Q_ROWS * GRID_W
NA_K = NA_K_ROWS * GRID_W
NA_KEYS = NA_K + NA_META_PAD


def _block_key_row(blk, rows):
    return min(max(blk * NA_Q_ROWS - WIN_H // 2, 0), rows - NA_K_ROWS)


def _na_assemble_bias(t_ref, bias, n_blocks, rows):
    kh = min(WIN_H, rows)
    masked = jnp.full((GRID_W, GRID_W), MASK_VALUE, F32)
    meta = jnp.where(lax.broadcasted_iota(jnp.int32, (GRID_W, GRID_W), 1) < N_META, 0.0, MASK_VALUE)
    for kind, blk in enumerate((0, 1, n_blocks - 1)):
        for head in range(2):
            for qr in range(NA_Q_ROWS):
                r = blk * NA_Q_ROWS + qr
                row_start = min(max(r - kh // 2, 0), rows - kh)
                tiles = []
                for slot in range(NA_K_ROWS):
                    kr = _block_key_row(blk, rows) + slot
                    valid = row_start <= kr < row_start + kh
                    tiles.append(t_ref[head, kr - r + WIN_H - 1] if valid else masked)
                tiles.append(meta)
                for c in range(0, len(tiles), 2):
                    bias[kind, head, qr * GRID_W:(qr + 1) * GRID_W, c * GRID_W:(c + 2) * GRID_W] = (
                        jnp.concatenate(tiles[c:c + 2], axis=1))


def _na_body(q_ref, k_ref, v_ref, km_ref, vm_ref, t_ref, o_ref, bias, *, n_blocks, rows):
    @pl.when(pl.program_id(1) == 0)
    def _():
        _na_assemble_bias(t_ref, bias, n_blocks, rows)

    first_head = lax.broadcasted_iota(jnp.int32, (1, LANES), 1) < NA_HEAD_DIM
    km = km_ref[...]
    vm = vm_ref[...]

    def block(i, carry):
        key_row = jnp.clip(i * NA_Q_ROWS - WIN_H // 2, 0, rows - NA_K_ROWS)
        k0 = pl.multiple_of(key_row * GRID_W, GRID_W)
        q0 = pl.multiple_of(i * NA_Q, NA_Q)
        kind = jnp.where(i == 0, 0, jnp.where(i == n_blocks - 1, 2, 1))
        q = q_ref[pl.ds(q0, NA_Q), :]
        zero = jnp.zeros_like(q)
        qs = jnp.concatenate([jnp.where(first_head, q, zero), jnp.where(first_head, zero, q)], axis=0)
        keys = jnp.concatenate([k_ref[pl.ds(k0, NA_K), :], km], axis=0)
        vals = jnp.concatenate([v_ref[pl.ds(k0, NA_K), :], vm], axis=0)
        s = _dot_nt(qs, keys) + bias[kind].reshape(2 * NA_Q, NA_KEYS)
        m = jnp.max(s, axis=-1, keepdims=True)
        p = jnp.exp(s - m)
        l = jnp.sum(p, axis=-1, keepdims=True)
        o = _dot(p.astype(BF16), vals) / l
        o_ref[pl.ds(q0, NA_Q), :] = jnp.where(first_head, o[:NA_Q], o[NA_Q:]).astype(BF16)
        return carry

    lax.fori_loop(0, n_blocks, block, 0, unroll=8)


def _na_call(q, k, v, km, vm, col_bias, *, batch, seq):
    rows = seq // GRID_W
    n_pairs = NA_HEADS // 2
    n_blocks = rows // NA_Q_ROWS
    assert NA_META_PAD == GRID_W and (NA_K_ROWS + 1) % 2 == 0
    seq_spec = pl.BlockSpec((seq, LANES), lambda hp, b: (b, hp))
    meta_spec = pl.BlockSpec((NA_META_PAD, LANES), lambda hp, b: (0, hp))
    table_spec = pl.BlockSpec((2, 2 * WIN_H - 1, GRID_W, GRID_W), lambda hp, b: (hp, 0, 0, 0))
    return pl.pallas_call(
        functools.partial(_na_body, n_blocks=n_blocks, rows=rows),
        grid=(n_pairs, batch),
        in_specs=[seq_spec, seq_spec, seq_spec, meta_spec, meta_spec, table_spec],
        out_specs=seq_spec,
        out_shape=jax.ShapeDtypeStruct((batch * seq, NA_WIDTH), BF16),
        scratch_shapes=[pltpu.VMEM((3, 2, NA_Q, NA_KEYS), F32)],
        compiler_params=_params(2),
        name="na",
    )(q, k, v, km, vm, col_bias)


def _na_col_bias(rpb):
    cols = jnp.arange(GRID_W)
    col_start = jnp.clip(cols - WIN_W // 2, 0, GRID_W - WIN_W)
    col_valid = (cols[None, :] >= col_start[:, None]) & (cols[None, :] < col_start[:, None] + WIN_W)
    period = 2 * GRID_W
    n_head, n_off, _ = rpb.shape
    ring = jnp.zeros((n_head, n_off, period), F32)
    ring = ring.at[..., :WIN_W].set(rpb[..., WIN_W - 1:].astype(F32))
    ring = ring.at[..., period - (WIN_W - 1):].set(rpb[..., :WIN_W - 1].astype(F32))
    flat = jnp.tile(ring, (1, 1, GRID_W))[..., :GRID_W * (period - 1)]
    rpb_c = flat.reshape(n_head, n_off, GRID_W, period - 1)[..., :GRID_W]
    return jnp.where(col_valid[None, None], rpb_c, MASK_VALUE)


def _merge_body(h_ref, z_ref, a_ref, g_ref, w_gate_ref, w_ssm_ref, w_na_ref, w_out_ref, o_ref):
    h = h_ref[...]
    hn = _rms_norm(h, g_ref[...]).astype(BF16)
    gates = _dot(hn, w_gate_ref[...])
    y_ssm = _dot(z_ref[...], w_ssm_ref[...])
    y_na = _dot(a_ref[...], w_na_ref[...])
    merged = (jax.nn.sigmoid(gates[:, :D_MODEL]) * y_ssm
              + jax.nn.sigmoid(gates[:, D_MODEL:]) * y_na)
    o_ref[...] = h + _dot(merged.astype(BF16), w_out_ref[...])


def _merge_call(h1, z, a, g, w_gate, w_ssm, w_na, w_out):
    rows = h1.shape[0]
    row_spec = lambda width: pl.BlockSpec((WIDE_ROW_TILE, width), lambda i: (i, 0))
    return pl.pallas_call(
        _merge_body,
        grid=(rows // WIDE_ROW_TILE,),
        in_specs=[row_spec(D_MODEL), row_spec(SSM_WIDTH), row_spec(NA_WIDTH),
                  _const_spec((1, D_MODEL)), _const_spec((D_MODEL, 2 * D_MODEL)),
                  _const_spec((SSM_WIDTH, D_MODEL)), _const_spec((NA_WIDTH, D_MODEL)),
                  _const_spec((D_MODEL, D_MODEL))],
        out_specs=row_spec(D_MODEL),
        out_shape=jax.ShapeDtypeStruct((rows, D_MODEL), F32),
        compiler_params=_params(1),
        name="merge",
    )(h1, z, a, g, w_gate, w_ssm, w_na, w_out)


def kernel(x, meta_tokens, norm_ffn1, w_ffn1_in, w_ffn1_out, norm_mix, w_in, ssm_a_re_fwd, ssm_a_im_fwd, ssm_log_dt_fwd, ssm_a_re_bwd, ssm_a_im_bwd, ssm_log_dt_bwd, ssm_b_re, ssm_b_im, ssm_c_re, ssm_c_im, ssm_d, w_glu, na_rpb, w_branch_ssm, w_branch_na, w_out, norm_ffn2, w_ffn2_in, w_ffn2_out, norm_final):
    batch, seq, d_model = x.shape
    assert d_model == D_MODEL and norm_ffn1.shape[0] == 1, "single-layer block of width 1024"
    assert seq % ROW_TILE == 0 and seq % (GRID_W * NA_Q_ROWS) == 0 and batch % 8 == 0
    rows = seq // GRID_W
    assert rows >= NA_K_ROWS + NA_Q_ROWS

    row = lambda a: a.reshape(1, -1).astype(F32)
    g1, gm, g2, gf = row(norm_ffn1[0]), row(norm_mix[0]), row(norm_ffn2[0]), row(norm_final)
    w1_in, w1_out = w_ffn1_in[0].astype(BF16), w_ffn1_out[0].astype(BF16)
    w2_in, w2_out = w_ffn2_in[0].astype(BF16), w_ffn2_out[0].astype(BF16)
    w_uqkv = w_in[0][:, :UQKV_WIDTH].astype(BF16)
    w_gate = w_in[0][:, UQKV_WIDTH:].astype(BF16)

    n_rows = batch * seq
    tok_spec = pl.BlockSpec((ROW_TILE, D_MODEL), lambda i: (i, 0))
    h1, u, q, k, v, u_meta, _, k_meta, v_meta = _ffn_proj_call(
        x.reshape(n_rows, D_MODEL), meta_tokens.astype(F32), g1, w1_in, w1_out, gm, w_uqkv)
    u = u.reshape(batch, seq, SSM_WIDTH).transpose(1, 0, 2).reshape(n_rows, SSM_WIDTH)

    states = lambda a: a[0].reshape(1, N_STATES).astype(F32)
    per_state = lambda a: jnp.repeat(a[0].astype(F32), SSM_STATE).reshape(1, N_STATES)
    b_re_t = ssm_b_re[0].reshape(N_STATES, SSM_GROUP).T.astype(F32)
    b_im_t = ssm_b_im[0].reshape(N_STATES, SSM_GROUP).T.astype(F32)
    lbf_re, lbf_im, wf_re, wf_im = _disc_call(
        states(ssm_a_re_fwd), states(ssm_a_im_fwd), per_state(ssm_log_dt_fwd), b_re_t, b_im_t)
    lbb_re, lbb_im, wb_re, wb_im = _disc_call(
        states(ssm_a_re_bwd), states(ssm_a_im_bwd), per_state(ssm_log_dt_bwd), b_re_t, b_im_t)
    c_re, c_im = _cslab_call(ssm_c_re[0].reshape(SSM_WIDTH, SSM_STATE).astype(F32),
                             ssm_c_im[0].reshape(SSM_WIDTH, SSM_STATE).astype(F32))

    zero_state = jnp.zeros((batch, 2 * N_STATES), F32)
    u_meta_rows = jnp.repeat(u_meta, batch, axis=0)
    _, state_meta = _s5_call(u_meta_rows, zero_state, lbf_re, lbf_im, wf_re, wf_im, c_re, c_im,
                             None, tile=N_META * batch, reverse=False)
    y_fwd, _ = _s5_call(u, state_meta, lbf_re, lbf_im, wf_re, wf_im, c_re, c_im,
                        None, tile=SCAN_TILE, reverse=False)
    z, _ = _s5_call(u, zero_state, lbb_re, lbb_im, wb_re, wb_im, c_re, c_im,
                    (y_fwd, row(ssm_d[0]), w_glu[0].astype(BF16)), tile=SCAN_TILE, reverse=True)

    pad_meta = lambda a: jnp.pad(a, ((0, NA_META_PAD - N_META), (0, 0)))
    att = _na_call(q, k, v, pad_meta(k_meta), pad_meta(v_meta), _na_col_bias(na_rpb[0]),
                   batch=batch, seq=seq)

    z = z.reshape(seq, batch, SSM_WIDTH).transpose(1, 0, 2).reshape(n_rows, SSM_WIDTH)
    h2 = _merge_call(h1, z, att, gm, w_gate,
                     w_branch_ssm[0].astype(BF16), w_branch_na[0].astype(BF16),
                     w_out[0].astype(BF16))

    out = _ffn_call(h2, g2, w2_in, w2_out, gf, grid=(n_rows // ROW_TILE,), x_spec=tok_spec,
                    o_spec=tok_spec, out_shape=(n_rows, D_MODEL), final_norm=True)
    return out.reshape(batch, seq, D_MODEL)
```

```python
import functools
import math

import jax
import jax.numpy as jnp
from jax import lax
from jax.experimental import pallas as pl
from jax.experimental.pallas import tpu as pltpu

D_MODEL = 1024
N_META = 16
GRID_W = 64
D_FF = 2816
SSM_WIDTH = 512
SSM_GROUP = 16
SSM_GROUPS = 32
SSM_STATE = 64
N_STATES = SSM_GROUPS * SSM_STATE
NA_HEAD_DIM = 64
NA_WIDTH = 512
NA_HEADS = 8
WIN_H = 8
WIN_W = 16
UQKV_WIDTH = SSM_WIDTH + 3 * NA_WIDTH
RMS_EPS = 1e-6
A_RE_MAX = -1e-4
MASK_VALUE = -1e30

LANES = 128
MXU_COLS = 256
VMEM_LIMIT = 56 * 1024 * 1024

ROW_TILE = 512
WIDE_ROW_TILE = 1024
SCAN_COLS = 256
SCAN_TILE = 1024
NA_Q_ROWS = 4
NA_K_ROWS = NA_Q_ROWS + WIN_H - 1
NA_META_PAD = GRID_W

BF16 = jnp.bfloat16
F32 = jnp.float32


def _const_spec(shape):
    zeros = (0,) * len(shape)
    return pl.BlockSpec(shape, lambda *_: zeros, pipeline_mode=pl.Buffered(1))


def _params(n_grid_dims):
    return pltpu.CompilerParams(
        dimension_semantics=("arbitrary",) * n_grid_dims,
        vmem_limit_bytes=VMEM_LIMIT)


def _rms_scale(x):
    return lax.rsqrt(jnp.mean(x * x, axis=-1, keepdims=True) + RMS_EPS)


def _rms_norm(x, g):
    return x * _rms_scale(x) * g


def _dot(a, b):
    return jnp.dot(a, b, preferred_element_type=F32)


def _dot_nt(a, b):
    return lax.dot_general(a, b, (((1,), (1,)), ((), ())), preferred_element_type=F32)


def _ffn_rows(x, g_ref, w_gate_ref, w_up_ref, w_out_ref):
    r = _rms_scale(x)
    xg = (x * g_ref[...]).astype(BF16)
    gate = _dot(xg, w_gate_ref[...]) * r
    up = _dot(xg, w_up_ref[...]) * r
    act = (jax.nn.silu(gate) * up).astype(BF16)
    return x + 0.5 * _dot(act, w_out_ref[...])


def _ffn_body(x_ref, g_ref, w_gate_ref, w_up_ref, w_out_ref, gf_ref, o_ref, *, final_norm):
    y = _ffn_rows(x_ref[...], g_ref, w_gate_ref, w_up_ref, w_out_ref)
    if final_norm:
        y = _rms_norm(y, gf_ref[...])
    o_ref[...] = y


def _proj_rows(h, g_ref, w_ref, u_ref, q_ref, k_ref, v_ref):
    hn = _rms_norm(h, g_ref[...]).astype(BF16)
    p = _dot(hn, w_ref[...])
    u_ref[...] = p[:, :SSM_WIDTH]
    q_ref[...] = (p[:, SSM_WIDTH:SSM_WIDTH + NA_WIDTH] * (NA_HEAD_DIM ** -0.5)).astype(BF16)
    k_ref[...] = p[:, SSM_WIDTH + NA_WIDTH:SSM_WIDTH + 2 * NA_WIDTH].astype(BF16)
    v_ref[...] = p[:, SSM_WIDTH + 2 * NA_WIDTH:].astype(BF16)


def _ffn_proj_body(x_ref, xm_ref, g_ref, w_gate_ref, w_up_ref, w_out_ref, gp_ref, wp_ref,
                   h_ref, u_ref, q_ref, k_ref, v_ref, um_ref, qm_ref, km_ref, vm_ref):
    h = _ffn_rows(x_ref[...], g_ref, w_gate_ref, w_up_ref, w_out_ref)
    h_ref[...] = h
    _proj_rows(h, gp_ref, wp_ref, u_ref, q_ref, k_ref, v_ref)

    @pl.when(pl.program_id(0) == 0)
    def _():
        hm = _ffn_rows(xm_ref[...], g_ref, w_gate_ref, w_up_ref, w_out_ref)
        _proj_rows(hm, gp_ref, wp_ref, um_ref, qm_ref, km_ref, vm_ref)


def _ffn_proj_call(x, x_meta, g, w_in, w_out, g_proj, w_proj):
    rows = x.shape[0]
    row_spec = lambda width: pl.BlockSpec((ROW_TILE, width), lambda i: (i, 0))
    meta_spec = lambda width: pl.BlockSpec((N_META, width), lambda i: (0, 0))
    wide = lambda n, dt: jax.ShapeDtypeStruct((n, SSM_WIDTH), dt)
    return pl.pallas_call(
        _ffn_proj_body,
        grid=(rows // ROW_TILE,),
        in_specs=[row_spec(D_MODEL), meta_spec(D_MODEL), _const_spec((1, D_MODEL)),
                  _const_spec((D_MODEL, D_FF)), _const_spec((D_MODEL, D_FF)), _const_spec((D_FF, D_MODEL)),
                  _const_spec((1, D_MODEL)), _const_spec((D_MODEL, UQKV_WIDTH))],
        out_specs=[row_spec(D_MODEL)] + [row_spec(SSM_WIDTH)] * 4 + [meta_spec(SSM_WIDTH)] * 4,
        out_shape=[jax.ShapeDtypeStruct((rows, D_MODEL), F32), wide(rows, F32)] + [wide(rows, BF16)] * 3
        + [wide(N_META, F32)] + [wide(N_META, BF16)] * 3,
        compiler_params=_params(1),
        name="ffn_proj",
    )(x, x_meta, g, w_in[:, :D_FF], w_in[:, D_FF:], w_out, g_proj, w_proj)


def _ffn_call(x, g, w_in, w_out, gf, *, grid, x_spec, o_spec, out_shape, final_norm):
    return pl.pallas_call(
        functools.partial(_ffn_body, final_norm=final_norm),
        grid=grid,
        in_specs=[x_spec, _const_spec((1, D_MODEL)), _const_spec((D_MODEL, D_FF)),
                  _const_spec((D_MODEL, D_FF)), _const_spec((D_FF, D_MODEL)), _const_spec((1, D_MODEL))],
        out_specs=o_spec,
        out_shape=jax.ShapeDtypeStruct(out_shape, F32),
        compiler_params=_params(len(grid)),
        name="ffn",
    )(x, g, w_in[:, :D_FF], w_in[:, D_FF:], w_out, gf)


def _same_group(shape, ch_axis, ch0, state0):
    ch = lax.broadcasted_iota(jnp.int32, shape, ch_axis) + ch0
    st = lax.broadcasted_iota(jnp.int32, shape, 1 - ch_axis) + state0
    return (ch >> int(math.log2(SSM_GROUP))) == (st >> int(math.log2(SSM_STATE)))


def _disc_body(a_re_ref, a_im_ref, log_dt_ref, b_re_ref, b_im_ref,
               lb_re_ref, lb_im_ref, w_re_ref, w_im_ref):
    dt = jnp.exp(log_dt_ref[...])
    a_re = jnp.minimum(a_re_ref[...], A_RE_MAX)
    a_im = a_im_ref[...]
    mag = jnp.exp(a_re * dt)
    lb_re = mag * jnp.cos(a_im * dt)
    lb_im = mag * jnp.sin(a_im * dt)
    den = a_re * a_re + a_im * a_im
    nr = lb_re - 1.0
    ni = lb_im
    f_re = (nr * a_re + ni * a_im) / den
    f_im = (ni * a_re - nr * a_im) / den
    lb_re_ref[...] = lb_re
    lb_im_ref[...] = lb_im
    bb_re = f_re * b_re_ref[...] - f_im * b_im_ref[...]
    bb_im = f_re * b_im_ref[...] + f_im * b_re_ref[...]
    for j in range(SCAN_BLOCKS):
        same = _same_group((LANES, SCAN_COLS), 0, (j // 2) * LANES, j * SCAN_COLS)
        for bb, w_ref in ((bb_re, w_re_ref), (bb_im, w_im_ref)):
            rows = jnp.concatenate([bb[:, j * SCAN_COLS:(j + 1) * SCAN_COLS]] * (LANES // SSM_GROUP), axis=0)
            w_ref[j] = jnp.where(same, rows, 0.0).astype(BF16)


def _disc_call(a_re, a_im, log_dt, b_re_t, b_im_t):
    row = jax.ShapeDtypeStruct((1, N_STATES), F32)
    slabs = jax.ShapeDtypeStruct((SCAN_BLOCKS, LANES, SCAN_COLS), BF16)
    return pl.pallas_call(_disc_body, out_shape=[row, row, slabs, slabs], name="disc")(
        a_re, a_im, log_dt, b_re_t, b_im_t)


def _cslab_body(c_re_ref, c_im_ref, ct_re_ref, ct_im_ref):
    for j in range(SCAN_BLOCKS):
        same = _same_group((LANES, SCAN_COLS), 0, (j // 2) * LANES, j * SCAN_COLS)
        for c_ref, ct_ref, sign in ((c_re_ref, ct_re_ref, 1.0), (c_im_ref, ct_im_ref, -1.0)):
            c = c_ref[(j // 2) * LANES:(j // 2 + 1) * LANES, :] * sign
            cols = jnp.concatenate([c] * (SCAN_COLS // SSM_STATE), axis=1)
            ct_ref[j] = jnp.where(same, cols, 0.0).astype(BF16)


def _cslab_call(c_re, c_im):
    slabs = jax.ShapeDtypeStruct((SCAN_BLOCKS, LANES, SCAN_COLS), BF16)
    return pl.pallas_call(_cslab_body, out_shape=[slabs, slabs], name="cslab")(c_re, c_im)


SCAN_BLOCKS = N_STATES // SCAN_COLS
Y_COLS = LANES


def _s5_body(*refs, reverse, steps, finish):
    if finish:
        (u_ref, init_ref, lb_re_ref, lb_im_ref, w_re_ref, w_im_ref, c_re_ref, c_im_ref,
         yf_ref, d_ref, w_glu_ref, o_ref, fin_ref, bu_re, bu_im, h_re, h_im) = refs
    else:
        (u_ref, init_ref, lb_re_ref, lb_im_ref, w_re_ref, w_im_ref, c_re_ref, c_im_ref,
         o_ref, fin_ref, bu_re, bu_im, h_re, h_im) = refs
    nb = init_ref.shape[0]

    @pl.when(pl.program_id(0) == 0)
    def _():
        h_re[...] = init_ref[:, :N_STATES]
        h_im[...] = init_ref[:, N_STATES:]

    def scan_order(ref):
        if not reverse:
            return ref[...]
        return jnp.concatenate(
            [ref[(steps - 1 - s) * nb:(steps - s) * nb, :] for s in range(steps)], axis=0)

    u = scan_order(u_ref)
    ub = u.astype(BF16)
    y_parts = []
    for j in range(SCAN_BLOCKS):
        cols = slice(j * SCAN_COLS, (j + 1) * SCAN_COLS)
        us = ub[:, (j // 2) * LANES:(j // 2 + 1) * LANES]
        bu_re[j] = _dot(us, w_re_ref[j])
        bu_im[j] = _dot(us, w_im_ref[j])
        lr = jnp.broadcast_to(lb_re_ref[:, cols], (nb, SCAN_COLS))
        li = jnp.broadcast_to(lb_im_ref[:, cols], (nb, SCAN_COLS))
        hr, hi = h_re[:, cols], h_im[:, cols]
        for s in range(steps):
            rows = slice(s * nb, (s + 1) * nb)
            hr, hi = (lr * hr - li * hi + bu_re[j, rows, :],
                      lr * hi + li * hr + bu_im[j, rows, :])
            bu_re[j, rows, :] = hr
            bu_im[j, rows, :] = hi
        h_re[:, cols] = hr
        h_im[:, cols] = hi
        part = (_dot_nt(bu_re[j].astype(BF16), c_re_ref[j])
                + _dot_nt(bu_im[j].astype(BF16), c_im_ref[j]))
        if j % 2 == 0:
            y_parts.append(part)
        else:
            y_parts[-1] = y_parts[-1] + part
    y = jnp.concatenate(y_parts, axis=-1)
    fin_ref[:, :N_STATES] = h_re[...]
    fin_ref[:, N_STATES:] = h_im[...]
    if finish:
        y = y + scan_order(yf_ref) + d_ref[...] * u
        z = jax.nn.gelu(y)
        y = (z * jax.nn.sigmoid(_dot(z.astype(BF16), w_glu_ref[...]))).astype(BF16)
    if reverse:
        for s in range(steps):
            o_ref[(steps - 1 - s) * nb:(steps - s) * nb, :] = y[s * nb:(s + 1) * nb, :]
    else:
        o_ref[...] = y


def _s5_rider_body(*refs, n_in, s5_body, rider_prologue, rider_body):
    s5_in, rider_in = refs[:n_in], refs[n_in:n_in + 6]
    outs, scratch = refs[n_in + 6:n_in + 9], refs[n_in + 9:]
    rider_prologue(*rider_in, outs[2], scratch[4])
    s5_body(*s5_in, *outs[:2], *scratch[:4])
    rider_body(*rider_in, outs[2], scratch[4])


def _s5_call(u, init, lb_re, lb_im, w_re, w_im, c_re, c_im, finish_args, *, tile, reverse, rider=None):
    rows = u.shape[0]
    nb = init.shape[0]
    n_tiles = rows // tile
    finish = finish_args is not None
    row_map = (lambda i: (n_tiles - 1 - i, 0)) if reverse else (lambda i: (i, 0))
    row_spec = pl.BlockSpec((tile, SSM_WIDTH), row_map)
    in_specs = [row_spec, _const_spec((nb, 2 * N_STATES)),
                _const_spec((1, N_STATES)), _const_spec((1, N_STATES)),
                _const_spec((SCAN_BLOCKS, LANES, SCAN_COLS)), _const_spec((SCAN_BLOCKS, LANES, SCAN_COLS)),
                _const_spec((SCAN_BLOCKS, LANES, SCAN_COLS)), _const_spec((SCAN_BLOCKS, LANES, SCAN_COLS))]
    args = [u, init, lb_re, lb_im, w_re, w_im, c_re, c_im]
    if finish:
        in_specs += [row_spec, _const_spec((1, SSM_WIDTH)), _const_spec((SSM_WIDTH, SSM_WIDTH))]
        args += list(finish_args)
    body = functools.partial(_s5_body, reverse=reverse, steps=tile // nb, finish=finish)
    out_specs = [row_spec, pl.BlockSpec((nb, 2 * N_STATES), lambda i: (0, 0))]
    out_shape = [jax.ShapeDtypeStruct((rows, SSM_WIDTH), BF16 if finish else F32),
                 jax.ShapeDtypeStruct((nb, 2 * N_STATES), F32)]
    scratch = [pltpu.VMEM((SCAN_BLOCKS, tile, SCAN_COLS), F32),
               pltpu.VMEM((SCAN_BLOCKS, tile, SCAN_COLS), F32),
               pltpu.VMEM((nb, N_STATES), F32), pltpu.VMEM((nb, N_STATES), F32)]
    if rider is not None:
        assert rider["steps"] == n_tiles, "one rider unit per scan tile"
        body = functools.partial(_s5_rider_body, n_in=len(args), s5_body=body,
                                 rider_prologue=rider["prologue"], rider_body=rider["body"])
        in_specs = in_specs + rider["in_specs"]
        args = args + rider["args"]
        out_specs.append(rider["out_spec"])
        out_shape.append(rider["out_shape"])
        scratch.append(rider["scratch"])
    return pl.pallas_call(
        body,
        grid=(n_tiles,),
        in_specs=in_specs,
        out_specs=out_specs,
        out_shape=out_shape,
        scratch_shapes=scratch,
        compiler_params=_params(1),
        name="s5_bwd" if reverse else "s5_fwd",
    )(*args)


NA_Q = NA_Q_ROWS * GRID_W
NA_K = NA_K_ROWS * GRID_W
NA_KEYS = NA_K + NA_META_PAD


def _block_key_row(blk, rows):
    return min(max(blk * NA_Q_ROWS - WIN_H // 2, 0), rows - NA_K_ROWS)


def _na_assemble_bias(t_ref, bias, n_blocks, rows):
    kh = min(WIN_H, rows)
    masked = jnp.full((GRID_W, GRID_W), MASK_VALUE, F32)
    meta = jnp.where(lax.broadcasted_iota(jnp.int32, (GRID_W, GRID_W), 1) < N_META, 0.0, MASK_VALUE)
    for kind, blk in enumerate((0, 1, n_blocks - 1)):
        for head in range(2):
            for qr in range(NA_Q_ROWS):
                r = blk * NA_Q_ROWS + qr
                row_start = min(max(r - kh // 2, 0), rows - kh)
                tiles = []
                for slot in range(NA_K_ROWS):
                    kr = _block_key_row(blk, rows) + slot
                    valid = row_start <= kr < row_start + kh
                    tiles.append(t_ref[head, kr - r + WIN_H - 1] if valid else masked)
                tiles.append(meta)
                for c in range(0, len(tiles), 2):
                    bias[kind, head, qr * GRID_W:(qr + 1) * GRID_W, c * GRID_W:(c + 2) * GRID_W] = (
                        jnp.concatenate(tiles[c:c + 2], axis=1))


def _na_prologue(q_ref, k_ref, v_ref, km_ref, vm_ref, t_ref, o_ref, bias, *, n_blocks, rows, batch):
    @pl.when(pl.program_id(0) % batch == 0)
    def _():
        _na_assemble_bias(t_ref, bias, n_blocks, rows)


def _na_body(q_ref, k_ref, v_ref, km_ref, vm_ref, t_ref, o_ref, bias, *, n_blocks, rows, batch):
    first_head = lax.broadcasted_iota(jnp.int32, (1, LANES), 1) < NA_HEAD_DIM
    km = km_ref[...]
    vm = vm_ref[...]

    def block(i, carry):
        key_row = jnp.clip(i * NA_Q_ROWS - WIN_H // 2, 0, rows - NA_K_ROWS)
        k0 = pl.multiple_of(key_row * GRID_W, GRID_W)
        q0 = pl.multiple_of(i * NA_Q, NA_Q)
        kind = jnp.where(i == 0, 0, jnp.where(i == n_blocks - 1, 2, 1))
        q = q_ref[pl.ds(q0, NA_Q), :]
        zero = jnp.zeros_like(q)
        qs = jnp.concatenate([jnp.where(first_head, q, zero), jnp.where(first_head, zero, q)], axis=0)
        keys = jnp.concatenate([k_ref[pl.ds(k0, NA_K), :], km], axis=0)
        vals = jnp.concatenate([v_ref[pl.ds(k0, NA_K), :], vm], axis=0)
        s = _dot_nt(qs, keys) + bias[kind].reshape(2 * NA_Q, NA_KEYS)
        m = jnp.max(s, axis=-1, keepdims=True)
        p = jnp.exp(s - m)
        l = jnp.sum(p, axis=-1, keepdims=True)
        o = _dot(p.astype(BF16), vals) / l
        o_ref[pl.ds(q0, NA_Q), :] = jnp.where(first_head, o[:NA_Q], o[NA_Q:]).astype(BF16)
        return carry

    lax.fori_loop(0, n_blocks, block, 0, unroll=8)


def _na_rider(q, k, v, km, vm, col_bias, *, first_pair, n_pairs, batch, seq):
    rows = seq // GRID_W
    assert NA_META_PAD == GRID_W and (NA_K_ROWS + 1) % 2 == 0
    pair = lambda i: first_pair + i // batch
    seq_spec = pl.BlockSpec((seq, LANES), lambda i: (i % batch, pair(i)))
    meta_spec = pl.BlockSpec((NA_META_PAD, LANES), lambda i: (0, pair(i)))
    table_spec = pl.BlockSpec((2, 2 * WIN_H - 1, GRID_W, GRID_W), lambda i: (pair(i), 0, 0, 0))
    return dict(
        steps=n_pairs * batch,
        prologue=functools.partial(_na_prologue, n_blocks=rows // NA_Q_ROWS, rows=rows, batch=batch),
        body=functools.partial(_na_body, n_blocks=rows // NA_Q_ROWS, rows=rows, batch=batch),
        args=[q, k, v, km, vm, col_bias],
        in_specs=[seq_spec, seq_spec, seq_spec, meta_spec, meta_spec, table_spec],
        out_spec=pl.BlockSpec((seq, LANES), lambda i: (i % batch, i // batch)),
        out_shape=jax.ShapeDtypeStruct((batch * seq, n_pairs * LANES), BF16),
        scratch=pltpu.VMEM((3, 2, NA_Q, NA_KEYS), F32))


def _na_col_bias(rpb):
    cols = jnp.arange(GRID_W)
    col_start = jnp.clip(cols - WIN_W // 2, 0, GRID_W - WIN_W)
    col_valid = (cols[None, :] >= col_start[:, None]) & (cols[None, :] < col_start[:, None] + WIN_W)
    period = 2 * GRID_W
    n_head, n_off, _ = rpb.shape
    ring = jnp.zeros((n_head, n_off, period), F32)
    ring = ring.at[..., :WIN_W].set(rpb[..., WIN_W - 1:].astype(F32))
    ring = ring.at[..., period - (WIN_W - 1):].set(rpb[..., :WIN_W - 1].astype(F32))
    flat = jnp.tile(ring, (1, 1, GRID_W))[..., :GRID_W * (period - 1)]
    rpb_c = flat.reshape(n_head, n_off, GRID_W, period - 1)[..., :GRID_W]
    return jnp.where(col_valid[None, None], rpb_c, MASK_VALUE)


def _merge_body(h_ref, z_ref, a_lo_ref, a_hi_ref, g_ref, w_gate_ref, w_ssm_ref, w_na_ref, w_out_ref, o_ref):
    h = h_ref[...]
    hn = _rms_norm(h, g_ref[...]).astype(BF16)
    gates = _dot(hn, w_gate_ref[...])
    y_ssm = _dot(z_ref[...], w_ssm_ref[...])
    y_na = (_dot(a_lo_ref[...], w_na_ref[:NA_WIDTH // 2, :])
            + _dot(a_hi_ref[...], w_na_ref[NA_WIDTH // 2:, :]))
    merged = (jax.nn.sigmoid(gates[:, :D_MODEL]) * y_ssm
              + jax.nn.sigmoid(gates[:, D_MODEL:]) * y_na)
    o_ref[...] = h + _dot(merged.astype(BF16), w_out_ref[...])


def _merge_call(h1, z, a_lo, a_hi, g, w_gate, w_ssm, w_na, w_out):
    rows = h1.shape[0]
    row_spec = lambda width: pl.BlockSpec((WIDE_ROW_TILE, width), lambda i: (i, 0))
    return pl.pallas_call(
        _merge_body,
        grid=(rows // WIDE_ROW_TILE,),
        in_specs=[row_spec(D_MODEL), row_spec(SSM_WIDTH), row_spec(NA_WIDTH // 2), row_spec(NA_WIDTH // 2),
                  _const_spec((1, D_MODEL)), _const_spec((D_MODEL, 2 * D_MODEL)),
                  _const_spec((SSM_WIDTH, D_MODEL)), _const_spec((NA_WIDTH, D_MODEL)),
                  _const_spec((D_MODEL, D_MODEL))],
        out_specs=row_spec(D_MODEL),
        out_shape=jax.ShapeDtypeStruct((rows, D_MODEL), F32),
        compiler_params=_params(1),
        name="merge",
    )(h1, z, a_lo, a_hi, g, w_gate, w_ssm, w_na, w_out)


def kernel(x, meta_tokens, norm_ffn1, w_ffn1_in, w_ffn1_out, norm_mix, w_in, ssm_a_re_fwd, ssm_a_im_fwd, ssm_log_dt_fwd, ssm_a_re_bwd, ssm_a_im_bwd, ssm_log_dt_bwd, ssm_b_re, ssm_b_im, ssm_c_re, ssm_c_im, ssm_d, w_glu, na_rpb, w_branch_ssm, w_branch_na, w_out, norm_ffn2, w_ffn2_in, w_ffn2_out, norm_final):
    batch, seq, d_model = x.shape
    assert d_model == D_MODEL and norm_ffn1.shape[0] == 1, "single-layer block of width 1024"
    assert seq % ROW_TILE == 0 and seq % (GRID_W * NA_Q_ROWS) == 0 and batch % 8 == 0
    rows = seq // GRID_W
    assert rows >= NA_K_ROWS + NA_Q_ROWS

    row = lambda a: a.reshape(1, -1).astype(F32)
    g1, gm, g2, gf = row(norm_ffn1[0]), row(norm_mix[0]), row(norm_ffn2[0]), row(norm_final)
    w1_in, w1_out = w_ffn1_in[0].astype(BF16), w_ffn1_out[0].astype(BF16)
    w2_in, w2_out = w_ffn2_in[0].astype(BF16), w_ffn2_out[0].astype(BF16)
    w_uqkv = w_in[0][:, :UQKV_WIDTH].astype(BF16)
    w_gate = w_in[0][:, UQKV_WIDTH:].astype(BF16)

    n_rows = batch * seq
    tok_spec = pl.BlockSpec((ROW_TILE, D_MODEL), lambda i: (i, 0))
    h1, u, q, k, v, u_meta, _, k_meta, v_meta = _ffn_proj_call(
        x.reshape(n_rows, D_MODEL), meta_tokens.astype(F32), g1, w1_in, w1_out, gm, w_uqkv)
    u = u.reshape(batch, seq, SSM_WIDTH).transpose(1, 0, 2).reshape(n_rows, SSM_WIDTH)

    states = lambda a: a[0].reshape(1, N_STATES).astype(F32)
    per_state = lambda a: jnp.repeat(a[0].astype(F32), SSM_STATE).reshape(1, N_STATES)
    b_re_t = ssm_b_re[0].reshape(N_STATES, SSM_GROUP).T.astype(F32)
    b_im_t = ssm_b_im[0].reshape(N_STATES, SSM_GROUP).T.astype(F32)
    lbf_re, lbf_im, wf_re, wf_im = _disc_call(
        states(ssm_a_re_fwd), states(ssm_a_im_fwd), per_state(ssm_log_dt_fwd), b_re_t, b_im_t)
    lbb_re, lbb_im, wb_re, wb_im = _disc_call(
        states(ssm_a_re_bwd), states(ssm_a_im_bwd), per_state(ssm_log_dt_bwd), b_re_t, b_im_t)
    c_re, c_im = _cslab_call(ssm_c_re[0].reshape(SSM_WIDTH, SSM_STATE).astype(F32),
                             ssm_c_im[0].reshape(SSM_WIDTH, SSM_STATE).astype(F32))

    zero_state = jnp.zeros((batch, 2 * N_STATES), F32)
    u_meta_rows = jnp.repeat(u_meta, batch, axis=0)
    _, state_meta = _s5_call(u_meta_rows, zero_state, lbf_re, lbf_im, wf_re, wf_im, c_re, c_im,
                             None, tile=N_META * batch, reverse=False)
    pad_meta = lambda a: jnp.pad(a, ((0, NA_META_PAD - N_META), (0, 0)))
    half = NA_HEADS // 4
    riders = [_na_rider(q, k, v, pad_meta(k_meta), pad_meta(v_meta), _na_col_bias(na_rpb[0]),
                        first_pair=p0, n_pairs=half, batch=batch, seq=seq) for p0 in (0, half)]
    y_fwd, _, att_lo = _s5_call(u, state_meta, lbf_re, lbf_im, wf_re, wf_im, c_re, c_im,
                                None, tile=SCAN_TILE, reverse=False, rider=riders[0])
    z, _, att_hi = _s5_call(u, zero_state, lbb_re, lbb_im, wb_re, wb_im, c_re, c_im,
                            (y_fwd, row(ssm_d[0]), w_glu[0].astype(BF16)), tile=SCAN_TILE,
                            reverse=True, rider=riders[1])

    z = z.reshape(seq, batch, SSM_WIDTH).transpose(1, 0, 2).reshape(n_rows, SSM_WIDTH)
    h2 = _merge_call(h1, z, att_lo, att_hi, gm, w_gate,
                     w_branch_ssm[0].astype(BF16), w_branch_na[0].astype(BF16),
                     w_out[0].astype(BF16))

    out = _ffn_call(h2, g2, w2_in, w2_out, gf, grid=(n_rows // ROW_TILE,), x_spec=tok_spec,
                    o_spec=tok_spec, out_shape=(n_rows, D_MODEL), final_norm=True)
    return out.reshape(batch, seq, D_MODEL)
```

```python
import functools
import math

import jax
import jax.numpy as jnp
from jax import lax
from jax.experimental import pallas as pl
from jax.experimental.pallas import tpu as pltpu

D_MODEL = 1024
N_META = 16
GRID_W = 64
D_FF = 2816
SSM_WIDTH = 512
SSM_GROUP = 16
SSM_GROUPS = 32
SSM_STATE = 64
N_STATES = SSM_GROUPS * SSM_STATE
NA_HEAD_DIM = 64
NA_WIDTH = 512
NA_HEADS = 8
WIN_H = 8
WIN_W = 16
UQKV_WIDTH = SSM_WIDTH + 3 * NA_WIDTH
RMS_EPS = 1e-6
A_RE_MAX = -1e-4
MASK_VALUE = -1e30

LANES = 128
MXU_COLS = 256
VMEM_LIMIT = 56 * 1024 * 1024

ROW_TILE = 512
WIDE_ROW_TILE = 1024
SCAN_COLS = 256
SCAN_TILE = 1024
NA_Q_ROWS = 4
NA_K_ROWS = NA_Q_ROWS + WIN_H - 1
NA_META_PAD = GRID_W

BF16 = jnp.bfloat16
F32 = jnp.float32


def _const_spec(shape):
    zeros = (0,) * len(shape)
    return pl.BlockSpec(shape, lambda *_: zeros, pipeline_mode=pl.Buffered(1))


def _params(n_grid_dims):
    return pltpu.CompilerParams(
        dimension_semantics=("arbitrary",) * n_grid_dims,
        vmem_limit_bytes=VMEM_LIMIT)


def _rms_scale(x):
    return lax.rsqrt(jnp.mean(x * x, axis=-1, keepdims=True) + RMS_EPS)


def _rms_norm(x, g):
    return x * _rms_scale(x) * g


def _dot(a, b):
    return jnp.dot(a, b, preferred_element_type=F32)


def _dot_nt(a, b):
    return lax.dot_general(a, b, (((1,), (1,)), ((), ())), preferred_element_type=F32)


def _ffn_rows(x, g_ref, w_in_ref, w_out_ref):
    r = _rms_scale(x)
    xg = (x * g_ref[...]).astype(BF16)
    gate = _dot(xg, w_in_ref[:, :D_FF]) * r
    up = _dot(xg, w_in_ref[:, D_FF:]) * r
    act = (jax.nn.silu(gate) * up).astype(BF16)
    return x + 0.5 * _dot(act, w_out_ref[...])


def _ffn_body(x_ref, g_ref, w_in_ref, w_out_ref, gf_ref, o_ref, *, final_norm):
    y = _ffn_rows(x_ref[...], g_ref, w_in_ref, w_out_ref)
    if final_norm:
        y = _rms_norm(y, gf_ref[...])
    o_ref[...] = y


def _proj_rows(h, g_ref, w_ref, u_ref, q_ref, k_ref, v_ref):
    hn = _rms_norm(h, g_ref[...]).astype(BF16)
    p = _dot(hn, w_ref[...])
    u_ref[...] = p[:, :SSM_WIDTH]
    q_ref[...] = (p[:, SSM_WIDTH:SSM_WIDTH + NA_WIDTH] * (NA_HEAD_DIM ** -0.5)).astype(BF16)
    k_ref[...] = p[:, SSM_WIDTH + NA_WIDTH:SSM_WIDTH + 2 * NA_WIDTH].astype(BF16)
    v_ref[...] = p[:, SSM_WIDTH + 2 * NA_WIDTH:].astype(BF16)


def _ffn_proj_body(x_ref, xm_ref, g_ref, w_in_ref, w_out_ref, gp_ref, wp_ref,
                   h_ref, u_ref, q_ref, k_ref, v_ref, um_ref, qm_ref, km_ref, vm_ref):
    h = _ffn_rows(x_ref[...], g_ref, w_in_ref, w_out_ref)
    h_ref[...] = h
    _proj_rows(h, gp_ref, wp_ref, u_ref, q_ref, k_ref, v_ref)

    @pl.when(pl.program_id(0) == 0)
    def _():
        hm = _ffn_rows(xm_ref[...], g_ref, w_in_ref, w_out_ref)
        _proj_rows(hm, gp_ref, wp_ref, um_ref, qm_ref, km_ref, vm_ref)


def _ffn_proj_call(x, x_meta, g, w_in, w_out, g_proj, w_proj):
    rows = x.shape[0]
    row_spec = lambda width: pl.BlockSpec((ROW_TILE, width), lambda i: (i, 0))
    meta_spec = lambda width: pl.BlockSpec((N_META, width), lambda i: (0, 0))
    wide = lambda n, dt: jax.ShapeDtypeStruct((n, SSM_WIDTH), dt)
    return pl.pallas_call(
        _ffn_proj_body,
        grid=(rows // ROW_TILE,),
        in_specs=[row_spec(D_MODEL), meta_spec(D_MODEL), _const_spec((1, D_MODEL)),
                  _const_spec((D_MODEL, 2 * D_FF)), _const_spec((D_FF, D_MODEL)),
                  _const_spec((1, D_MODEL)), _const_spec((D_MODEL, UQKV_WIDTH))],
        out_specs=[row_spec(D_MODEL)] + [row_spec(SSM_WIDTH)] * 4 + [meta_spec(SSM_WIDTH)] * 4,
        out_shape=[jax.ShapeDtypeStruct((rows, D_MODEL), F32), wide(rows, F32)] + [wide(rows, BF16)] * 3
        + [wide(N_META, F32)] + [wide(N_META, BF16)] * 3,
        compiler_params=_params(1),
        name="ffn_proj",
    )(x, x_meta, g, w_in, w_out, g_proj, w_proj)


def _ffn_call(x, g, w_in, w_out, gf, *, grid, x_spec, o_spec, out_shape, final_norm):
    return pl.pallas_call(
        functools.partial(_ffn_body, final_norm=final_norm),
        grid=grid,
        in_specs=[x_spec, _const_spec((1, D_MODEL)), _const_spec((D_MODEL, 2 * D_FF)),
                  _const_spec((D_FF, D_MODEL)), _const_spec((1, D_MODEL))],
        out_specs=o_spec,
        out_shape=jax.ShapeDtypeStruct(out_shape, F32),
        compiler_params=_params(len(grid)),
        name="ffn",
    )(x, g, w_in, w_out, gf)


def _same_group(shape, ch_axis, ch0, state0):
    ch = lax.broadcasted_iota(jnp.int32, shape, ch_axis) + ch0
    st = lax.broadcasted_iota(jnp.int32, shape, 1 - ch_axis) + state0
    return (ch >> int(math.log2(SSM_GROUP))) == (st >> int(math.log2(SSM_STATE)))


def _disc_body(a_re_ref, a_im_ref, log_dt_ref, b_re_ref, b_im_ref,
               lb_re_ref, lb_im_ref, w_re_ref, w_im_ref):
    dt = jnp.exp(log_dt_ref[...])
    a_re = jnp.minimum(a_re_ref[...], A_RE_MAX)
    a_im = a_im_ref[...]
    mag = jnp.exp(a_re * dt)
    lb_re = mag * jnp.cos(a_im * dt)
    lb_im = mag * jnp.sin(a_im * dt)
    den = a_re * a_re + a_im * a_im
    nr = lb_re - 1.0
    ni = lb_im
    f_re = (nr * a_re + ni * a_im) / den
    f_im = (ni * a_re - nr * a_im) / den
    lb_re_ref[...] = lb_re
    lb_im_ref[...] = lb_im
    bb_re = f_re * b_re_ref[...] - f_im * b_im_ref[...]
    bb_im = f_re * b_im_ref[...] + f_im * b_re_ref[...]
    for j in range(SCAN_BLOCKS):
        same = _same_group((LANES, SCAN_COLS), 0, (j // 2) * LANES, j * SCAN_COLS)
        for bb, w_ref in ((bb_re, w_re_ref), (bb_im, w_im_ref)):
            rows = jnp.concatenate([bb[:, j * SCAN_COLS:(j + 1) * SCAN_COLS]] * (LANES // SSM_GROUP), axis=0)
            w_ref[j] = jnp.where(same, rows, 0.0).astype(BF16)


def _disc_call(a_re, a_im, log_dt, b_re_t, b_im_t):
    row = jax.ShapeDtypeStruct((1, N_STATES), F32)
    slabs = jax.ShapeDtypeStruct((SCAN_BLOCKS, LANES, SCAN_COLS), BF16)
    return pl.pallas_call(_disc_body, out_shape=[row, row, slabs, slabs], name="disc")(
        a_re, a_im, log_dt, b_re_t, b_im_t)


def _cslab_body(c_re_ref, c_im_ref, ct_re_ref, ct_im_ref):
    for j in range(SCAN_BLOCKS):
        same = _same_group((LANES, SCAN_COLS), 0, (j // 2) * LANES, j * SCAN_COLS)
        for c_ref, ct_ref, sign in ((c_re_ref, ct_re_ref, 1.0), (c_im_ref, ct_im_ref, -1.0)):
            c = c_ref[(j // 2) * LANES:(j // 2 + 1) * LANES, :] * sign
            cols = jnp.concatenate([c] * (SCAN_COLS // SSM_STATE), axis=1)
            ct_ref[j] = jnp.where(same, cols, 0.0).astype(BF16)


def _cslab_call(c_re, c_im):
    slabs = jax.ShapeDtypeStruct((SCAN_BLOCKS, LANES, SCAN_COLS), BF16)
    return pl.pallas_call(_cslab_body, out_shape=[slabs, slabs], name="cslab")(c_re, c_im)


SCAN_BLOCKS = N_STATES // SCAN_COLS
Y_COLS = LANES


def _s5_body(*refs, reverse, steps, finish):
    if finish:
        (u_ref, init_ref, lb_re_ref, lb_im_ref, w_re_ref, w_im_ref, c_re_ref, c_im_ref,
         yf_ref, d_ref, w_glu_ref, o_ref, fin_ref, bu_re, bu_im, h_re, h_im) = refs
    else:
        (u_ref, init_ref, lb_re_ref, lb_im_ref, w_re_ref, w_im_ref, c_re_ref, c_im_ref,
         o_ref, fin_ref, bu_re, bu_im, h_re, h_im) = refs
    nb = init_ref.shape[0]

    @pl.when(pl.program_id(0) == 0)
    def _():
        h_re[...] = init_ref[:, :N_STATES]
        h_im[...] = init_ref[:, N_STATES:]

    def scan_order(ref):
        if not reverse:
            return ref[...]
        return jnp.concatenate(
            [ref[(steps - 1 - s) * nb:(steps - s) * nb, :] for s in range(steps)], axis=0)

    u = scan_order(u_ref)
    ub = u.astype(BF16)
    y_parts = []
    for j in range(SCAN_BLOCKS):
        cols = slice(j * SCAN_COLS, (j + 1) * SCAN_COLS)
        us = ub[:, (j // 2) * LANES:(j // 2 + 1) * LANES]
        bu_re[j] = _dot(us, w_re_ref[j])
        bu_im[j] = _dot(us, w_im_ref[j])
        lr = jnp.broadcast_to(lb_re_ref[:, cols], (nb, SCAN_COLS))
        li = jnp.broadcast_to(lb_im_ref[:, cols], (nb, SCAN_COLS))
        hr, hi = h_re[:, cols], h_im[:, cols]
        for s in range(steps):
            rows = slice(s * nb, (s + 1) * nb)
            hr, hi = (lr * hr - li * hi + bu_re[j, rows, :],
                      lr * hi + li * hr + bu_im[j, rows, :])
            bu_re[j, rows, :] = hr
            bu_im[j, rows, :] = hi
        h_re[:, cols] = hr
        h_im[:, cols] = hi
        part = (_dot_nt(bu_re[j].astype(BF16), c_re_ref[j])
                + _dot_nt(bu_im[j].astype(BF16), c_im_ref[j]))
        if j % 2 == 0:
            y_parts.append(part)
        else:
            y_parts[-1] = y_parts[-1] + part
    y = jnp.concatenate(y_parts, axis=-1)
    fin_ref[:, :N_STATES] = h_re[...]
    fin_ref[:, N_STATES:] = h_im[...]
    if finish:
        y = y + scan_order(yf_ref) + d_ref[...] * u
        z = jax.nn.gelu(y)
        y = (z * jax.nn.sigmoid(_dot(z.astype(BF16), w_glu_ref[...]))).astype(BF16)
    if reverse:
        for s in range(steps):
            o_ref[(steps - 1 - s) * nb:(steps - s) * nb, :] = y[s * nb:(s + 1) * nb, :]
    else:
        o_ref[...] = y


def _s5_rider_body(*refs, n_in, s5_body, rider_prologue, rider_body):
    s5_in, rider_in = refs[:n_in], refs[n_in:n_in + 6]
    outs, scratch = refs[n_in + 6:n_in + 9], refs[n_in + 9:]
    rider_prologue(*rider_in, outs[2], scratch[4])
    s5_body(*s5_in, *outs[:2], *scratch[:4])
    rider_body(*rider_in, outs[2], scratch[4])


def _s5_call(u, init, lb_re, lb_im, w_re, w_im, c_re, c_im, finish_args, *, tile, reverse, rider=None):
    rows = u.shape[0]
    nb = init.shape[0]
    n_tiles = rows // tile
    finish = finish_args is not None
    row_map = (lambda i: (n_tiles - 1 - i, 0)) if reverse else (lambda i: (i, 0))
    row_spec = pl.BlockSpec((tile, SSM_WIDTH), row_map)
    in_specs = [row_spec, _const_spec((nb, 2 * N_STATES)),
                _const_spec((1, N_STATES)), _const_spec((1, N_STATES)),
                _const_spec((SCAN_BLOCKS, LANES, SCAN_COLS)), _const_spec((SCAN_BLOCKS, LANES, SCAN_COLS)),
                _const_spec((SCAN_BLOCKS, LANES, SCAN_COLS)), _const_spec((SCAN_BLOCKS, LANES, SCAN_COLS))]
    args = [u, init, lb_re, lb_im, w_re, w_im, c_re, c_im]
    if finish:
        in_specs += [row_spec, _const_spec((1, SSM_WIDTH)), _const_spec((SSM_WIDTH, SSM_WIDTH))]
        args += list(finish_args)
    body = functools.partial(_s5_body, reverse=reverse, steps=tile // nb, finish=finish)
    out_specs = [row_spec, pl.BlockSpec((nb, 2 * N_STATES), lambda i: (0, 0))]
    out_shape = [jax.ShapeDtypeStruct((rows, SSM_WIDTH), BF16 if finish else F32),
                 jax.ShapeDtypeStruct((nb, 2 * N_STATES), F32)]
    scratch = [pltpu.VMEM((SCAN_BLOCKS, tile, SCAN_COLS), F32),
               pltpu.VMEM((SCAN_BLOCKS, tile, SCAN_COLS), F32),
               pltpu.VMEM((nb, N_STATES), F32), pltpu.VMEM((nb, N_STATES), F32)]
    if rider is not None:
        assert rider["steps"] == n_tiles, "one rider unit per scan tile"
        body = functools.partial(_s5_rider_body, n_in=len(args), s5_body=body,
                                 rider_prologue=rider["prologue"], rider_body=rider["body"])
        in_specs = in_specs + rider["in_specs"]
        args = args + rider["args"]
        out_specs.append(rider["out_spec"])
        out_shape.append(rider["out_shape"])
        scratch.append(rider["scratch"])
    return pl.pallas_call(
        body,
        grid=(n_tiles,),
        in_specs=in_specs,
        out_specs=out_specs,
        out_shape=out_shape,
        scratch_shapes=scratch,
        compiler_params=_params(1),
        name="s5_bwd" if reverse else "s5_fwd",
    )(*args)


NA_Q = NA_Q_ROWS * GRID_W
NA_K = NA_K_ROWS * GRID_W
NA_KEYS = NA_K + NA_META_PAD
NA_RING = LANES


def _block_key_row(blk, rows):
    return min(max(blk * NA_Q_ROWS - WIN_H // 2, 0), rows - NA_K_ROWS)


def _na_assemble_bias(ring_ref, bias, n_blocks, rows):
    kh = min(WIN_H, rows)
    masked = jnp.full((GRID_W, GRID_W), MASK_VALUE, F32)
    meta = jnp.where(lax.broadcasted_iota(jnp.int32, (GRID_W, GRID_W), 1) < N_META, 0.0, MASK_VALUE)
    cq = lax.broadcasted_iota(jnp.int32, (GRID_W, GRID_W), 0)
    ck = lax.broadcasted_iota(jnp.int32, (GRID_W, GRID_W), 1)
    col_start = jnp.clip(cq - WIN_W // 2, 0, GRID_W - WIN_W)
    col_valid = (ck >= col_start) & (ck < col_start + WIN_W)
    col_tiles = {}

    def col_tile(head, off):
        if (head, off) not in col_tiles:
            ring = jnp.broadcast_to(ring_ref[head, off:off + 1, :], (GRID_W, NA_RING))
            toeplitz = pltpu.roll(ring, 0, 1, stride=1, stride_axis=0)[:, :GRID_W]
            col_tiles[head, off] = jnp.where(col_valid, toeplitz, MASK_VALUE)
        return col_tiles[head, off]

    for kind, blk in enumerate((0, 1, n_blocks - 1)):
        for head in range(2):
            for qr in range(NA_Q_ROWS):
                r = blk * NA_Q_ROWS + qr
                row_start = min(max(r - kh // 2, 0), rows - kh)
                tiles = []
                for slot in range(NA_K_ROWS):
                    kr = _block_key_row(blk, rows) + slot
                    valid = row_start <= kr < row_start + kh
                    tiles.append(col_tile(head, kr - r + WIN_H - 1) if valid else masked)
                tiles.append(meta)
                for c in range(0, len(tiles), 2):
                    bias[kind, head, qr * GRID_W:(qr + 1) * GRID_W, c * GRID_W:(c + 2) * GRID_W] = (
                        jnp.concatenate(tiles[c:c + 2], axis=1))


def _na_prologue(q_ref, k_ref, v_ref, km_ref, vm_ref, t_ref, o_ref, bias, *, n_blocks, rows, batch):
    @pl.when(pl.program_id(0) % batch == 0)
    def _():
        _na_assemble_bias(t_ref, bias, n_blocks, rows)


def _na_body(q_ref, k_ref, v_ref, km_ref, vm_ref, t_ref, o_ref, bias, *, n_blocks, rows, batch):
    first_head = lax.broadcasted_iota(jnp.int32, (1, LANES), 1) < NA_HEAD_DIM
    km = km_ref[...]
    vm = vm_ref[...]

    def block(i, carry):
        key_row = jnp.clip(i * NA_Q_ROWS - WIN_H // 2, 0, rows - NA_K_ROWS)
        k0 = pl.multiple_of(key_row * GRID_W, GRID_W)
        q0 = pl.multiple_of(i * NA_Q, NA_Q)
        kind = jnp.where(i == 0, 0, jnp.where(i == n_blocks - 1, 2, 1))
        q = q_ref[pl.ds(q0, NA_Q), :]
        zero = jnp.zeros_like(q)
        qs = jnp.concatenate([jnp.where(first_head, q, zero), jnp.where(first_head, zero, q)], axis=0)
        keys = jnp.concatenate([k_ref[pl.ds(k0, NA_K), :], km], axis=0)
        vals = jnp.concatenate([v_ref[pl.ds(k0, NA_K), :], vm], axis=0)
        s = _dot_nt(qs, keys) + bias[kind].reshape(2 * NA_Q, NA_KEYS)
        m = jnp.max(s, axis=-1, keepdims=True)
        p = jnp.exp(s - m)
        l = jnp.sum(p, axis=-1, keepdims=True)
        o = _dot(p.astype(BF16), vals) / l
        o_ref[pl.ds(q0, NA_Q), :] = jnp.where(first_head, o[:NA_Q], o[NA_Q:]).astype(BF16)
        return carry

    lax.fori_loop(0, n_blocks, block, 0, unroll=8)


def _na_rider(q, k, v, km, vm, col_bias, *, first_pair, n_pairs, batch, seq):
    rows = seq // GRID_W
    assert NA_META_PAD == GRID_W and (NA_K_ROWS + 1) % 2 == 0
    pair = lambda i: first_pair + i // batch
    seq_spec = pl.BlockSpec((seq, LANES), lambda i: (i % batch, pair(i)))
    meta_spec = pl.BlockSpec((NA_META_PAD, LANES), lambda i: (0, pair(i)))
    table_spec = pl.BlockSpec((2, 2 * WIN_H - 1, NA_RING), lambda i: (pair(i), 0, 0))
    return dict(
        steps=n_pairs * batch,
        prologue=functools.partial(_na_prologue, n_blocks=rows // NA_Q_ROWS, rows=rows, batch=batch),
        body=functools.partial(_na_body, n_blocks=rows // NA_Q_ROWS, rows=rows, batch=batch),
        args=[q, k, v, km, vm, col_bias],
        in_specs=[seq_spec, seq_spec, seq_spec, meta_spec, meta_spec, table_spec],
        out_spec=pl.BlockSpec((seq, LANES), lambda i: (i % batch, i // batch)),
        out_shape=jax.ShapeDtypeStruct((batch * seq, n_pairs * LANES), BF16),
        scratch=pltpu.VMEM((3, 2, NA_Q, NA_KEYS), F32))


def _na_col_ring(rpb):
    n_head, n_off, _ = rpb.shape
    ring = jnp.zeros((n_head, n_off, NA_RING), F32)
    ring = ring.at[..., :WIN_W].set(rpb[..., WIN_W - 1:].astype(F32))
    return ring.at[..., NA_RING - (WIN_W - 1):].set(rpb[..., :WIN_W - 1].astype(F32))


def _merge_body(h_ref, z_ref, a_lo_ref, a_hi_ref, g_ref, w_gate_ref, w_ssm_ref, w_na_ref, w_out_ref, o_ref):
    h = h_ref[...]
    hn = _rms_norm(h, g_ref[...]).astype(BF16)
    gates = _dot(hn, w_gate_ref[...])
    y_ssm = _dot(z_ref[...], w_ssm_ref[...])
    y_na = (_dot(a_lo_ref[...], w_na_ref[:NA_WIDTH // 2, :])
            + _dot(a_hi_ref[...], w_na_ref[NA_WIDTH // 2:, :]))
    merged = (jax.nn.sigmoid(gates[:, :D_MODEL]) * y_ssm
              + jax.nn.sigmoid(gates[:, D_MODEL:]) * y_na)
    o_ref[...] = h + _dot(merged.astype(BF16), w_out_ref[...])


def _merge_call(h1, z, a_lo, a_hi, g, w_gate, w_ssm, w_na, w_out):
    rows = h1.shape[0]
    row_spec = lambda width: pl.BlockSpec((WIDE_ROW_TILE, width), lambda i: (i, 0))
    return pl.pallas_call(
        _merge_body,
        grid=(rows // WIDE_ROW_TILE,),
        in_specs=[row_spec(D_MODEL), row_spec(SSM_WIDTH), row_spec(NA_WIDTH // 2), row_spec(NA_WIDTH // 2),
                  _const_spec((1, D_MODEL)), _const_spec((D_MODEL, 2 * D_MODEL)),
                  _const_spec((SSM_WIDTH, D_MODEL)), _const_spec((NA_WIDTH, D_MODEL)),
                  _const_spec((D_MODEL, D_MODEL))],
        out_specs=row_spec(D_MODEL),
        out_shape=jax.ShapeDtypeStruct((rows, D_MODEL), F32),
        compiler_params=_params(1),
        name="merge",
    )(h1, z, a_lo, a_hi, g, w_gate, w_ssm, w_na, w_out)


def kernel(x, meta_tokens, norm_ffn1, w_ffn1_in, w_ffn1_out, norm_mix, w_in, ssm_a_re_fwd, ssm_a_im_fwd, ssm_log_dt_fwd, ssm_a_re_bwd, ssm_a_im_bwd, ssm_log_dt_bwd, ssm_b_re, ssm_b_im, ssm_c_re, ssm_c_im, ssm_d, w_glu, na_rpb, w_branch_ssm, w_branch_na, w_out, norm_ffn2, w_ffn2_in, w_ffn2_out, norm_final):
    batch, seq, d_model = x.shape
    assert d_model == D_MODEL and norm_ffn1.shape[0] == 1, "single-layer block of width 1024"
    assert seq % ROW_TILE == 0 and seq % (GRID_W * NA_Q_ROWS) == 0 and batch % 8 == 0
    rows = seq // GRID_W
    assert rows >= NA_K_ROWS + NA_Q_ROWS

    row = lambda a: a.reshape(1, -1).astype(F32)
    g1, gm, g2, gf = row(norm_ffn1[0]), row(norm_mix[0]), row(norm_ffn2[0]), row(norm_final)
    w1_in, w1_out = w_ffn1_in[0].astype(BF16), w_ffn1_out[0].astype(BF16)
    w2_in, w2_out = w_ffn2_in[0].astype(BF16), w_ffn2_out[0].astype(BF16)
    w_uqkv = w_in[0][:, :UQKV_WIDTH].astype(BF16)
    w_gate = w_in[0][:, UQKV_WIDTH:].astype(BF16)

    n_rows = batch * seq
    tok_spec = pl.BlockSpec((ROW_TILE, D_MODEL), lambda i: (i, 0))
    h1, u, q, k, v, u_meta, _, k_meta, v_meta = _ffn_proj_call(
        x.reshape(n_rows, D_MODEL), meta_tokens.astype(F32), g1, w1_in, w1_out, gm, w_uqkv)
    u = u.reshape(batch, seq, SSM_WIDTH).transpose(1, 0, 2).reshape(n_rows, SSM_WIDTH)

    states = lambda a: a[0].reshape(1, N_STATES).astype(F32)
    per_state = lambda a: jnp.repeat(a[0].astype(F32), SSM_STATE).reshape(1, N_STATES)
    b_re_t = ssm_b_re[0].reshape(N_STATES, SSM_GROUP).T.astype(F32)
    b_im_t = ssm_b_im[0].reshape(N_STATES, SSM_GROUP).T.astype(F32)
    lbf_re, lbf_im, wf_re, wf_im = _disc_call(
        states(ssm_a_re_fwd), states(ssm_a_im_fwd), per_state(ssm_log_dt_fwd), b_re_t, b_im_t)
    lbb_re, lbb_im, wb_re, wb_im = _disc_call(
        states(ssm_a_re_bwd), states(ssm_a_im_bwd), per_state(ssm_log_dt_bwd), b_re_t, b_im_t)
    c_re, c_im = _cslab_call(ssm_c_re[0].reshape(SSM_WIDTH, SSM_STATE).astype(F32),
                             ssm_c_im[0].reshape(SSM_WIDTH, SSM_STATE).astype(F32))

    zero_state = jnp.zeros((batch, 2 * N_STATES), F32)
    u_meta_rows = jnp.repeat(u_meta, batch, axis=0)
    _, state_meta = _s5_call(u_meta_rows, zero_state, lbf_re, lbf_im, wf_re, wf_im, c_re, c_im,
                             None, tile=N_META * batch, reverse=False)
    pad_meta = lambda a: jnp.pad(a, ((0, NA_META_PAD - N_META), (0, 0)))
    half = NA_HEADS // 4
    riders = [_na_rider(q, k, v, pad_meta(k_meta), pad_meta(v_meta), _na_col_ring(na_rpb[0]),
                        first_pair=p0, n_pairs=half, batch=batch, seq=seq) for p0 in (0, half)]
    y_fwd, _, att_lo = _s5_call(u, state_meta, lbf_re, lbf_im, wf_re, wf_im, c_re, c_im,
                                None, tile=SCAN_TILE, reverse=False, rider=riders[0])
    z, _, att_hi = _s5_call(u, zero_state, lbb_re, lbb_im, wb_re, wb_im, c_re, c_im,
                            (y_fwd, row(ssm_d[0]), w_glu[0].astype(BF16)), tile=SCAN_TILE,
                            reverse=True, rider=riders[1])

    z = z.reshape(seq, batch, SSM_WIDTH).transpose(1, 0, 2).reshape(n_rows, SSM_WIDTH)
    h2 = _merge_call(h1, z, att_lo, att_hi, gm, w_gate,
                     w_branch_ssm[0].astype(BF16), w_branch_na[0].astype(BF16),
                     w_out[0].astype(BF16))

    out = _ffn_call(h2, g2, w2_in, w2_out, gf, grid=(n_rows // ROW_TILE,), x_spec=tok_spec,
                    o_spec=tok_spec, out_shape=(n_rows, D_MODEL), final_norm=True)
    return out.reshape(batch, seq, D_MODEL)
```

```python
import functools
import math

import jax
import jax.numpy as jnp
from jax import lax
from jax.experimental import pallas as pl
from jax.experimental.pallas import tpu as pltpu

D_MODEL = 1024
N_META = 16
GRID_W = 64
D_FF = 2816
SSM_WIDTH = 512
SSM_GROUP = 16
SSM_GROUPS = 32
SSM_STATE = 64
N_STATES = SSM_GROUPS * SSM_STATE
NA_HEAD_DIM = 64
NA_WIDTH = 512
NA_HEADS = 8
WIN_H = 8
WIN_W = 16
UQKV_WIDTH = SSM_WIDTH + 3 * NA_WIDTH
RMS_EPS = 1e-6
A_RE_MAX = -1e-4
MASK_VALUE = -1e30

LANES = 128
BF16_ROWS = 16
MXU_COLS = 256
VMEM_LIMIT = 56 * 1024 * 1024

ROW_TILE = 512
WIDE_ROW_TILE = 1024
SCAN_COLS = 256
SCAN_TILE = 1024
NA_Q_ROWS = 4
NA_K_ROWS = NA_Q_ROWS + WIN_H - 1
NA_META_PAD = GRID_W

BF16 = jnp.bfloat16
F32 = jnp.float32


def _const_spec(shape):
    zeros = (0,) * len(shape)
    return pl.BlockSpec(shape, lambda *_: zeros, pipeline_mode=pl.Buffered(1))


def _params(n_grid_dims):
    return pltpu.CompilerParams(
        dimension_semantics=("arbitrary",) * n_grid_dims,
        vmem_limit_bytes=VMEM_LIMIT)


def _rms_scale(x):
    return lax.rsqrt(jnp.mean(x * x, axis=-1, keepdims=True) + RMS_EPS)


def _rms_norm(x, g):
    return x * _rms_scale(x) * g


def _dot(a, b):
    return jnp.dot(a, b, preferred_element_type=F32)


def _dot_nt(a, b):
    return lax.dot_general(a, b, (((1,), (1,)), ((), ())), preferred_element_type=F32)


def _ffn_rows(x, g_ref, w_in_ref, w_out_ref):
    r = _rms_scale(x)
    xg = (x * g_ref[...]).astype(BF16)
    gate = _dot(xg, w_in_ref[:, :D_FF]) * r
    up = _dot(xg, w_in_ref[:, D_FF:]) * r
    act = (jax.nn.silu(gate) * up).astype(BF16)
    return x + 0.5 * _dot(act, w_out_ref[...])


def _ffn_body(x_ref, g_ref, w_in_ref, w_out_ref, gf_ref, o_ref, *, final_norm):
    y = _ffn_rows(x_ref[...], g_ref, w_in_ref, w_out_ref)
    if final_norm:
        y = _rms_norm(y, gf_ref[...])
    o_ref[...] = y


def _proj_rows(h, g_ref, w_ref, u_ref, q_ref, k_ref, v_ref):
    hn = _rms_norm(h, g_ref[...]).astype(BF16)
    p = _dot(hn, w_ref[...])
    u_ref[...] = p[:, :SSM_WIDTH]
    q_ref[...] = (p[:, SSM_WIDTH:SSM_WIDTH + NA_WIDTH] * (NA_HEAD_DIM ** -0.5)).astype(BF16)
    k_ref[...] = p[:, SSM_WIDTH + NA_WIDTH:SSM_WIDTH + 2 * NA_WIDTH].astype(BF16)
    v_ref[...] = p[:, SSM_WIDTH + 2 * NA_WIDTH:].astype(BF16)


N_FFN_PROJ_IN = 7
N_FFN_PROJ_OUT = 9


def _ffn_proj_body(*refs):
    x_ref, xm_ref, g_ref, w_in_ref, w_out_ref, gp_ref, wp_ref = refs[:N_FFN_PROJ_IN]
    n_cast = (len(refs) - N_FFN_PROJ_IN - N_FFN_PROJ_OUT) // 2
    cast_in = refs[N_FFN_PROJ_IN:N_FFN_PROJ_IN + n_cast]
    outs = refs[N_FFN_PROJ_IN + n_cast:]
    h_ref, u_ref, q_ref, k_ref, v_ref, um_ref, qm_ref, km_ref, vm_ref = outs[:N_FFN_PROJ_OUT]
    cast_out = outs[N_FFN_PROJ_OUT:]

    @pl.when(pl.program_id(0) == 0)
    def _():
        hm = _ffn_rows(xm_ref[...], g_ref, w_in_ref, w_out_ref)
        _proj_rows(hm, gp_ref, wp_ref, um_ref, qm_ref, km_ref, vm_ref)

    h = _ffn_rows(x_ref[...], g_ref, w_in_ref, w_out_ref)
    h_ref[...] = h
    _proj_rows(h, gp_ref, wp_ref, u_ref, q_ref, k_ref, v_ref)
    for src, dst in zip(cast_in, cast_out):
        dst[...] = src[...].astype(BF16)


def _ffn_proj_call(x, x_meta, g, w_in, w_out, g_proj, w_proj, to_cast):
    rows = x.shape[0]
    steps = rows // ROW_TILE
    row_spec = lambda width: pl.BlockSpec((ROW_TILE, width), lambda i: (i, 0))
    meta_spec = lambda width: pl.BlockSpec((N_META, width), lambda i: (0, 0))
    wide = lambda n, dt: jax.ShapeDtypeStruct((n, SSM_WIDTH), dt)

    def cast_specs(a, col_block, n_cols):
        n = a.shape[0]
        chunk = min(c for c in range(BF16_ROWS, n + 1, BF16_ROWS) if n % c == 0 and n // c <= steps)
        last = n // chunk - 1
        return (pl.BlockSpec((chunk, n_cols), lambda i: (jnp.minimum(i, last), col_block)),
                pl.BlockSpec((chunk, n_cols), lambda i: (jnp.minimum(i, last), 0)))

    cast_in_specs, cast_out_specs = zip(*[cast_specs(*c) for c in to_cast])
    cast_shapes = [jax.ShapeDtypeStruct((a.shape[0], n_cols), BF16) for a, _, n_cols in to_cast]
    return pl.pallas_call(
        _ffn_proj_body,
        grid=(steps,),
        in_specs=[row_spec(D_MODEL), meta_spec(D_MODEL), _const_spec((1, D_MODEL)),
                  _const_spec((D_MODEL, 2 * D_FF)), _const_spec((D_FF, D_MODEL)),
                  _const_spec((1, D_MODEL)), _const_spec((D_MODEL, UQKV_WIDTH))] + list(cast_in_specs),
        out_specs=[row_spec(D_MODEL)] + [row_spec(SSM_WIDTH)] * 4 + [meta_spec(SSM_WIDTH)] * 4
        + list(cast_out_specs),
        out_shape=[jax.ShapeDtypeStruct((rows, D_MODEL), F32), wide(rows, F32)] + [wide(rows, BF16)] * 3
        + [wide(N_META, F32)] + [wide(N_META, BF16)] * 3 + cast_shapes,
        compiler_params=_params(1),
        name="ffn_proj",
    )(x, x_meta, g, w_in, w_out, g_proj, w_proj, *[a for a, _, _ in to_cast])


def _ffn_call(x, g, w_in, w_out, gf, *, grid, x_spec, o_spec, out_shape, final_norm):
    return pl.pallas_call(
        functools.partial(_ffn_body, final_norm=final_norm),
        grid=grid,
        in_specs=[x_spec, _const_spec((1, D_MODEL)), _const_spec((D_MODEL, 2 * D_FF)),
                  _const_spec((D_FF, D_MODEL)), _const_spec((1, D_MODEL))],
        out_specs=o_spec,
        out_shape=jax.ShapeDtypeStruct(out_shape, F32),
        compiler_params=_params(len(grid)),
        name="ffn",
    )(x, g, w_in, w_out, gf)


def _same_group(shape, ch_axis, ch0, state0):
    ch = lax.broadcasted_iota(jnp.int32, shape, ch_axis) + ch0
    st = lax.broadcasted_iota(jnp.int32, shape, 1 - ch_axis) + state0
    return (ch >> int(math.log2(SSM_GROUP))) == (st >> int(math.log2(SSM_STATE)))


def _disc_body(a_re_ref, a_im_ref, log_dt_ref, b_re_ref, b_im_ref,
               lb_re_ref, lb_im_ref, w_re_ref, w_im_ref):
    dt = jnp.exp(log_dt_ref[...])
    a_re = jnp.minimum(a_re_ref[...], A_RE_MAX)
    a_im = a_im_ref[...]
    mag = jnp.exp(a_re * dt)
    lb_re = mag * jnp.cos(a_im * dt)
    lb_im = mag * jnp.sin(a_im * dt)
    den = a_re * a_re + a_im * a_im
    nr = lb_re - 1.0
    ni = lb_im
    f_re = (nr * a_re + ni * a_im) / den
    f_im = (ni * a_re - nr * a_im) / den
    lb_re_ref[...] = lb_re
    lb_im_ref[...] = lb_im
    bb_re = f_re * b_re_ref[...] - f_im * b_im_ref[...]
    bb_im = f_re * b_im_ref[...] + f_im * b_re_ref[...]
    for j in range(SCAN_BLOCKS):
        same = _same_group((LANES, SCAN_COLS), 0, _lane_tile(j) * LANES, j * SCAN_COLS)
        for bb, w_ref in ((bb_re, w_re_ref), (bb_im, w_im_ref)):
            rows = jnp.concatenate([bb[:, j * SCAN_COLS:(j + 1) * SCAN_COLS]] * (LANES // SSM_GROUP), axis=0)
            w_ref[j] = jnp.where(same, rows, 0.0).astype(BF16)


def _disc_call(a_re, a_im, log_dt, b_re_t, b_im_t):
    row = jax.ShapeDtypeStruct((1, N_STATES), F32)
    slabs = jax.ShapeDtypeStruct((SCAN_BLOCKS, LANES, SCAN_COLS), BF16)
    return pl.pallas_call(_disc_body, out_shape=[row, row, slabs, slabs], name="disc")(
        a_re, a_im, log_dt, b_re_t, b_im_t)


def _cslab_body(c_re_ref, c_im_ref, ct_re_ref, ct_im_ref):
    for j in range(SCAN_BLOCKS):
        same = _same_group((LANES, SCAN_COLS), 0, _lane_tile(j) * LANES, j * SCAN_COLS)
        for c_ref, ct_ref, sign in ((c_re_ref, ct_re_ref, 1.0), (c_im_ref, ct_im_ref, -1.0)):
            c = c_ref[_lane_tile(j) * LANES:(_lane_tile(j) + 1) * LANES, :] * sign
            cols = jnp.concatenate([c] * (SCAN_COLS // SSM_STATE), axis=1)
            ct_ref[j] = jnp.where(same, cols, 0.0).astype(BF16)


def _cslab_call(c_re, c_im):
    slabs = jax.ShapeDtypeStruct((SCAN_BLOCKS, LANES, SCAN_COLS), BF16)
    return pl.pallas_call(_cslab_body, out_shape=[slabs, slabs], name="cslab")(c_re, c_im)


SCAN_BLOCKS = N_STATES // SCAN_COLS
BLOCKS_PER_LANE_TILE = LANES * SSM_STATE // (SSM_GROUP * SCAN_COLS)


def _lane_tile(j):
    return j // BLOCKS_PER_LANE_TILE
Y_COLS = LANES


def _s5_body(*refs, reverse, steps, finish):
    if finish:
        (u_ref, init_ref, lb_re_ref, lb_im_ref, w_re_ref, w_im_ref, c_re_ref, c_im_ref,
         yf_ref, d_ref, w_glu_ref, o_ref, fin_ref, bu_re, bu_im, h_re, h_im) = refs
    else:
        (u_ref, init_ref, lb_re_ref, lb_im_ref, w_re_ref, w_im_ref, c_re_ref, c_im_ref,
         o_ref, fin_ref, bu_re, bu_im, h_re, h_im) = refs
    nb = init_ref.shape[0]

    @pl.when(pl.program_id(0) == 0)
    def _():
        h_re[...] = init_ref[:, :N_STATES]
        h_im[...] = init_ref[:, N_STATES:]

    def scan_order(ref):
        if not reverse:
            return ref[...]
        return jnp.concatenate(
            [ref[(steps - 1 - s) * nb:(steps - s) * nb, :] for s in range(steps)], axis=0)

    u = scan_order(u_ref)
    ub = u.astype(BF16)
    y_parts = []
    for j in range(SCAN_BLOCKS):
        cols = slice(j * SCAN_COLS, (j + 1) * SCAN_COLS)
        us = ub[:, _lane_tile(j) * LANES:(_lane_tile(j) + 1) * LANES]
        bu_re[j] = _dot(us, w_re_ref[j])
        bu_im[j] = _dot(us, w_im_ref[j])
        lr = jnp.broadcast_to(lb_re_ref[:, cols], (nb, SCAN_COLS))
        li = jnp.broadcast_to(lb_im_ref[:, cols], (nb, SCAN_COLS))
        hr, hi = h_re[:, cols], h_im[:, cols]
        for s in range(steps):
            rows = slice(s * nb, (s + 1) * nb)
            hr, hi = (lr * hr - li * hi + bu_re[j, rows, :],
                      lr * hi + li * hr + bu_im[j, rows, :])
            bu_re[j, rows, :] = hr
            bu_im[j, rows, :] = hi
        h_re[:, cols] = hr
        h_im[:, cols] = hi
        part = (_dot_nt(bu_re[j].astype(BF16), c_re_ref[j])
                + _dot_nt(bu_im[j].astype(BF16), c_im_ref[j]))
        if j % BLOCKS_PER_LANE_TILE == 0:
            y_parts.append(part)
        else:
            y_parts[-1] = y_parts[-1] + part
    y = jnp.concatenate(y_parts, axis=-1)
    fin_ref[:, :N_STATES] = h_re[...]
    fin_ref[:, N_STATES:] = h_im[...]
    if finish:
        y = y + scan_order(yf_ref) + d_ref[...] * u
        z = jax.nn.gelu(y)
        y = (z * jax.nn.sigmoid(_dot(z.astype(BF16), w_glu_ref[...]))).astype(BF16)
    if reverse:
        for s in range(steps):
            o_ref[(steps - 1 - s) * nb:(steps - s) * nb, :] = y[s * nb:(s + 1) * nb, :]
    else:
        o_ref[...] = y


def _s5_rider_body(*refs, n_in, s5_body, rider_prologue, rider_body):
    s5_in, rider_in = refs[:n_in], refs[n_in:n_in + 6]
    outs, scratch = refs[n_in + 6:n_in + 9], refs[n_in + 9:]
    rider_prologue(*rider_in, outs[2], scratch[4])
    s5_body(*s5_in, *outs[:2], *scratch[:4])
    rider_body(*rider_in, outs[2], scratch[4])


def _s5_call(u, init, lb_re, lb_im, w_re, w_im, c_re, c_im, finish_args, *, tile, reverse, rider=None):
    rows = u.shape[0]
    nb = init.shape[0]
    n_tiles = rows // tile
    finish = finish_args is not None
    row_map = (lambda i: (n_tiles - 1 - i, 0)) if reverse else (lambda i: (i, 0))
    row_spec = pl.BlockSpec((tile, SSM_WIDTH), row_map)
    in_specs = [row_spec, _const_spec((nb, 2 * N_STATES)),
                _const_spec((1, N_STATES)), _const_spec((1, N_STATES)),
                _const_spec((SCAN_BLOCKS, LANES, SCAN_COLS)), _const_spec((SCAN_BLOCKS, LANES, SCAN_COLS)),
                _const_spec((SCAN_BLOCKS, LANES, SCAN_COLS)), _const_spec((SCAN_BLOCKS, LANES, SCAN_COLS))]
    args = [u, init, lb_re, lb_im, w_re, w_im, c_re, c_im]
    if finish:
        in_specs += [row_spec, _const_spec((1, SSM_WIDTH)), _const_spec((SSM_WIDTH, SSM_WIDTH))]
        args += list(finish_args)
    body = functools.partial(_s5_body, reverse=reverse, steps=tile // nb, finish=finish)
    out_specs = [row_spec, pl.BlockSpec((nb, 2 * N_STATES), lambda i: (0, 0))]
    out_shape = [jax.ShapeDtypeStruct((rows, SSM_WIDTH), BF16 if finish else F32),
                 jax.ShapeDtypeStruct((nb, 2 * N_STATES), F32)]
    scratch = [pltpu.VMEM((SCAN_BLOCKS, tile, SCAN_COLS), F32),
               pltpu.VMEM((SCAN_BLOCKS, tile, SCAN_COLS), F32),
               pltpu.VMEM((nb, N_STATES), F32), pltpu.VMEM((nb, N_STATES), F32)]
    if rider is not None:
        assert rider["steps"] == n_tiles, "one rider unit per scan tile"
        body = functools.partial(_s5_rider_body, n_in=len(args), s5_body=body,
                                 rider_prologue=rider["prologue"], rider_body=rider["body"])
        in_specs = in_specs + rider["in_specs"]
        args = args + rider["args"]
        out_specs.append(rider["out_spec"])
        out_shape.append(rider["out_shape"])
        scratch.append(rider["scratch"])
    return pl.pallas_call(
        body,
        grid=(n_tiles,),
        in_specs=in_specs,
        out_specs=out_specs,
        out_shape=out_shape,
        scratch_shapes=scratch,
        compiler_params=_params(1),
        name="s5_bwd" if reverse else "s5_fwd",
    )(*args)


NA_Q = NA_Q_ROWS * GRID_W
NA_K = NA_K_ROWS * GRID_W
NA_KEYS = NA_K + NA_META_PAD
NA_RING = LANES


def _block_key_row(blk, rows):
    return min(max(blk * NA_Q_ROWS - WIN_H // 2, 0), rows - NA_K_ROWS)


def _na_assemble_bias(ring_ref, bias, n_blocks, rows):
    kh = min(WIN_H, rows)
    masked = jnp.full((GRID_W, GRID_W), MASK_VALUE, F32)
    meta = jnp.where(lax.broadcasted_iota(jnp.int32, (GRID_W, GRID_W), 1) < N_META, 0.0, MASK_VALUE)
    cq = lax.broadcasted_iota(jnp.int32, (GRID_W, GRID_W), 0)
    ck = lax.broadcasted_iota(jnp.int32, (GRID_W, GRID_W), 1)
    col_start = jnp.clip(cq - WIN_W // 2, 0, GRID_W - WIN_W)
    col_valid = (ck >= col_start) & (ck < col_start + WIN_W)
    col_tiles = {}

    def col_tile(head, off):
        if (head, off) not in col_tiles:
            ring = jnp.broadcast_to(ring_ref[head, off:off + 1, :], (GRID_W, NA_RING))
            toeplitz = pltpu.roll(ring, 0, 1, stride=1, stride_axis=0)[:, :GRID_W]
            col_tiles[head, off] = jnp.where(col_valid, toeplitz, MASK_VALUE)
        return col_tiles[head, off]

    for kind, blk in enumerate((0, 1, n_blocks - 1)):
        for head in range(2):
            for qr in range(NA_Q_ROWS):
                r = blk * NA_Q_ROWS + qr
                row_start = min(max(r - kh // 2, 0), rows - kh)
                tiles = []
                for slot in range(NA_K_ROWS):
                    kr = _block_key_row(blk, rows) + slot
                    valid = row_start <= kr < row_start + kh
                    tiles.append(col_tile(head, kr - r + WIN_H - 1) if valid else masked)
                tiles.append(meta)
                for c in range(0, len(tiles), 2):
                    bias[kind, head, qr * GRID_W:(qr + 1) * GRID_W, c * GRID_W:(c + 2) * GRID_W] = (
                        jnp.concatenate(tiles[c:c + 2], axis=1))


def _na_prologue(q_ref, k_ref, v_ref, km_ref, vm_ref, t_ref, o_ref, bias, *, n_blocks, rows, batch):
    @pl.when(pl.program_id(0) % batch == 0)
    def _():
        _na_assemble_bias(t_ref, bias, n_blocks, rows)


def _na_body(q_ref, k_ref, v_ref, km_ref, vm_ref, t_ref, o_ref, bias, *, n_blocks, rows, batch):
    first_head = lax.broadcasted_iota(jnp.int32, (1, LANES), 1) < NA_HEAD_DIM
    km = km_ref[...]
    vm = vm_ref[...]
    for i in range(n_blocks):
        k0 = _block_key_row(i, rows) * GRID_W
        q0 = i * NA_Q
        kind = 0 if i == 0 else 2 if i == n_blocks - 1 else 1
        q = q_ref[q0:q0 + NA_Q, :]
        zero = jnp.zeros_like(q)
        qs = jnp.concatenate([jnp.where(first_head, q, zero), jnp.where(first_head, zero, q)], axis=0)
        keys = jnp.concatenate([k_ref[k0:k0 + NA_K, :], km], axis=0)
        vals = jnp.concatenate([v_ref[k0:k0 + NA_K, :], vm], axis=0)
        s = _dot_nt(qs, keys) + bias[kind].reshape(2 * NA_Q, NA_KEYS)
        m = jnp.max(s, axis=-1, keepdims=True)
        p = jnp.exp(s - m)
        l = jnp.sum(p, axis=-1, keepdims=True)
        o = _dot(p.astype(BF16), vals) / l
        o_ref[q0:q0 + NA_Q, :] = jnp.where(first_head, o[:NA_Q], o[NA_Q:]).astype(BF16)


def _na_rider(q, k, v, km, vm, col_bias, *, first_pair, n_pairs, batch, seq):
    rows = seq // GRID_W
    assert NA_META_PAD == GRID_W and (NA_K_ROWS + 1) % 2 == 0
    pair = lambda i: first_pair + i // batch
    seq_spec = pl.BlockSpec((seq, LANES), lambda i: (i % batch, pair(i)))
    meta_spec = pl.BlockSpec((NA_META_PAD, LANES), lambda i: (0, pair(i)))
    table_spec = pl.BlockSpec((2, 2 * WIN_H - 1, NA_RING), lambda i: (pair(i), 0, 0))
    return dict(
        steps=n_pairs * batch,
        prologue=functools.partial(_na_prologue, n_blocks=rows // NA_Q_ROWS, rows=rows, batch=batch),
        body=functools.partial(_na_body, n_blocks=rows // NA_Q_ROWS, rows=rows, batch=batch),
        args=[q, k, v, km, vm, col_bias],
        in_specs=[seq_spec, seq_spec, seq_spec, meta_spec, meta_spec, table_spec],
        out_spec=pl.BlockSpec((seq, LANES), lambda i: (i % batch, i // batch)),
        out_shape=jax.ShapeDtypeStruct((batch * seq, n_pairs * LANES), BF16),
        scratch=pltpu.VMEM((3, 2, NA_Q, NA_KEYS), F32))


def _na_col_ring(rpb):
    n_head, n_off, _ = rpb.shape
    ring = jnp.zeros((n_head, n_off, NA_RING), F32)
    ring = ring.at[..., :WIN_W].set(rpb[..., WIN_W - 1:].astype(F32))
    return ring.at[..., NA_RING - (WIN_W - 1):].set(rpb[..., :WIN_W - 1].astype(F32))


def _merge_body(h_ref, z_ref, a_lo_ref, a_hi_ref, g_ref, w_gate_ref, w_ssm_ref, w_na_ref, w_out_ref, o_ref):
    h = h_ref[...]
    hn = _rms_norm(h, g_ref[...]).astype(BF16)
    gates = _dot(hn, w_gate_ref[...])
    y_ssm = _dot(z_ref[...], w_ssm_ref[...])
    y_na = (_dot(a_lo_ref[...], w_na_ref[:NA_WIDTH // 2, :])
            + _dot(a_hi_ref[...], w_na_ref[NA_WIDTH // 2:, :]))
    merged = (jax.nn.sigmoid(gates[:, :D_MODEL]) * y_ssm
              + jax.nn.sigmoid(gates[:, D_MODEL:]) * y_na)
    o_ref[...] = h + _dot(merged.astype(BF16), w_out_ref[...])


def _merge_call(h1, z, a_lo, a_hi, g, w_gate, w_ssm, w_na, w_out):
    rows = h1.shape[0]
    row_spec = lambda width: pl.BlockSpec((WIDE_ROW_TILE, width), lambda i: (i, 0))
    return pl.pallas_call(
        _merge_body,
        grid=(rows // WIDE_ROW_TILE,),
        in_specs=[row_spec(D_MODEL), row_spec(SSM_WIDTH), row_spec(NA_WIDTH // 2), row_spec(NA_WIDTH // 2),
                  _const_spec((1, D_MODEL)), _const_spec((D_MODEL, 2 * D_MODEL)),
                  _const_spec((SSM_WIDTH, D_MODEL)), _const_spec((NA_WIDTH, D_MODEL)),
                  _const_spec((D_MODEL, D_MODEL))],
        out_specs=row_spec(D_MODEL),
        out_shape=jax.ShapeDtypeStruct((rows, D_MODEL), F32),
        compiler_params=_params(1),
        name="merge",
    )(h1, z, a_lo, a_hi, g, w_gate, w_ssm, w_na, w_out)


def kernel(x, meta_tokens, norm_ffn1, w_ffn1_in, w_ffn1_out, norm_mix, w_in, ssm_a_re_fwd, ssm_a_im_fwd, ssm_log_dt_fwd, ssm_a_re_bwd, ssm_a_im_bwd, ssm_log_dt_bwd, ssm_b_re, ssm_b_im, ssm_c_re, ssm_c_im, ssm_d, w_glu, na_rpb, w_branch_ssm, w_branch_na, w_out, norm_ffn2, w_ffn2_in, w_ffn2_out, norm_final):
    batch, seq, d_model = x.shape
    assert d_model == D_MODEL and norm_ffn1.shape[0] == 1, "single-layer block of width 1024"
    assert seq % ROW_TILE == 0 and seq % (GRID_W * NA_Q_ROWS) == 0 and batch % 8 == 0
    rows = seq // GRID_W
    assert rows >= NA_K_ROWS + NA_Q_ROWS

    row = lambda a: a.reshape(1, -1).astype(F32)
    g1, gm, g2, gf = row(norm_ffn1[0]), row(norm_mix[0]), row(norm_ffn2[0]), row(norm_final)
    w1_in, w1_out = w_ffn1_in[0].astype(BF16), w_ffn1_out[0].astype(BF16)
    w_uqkv = w_in[0][:, :UQKV_WIDTH].astype(BF16)
    whole = lambda a: (a[0].astype(F32), 0, a.shape[-1])
    later_weights = [whole(w_ffn2_in), whole(w_ffn2_out), (w_in[0].astype(F32), 1, UQKV_WIDTH),
                     whole(w_branch_ssm), whole(w_branch_na), whole(w_out), whole(w_glu)]
    assert w_in.shape[-1] == 2 * UQKV_WIDTH, "the two gates are the second column block of w_in"

    n_rows = batch * seq
    tok_spec = pl.BlockSpec((ROW_TILE, D_MODEL), lambda i: (i, 0))
    (h1, u, q, k, v, u_meta, _, k_meta, v_meta,
     w2_in, w2_out, w_gate, w_ssm, w_na, w_mix_out, w_glu_bf) = _ffn_proj_call(
        x.reshape(n_rows, D_MODEL), meta_tokens.astype(F32), g1, w1_in, w1_out, gm, w_uqkv,
        later_weights)
    u = u.reshape(batch, seq, SSM_WIDTH).transpose(1, 0, 2).reshape(n_rows, SSM_WIDTH)

    states = lambda a: a[0].reshape(1, N_STATES).astype(F32)
    per_state = lambda a: jnp.repeat(a[0].astype(F32), SSM_STATE).reshape(1, N_STATES)
    b_re_t = ssm_b_re[0].reshape(N_STATES, SSM_GROUP).T.astype(F32)
    b_im_t = ssm_b_im[0].reshape(N_STATES, SSM_GROUP).T.astype(F32)
    lbf_re, lbf_im, wf_re, wf_im = _disc_call(
        states(ssm_a_re_fwd), states(ssm_a_im_fwd), per_state(ssm_log_dt_fwd), b_re_t, b_im_t)
    lbb_re, lbb_im, wb_re, wb_im = _disc_call(
        states(ssm_a_re_bwd), states(ssm_a_im_bwd), per_state(ssm_log_dt_bwd), b_re_t, b_im_t)
    c_re, c_im = _cslab_call(ssm_c_re[0].reshape(SSM_WIDTH, SSM_STATE).astype(F32),
                             ssm_c_im[0].reshape(SSM_WIDTH, SSM_STATE).astype(F32))

    zero_state = jnp.zeros((batch, 2 * N_STATES), F32)
    u_meta_rows = jnp.repeat(u_meta, batch, axis=0)
    _, state_meta = _s5_call(u_meta_rows, zero_state, lbf_re, lbf_im, wf_re, wf_im, c_re, c_im,
                             None, tile=N_META * batch, reverse=False)
    pad_meta = lambda a: jnp.pad(a, ((0, NA_META_PAD - N_META), (0, 0)))
    half = NA_HEADS // 4
    riders = [_na_rider(q, k, v, pad_meta(k_meta), pad_meta(v_meta), _na_col_ring(na_rpb[0]),
                        first_pair=p0, n_pairs=half, batch=batch, seq=seq) for p0 in (0, half)]
    y_fwd, _, att_lo = _s5_call(u, state_meta, lbf_re, lbf_im, wf_re, wf_im, c_re, c_im,
                                None, tile=SCAN_TILE, reverse=False, rider=riders[0])
    z, _, att_hi = _s5_call(u, zero_state, lbb_re, lbb_im, wb_re, wb_im, c_re, c_im,
                            (y_fwd, row(ssm_d[0]), w_glu_bf), tile=SCAN_TILE,
                            reverse=True, rider=riders[1])

    z = z.reshape(seq, batch, SSM_WIDTH).transpose(1, 0, 2).reshape(n_rows, SSM_WIDTH)
    h2 = _merge_call(h1, z, att_lo, att_hi, gm, w_gate,
                     w_ssm, w_na, w_mix_out)

    out = _ffn_call(h2, g2, w2_in, w2_out, gf, grid=(n_rows // ROW_TILE,), x_spec=tok_spec,
                    o_spec=tok_spec, out_shape=(n_rows, D_MODEL), final_norm=True)
    return out.reshape(batch, seq, D_MODEL)
```

```python
import functools
import math

import jax
import jax.numpy as jnp
from jax import lax
from jax.experimental import pallas as pl
from jax.experimental.pallas import tpu as pltpu

D_MODEL = 1024
N_META = 16
GRID_W = 64
D_FF = 2816
SSM_WIDTH = 512
SSM_GROUP = 16
SSM_GROUPS = 32
SSM_STATE = 64
N_STATES = SSM_GROUPS * SSM_STATE
NA_HEAD_DIM = 64
NA_WIDTH = 512
NA_HEADS = 8
WIN_H = 8
WIN_W = 16
UQKV_WIDTH = SSM_WIDTH + 3 * NA_WIDTH
RMS_EPS = 1e-6
A_RE_MAX = -1e-4
MASK_VALUE = -1e30

LANES = 128
BF16_ROWS = 16
VMEM_LIMIT = 56 * 1024 * 1024

ROW_TILE = 512
WIDE_ROW_TILE = 1024
SCAN_COLS = 256
SCAN_TILE = 1024
NA_Q_ROWS = 4
NA_K_ROWS = NA_Q_ROWS + WIN_H - 1
NA_META_PAD = GRID_W

BF16 = jnp.bfloat16
F32 = jnp.float32


def _const_spec(shape):
    zeros = (0,) * len(shape)
    return pl.BlockSpec(shape, lambda *_: zeros, pipeline_mode=pl.Buffered(1))


def _params(n_grid_dims):
    return pltpu.CompilerParams(
        dimension_semantics=("arbitrary",) * n_grid_dims,
        vmem_limit_bytes=VMEM_LIMIT)


def _rms_scale(x):
    return lax.rsqrt(jnp.mean(x * x, axis=-1, keepdims=True) + RMS_EPS)


def _rms_norm(x, g):
    return x * _rms_scale(x) * g


def _dot(a, b):
    return jnp.dot(a, b, preferred_element_type=F32)


def _dot_nt(a, b):
    return lax.dot_general(a, b, (((1,), (1,)), ((), ())), preferred_element_type=F32)


def _ffn_rows(x, g_ref, w_in_ref, w_out_ref):
    r = _rms_scale(x)
    xg = (x * g_ref[...]).astype(BF16)
    gate = _dot(xg, w_in_ref[:, :D_FF]) * r
    up = _dot(xg, w_in_ref[:, D_FF:]) * r
    act = (jax.nn.silu(gate) * up).astype(BF16)
    return x + 0.5 * _dot(act, w_out_ref[...])


def _ffn_body(x_ref, g_ref, w_in_ref, w_out_ref, gf_ref, o_ref, *, final_norm):
    y = _ffn_rows(x_ref[...], g_ref, w_in_ref, w_out_ref)
    if final_norm:
        y = _rms_norm(y, gf_ref[...])
    o_ref[...] = y


def _proj_rows(h, g_ref, w_ref, u_ref, q_ref, k_ref, v_ref):
    p = _dot((h * g_ref[...]).astype(BF16), w_ref[...]) * _rms_scale(h)
    u_ref[...] = p[:, :SSM_WIDTH]
    q_ref[...] = (p[:, SSM_WIDTH:SSM_WIDTH + NA_WIDTH] * (NA_HEAD_DIM ** -0.5)).astype(BF16)
    k_ref[...] = p[:, SSM_WIDTH + NA_WIDTH:SSM_WIDTH + 2 * NA_WIDTH].astype(BF16)
    v_ref[...] = p[:, SSM_WIDTH + 2 * NA_WIDTH:].astype(BF16)


N_FFN_PROJ_IN = 7
N_FFN_PROJ_OUT = 9


def _ffn_proj_body(*refs):
    x_ref, xm_ref, g_ref, w_in_ref, w_out_ref, gp_ref, wp_ref = refs[:N_FFN_PROJ_IN]
    n_cast = (len(refs) - N_FFN_PROJ_IN - N_FFN_PROJ_OUT) // 2
    cast_in = refs[N_FFN_PROJ_IN:N_FFN_PROJ_IN + n_cast]
    outs = refs[N_FFN_PROJ_IN + n_cast:]
    h_ref, u_ref, q_ref, k_ref, v_ref, um_ref, qm_ref, km_ref, vm_ref = outs[:N_FFN_PROJ_OUT]
    cast_out = outs[N_FFN_PROJ_OUT:]

    @pl.when(pl.program_id(0) == 0)
    def _():
        hm = _ffn_rows(xm_ref[...], g_ref, w_in_ref, w_out_ref)
        _proj_rows(hm, gp_ref, wp_ref, um_ref, qm_ref, km_ref, vm_ref)

    h = _ffn_rows(x_ref[...], g_ref, w_in_ref, w_out_ref)
    h_ref[...] = h
    _proj_rows(h, gp_ref, wp_ref, u_ref, q_ref, k_ref, v_ref)
    for src, dst in zip(cast_in, cast_out):
        dst[...] = src[...].astype(BF16)


def _ffn_proj_call(x, x_meta, g, w_in, w_out, g_proj, w_proj, to_cast):
    rows = x.shape[0]
    steps = rows // ROW_TILE
    row_spec = lambda width: pl.BlockSpec((ROW_TILE, width), lambda i: (i, 0))
    meta_spec = lambda width: pl.BlockSpec((N_META, width), lambda i: (0, 0))
    wide = lambda n, dt: jax.ShapeDtypeStruct((n, SSM_WIDTH), dt)

    def cast_specs(a, col_block, n_cols):
        n = a.shape[0]
        chunk = min(c for c in range(BF16_ROWS, n + 1, BF16_ROWS) if n % c == 0 and n // c <= steps)
        last = n // chunk - 1
        return (pl.BlockSpec((chunk, n_cols), lambda i: (jnp.minimum(i, last), col_block)),
                pl.BlockSpec((chunk, n_cols), lambda i: (jnp.minimum(i, last), 0)))

    cast_in_specs, cast_out_specs = zip(*[cast_specs(*c) for c in to_cast])
    cast_shapes = [jax.ShapeDtypeStruct((a.shape[0], n_cols), BF16) for a, _, n_cols in to_cast]
    return pl.pallas_call(
        _ffn_proj_body,
        grid=(steps,),
        in_specs=[row_spec(D_MODEL), meta_spec(D_MODEL), _const_spec((1, D_MODEL)),
                  _const_spec((D_MODEL, 2 * D_FF)), _const_spec((D_FF, D_MODEL)),
                  _const_spec((1, D_MODEL)), _const_spec((D_MODEL, UQKV_WIDTH))] + list(cast_in_specs),
        out_specs=[row_spec(D_MODEL)] + [row_spec(SSM_WIDTH)] * 4 + [meta_spec(SSM_WIDTH)] * 4
        + list(cast_out_specs),
        out_shape=[jax.ShapeDtypeStruct((rows, D_MODEL), F32), wide(rows, F32)] + [wide(rows, BF16)] * 3
        + [wide(N_META, F32)] + [wide(N_META, BF16)] * 3 + cast_shapes,
        compiler_params=_params(1),
        name="ffn_proj",
    )(x, x_meta, g, w_in, w_out, g_proj, w_proj, *[a for a, _, _ in to_cast])


def _ffn_call(x, g, w_in, w_out, gf, *, grid, x_spec, o_spec, out_shape, final_norm):
    return pl.pallas_call(
        functools.partial(_ffn_body, final_norm=final_norm),
        grid=grid,
        in_specs=[x_spec, _const_spec((1, D_MODEL)), _const_spec((D_MODEL, 2 * D_FF)),
                  _const_spec((D_FF, D_MODEL)), _const_spec((1, D_MODEL))],
        out_specs=o_spec,
        out_shape=jax.ShapeDtypeStruct(out_shape, F32),
        compiler_params=_params(len(grid)),
        name="ffn",
    )(x, g, w_in, w_out, gf)


def _same_group(shape, ch_axis, ch0, state0):
    ch = lax.broadcasted_iota(jnp.int32, shape, ch_axis) + ch0
    st = lax.broadcasted_iota(jnp.int32, shape, 1 - ch_axis) + state0
    return (ch >> int(math.log2(SSM_GROUP))) == (st >> int(math.log2(SSM_STATE)))


def _disc_body(a_re_ref, a_im_ref, log_dt_ref, b_re_ref, b_im_ref,
               lb_re_ref, lb_im_ref, w_re_ref, w_im_ref):
    dt = jnp.exp(log_dt_ref[...])
    a_re = jnp.minimum(a_re_ref[...], A_RE_MAX)
    a_im = a_im_ref[...]
    mag = jnp.exp(a_re * dt)
    lb_re = mag * jnp.cos(a_im * dt)
    lb_im = mag * jnp.sin(a_im * dt)
    den = a_re * a_re + a_im * a_im
    nr = lb_re - 1.0
    ni = lb_im
    f_re = (nr * a_re + ni * a_im) / den
    f_im = (ni * a_re - nr * a_im) / den
    lb_re_ref[...] = lb_re
    lb_im_ref[...] = lb_im
    bb_re = f_re * b_re_ref[...] - f_im * b_im_ref[...]
    bb_im = f_re * b_im_ref[...] + f_im * b_re_ref[...]
    for j in range(SCAN_BLOCKS):
        same = _same_group((LANES, SCAN_COLS), 0, _lane_tile(j) * LANES, j * SCAN_COLS)
        for bb, w_ref in ((bb_re, w_re_ref), (bb_im, w_im_ref)):
            rows = jnp.concatenate([bb[:, j * SCAN_COLS:(j + 1) * SCAN_COLS]] * (LANES // SSM_GROUP), axis=0)
            w_ref[j] = jnp.where(same, rows, 0.0).astype(BF16)


def _disc_call(a_re, a_im, log_dt, b_re_t, b_im_t):
    row = jax.ShapeDtypeStruct((1, N_STATES), F32)
    slabs = jax.ShapeDtypeStruct((SCAN_BLOCKS, LANES, SCAN_COLS), BF16)
    return pl.pallas_call(_disc_body, out_shape=[row, row, slabs, slabs], name="disc")(
        a_re, a_im, log_dt, b_re_t, b_im_t)


def _cslab_body(c_re_ref, c_im_ref, ct_re_ref, ct_im_ref):
    for j in range(SCAN_BLOCKS):
        same = _same_group((LANES, SCAN_COLS), 0, _lane_tile(j) * LANES, j * SCAN_COLS)
        for c_ref, ct_ref, sign in ((c_re_ref, ct_re_ref, 1.0), (c_im_ref, ct_im_ref, -1.0)):
            c = c_ref[_lane_tile(j) * LANES:(_lane_tile(j) + 1) * LANES, :] * sign
            cols = jnp.concatenate([c] * (SCAN_COLS // SSM_STATE), axis=1)
            ct_ref[j] = jnp.where(same, cols, 0.0).astype(BF16)


def _cslab_call(c_re, c_im):
    slabs = jax.ShapeDtypeStruct((SCAN_BLOCKS, LANES, SCAN_COLS), BF16)
    return pl.pallas_call(_cslab_body, out_shape=[slabs, slabs], name="cslab")(c_re, c_im)


SCAN_BLOCKS = N_STATES // SCAN_COLS
BLOCKS_PER_LANE_TILE = LANES * SSM_STATE // (SSM_GROUP * SCAN_COLS)


def _lane_tile(j):
    return j // BLOCKS_PER_LANE_TILE


def _s5_body(*refs, reverse, steps, finish):
    if finish:
        (u_ref, init_ref, lb_re_ref, lb_im_ref, w_re_ref, w_im_ref, c_re_ref, c_im_ref,
         yf_ref, d_ref, w_glu_ref, o_ref, fin_ref, bu_re, bu_im, h_re, h_im) = refs
    else:
        (u_ref, init_ref, lb_re_ref, lb_im_ref, w_re_ref, w_im_ref, c_re_ref, c_im_ref,
         o_ref, fin_ref, bu_re, bu_im, h_re, h_im) = refs
    nb = init_ref.shape[0]

    @pl.when(pl.program_id(0) == 0)
    def _():
        h_re[...] = init_ref[:, :N_STATES]
        h_im[...] = init_ref[:, N_STATES:]

    def scan_order(ref):
        if not reverse:
            return ref[...]
        return jnp.concatenate(
            [ref[(steps - 1 - s) * nb:(steps - s) * nb, :] for s in range(steps)], axis=0)

    u = scan_order(u_ref)
    ub = u.astype(BF16)
    y_parts = []
    for j in range(SCAN_BLOCKS):
        cols = slice(j * SCAN_COLS, (j + 1) * SCAN_COLS)
        us = ub[:, _lane_tile(j) * LANES:(_lane_tile(j) + 1) * LANES]
        bu_re[j] = _dot(us, w_re_ref[j])
        bu_im[j] = _dot(us, w_im_ref[j])
        lr = jnp.broadcast_to(lb_re_ref[:, cols], (nb, SCAN_COLS))
        li = jnp.broadcast_to(lb_im_ref[:, cols], (nb, SCAN_COLS))
        hr, hi = h_re[:, cols], h_im[:, cols]
        for s in range(steps):
            rows = slice(s * nb, (s + 1) * nb)
            hr, hi = (lr * hr - li * hi + bu_re[j, rows, :],
                      lr * hi + li * hr + bu_im[j, rows, :])
            bu_re[j, rows, :] = hr
            bu_im[j, rows, :] = hi
        h_re[:, cols] = hr
        h_im[:, cols] = hi
        part = (_dot_nt(bu_re[j].astype(BF16), c_re_ref[j])
                + _dot_nt(bu_im[j].astype(BF16), c_im_ref[j]))
        if j % BLOCKS_PER_LANE_TILE == 0:
            y_parts.append(part)
        else:
            y_parts[-1] = y_parts[-1] + part
    y = jnp.concatenate(y_parts, axis=-1)
    fin_ref[:, :N_STATES] = h_re[...]
    fin_ref[:, N_STATES:] = h_im[...]
    if finish:
        y = y + scan_order(yf_ref) + d_ref[...] * u
        z = jax.nn.gelu(y)
        y = (z * jax.nn.sigmoid(_dot(z.astype(BF16), w_glu_ref[...]))).astype(BF16)
    if reverse:
        for s in range(steps):
            o_ref[(steps - 1 - s) * nb:(steps - s) * nb, :] = y[s * nb:(s + 1) * nb, :]
    else:
        o_ref[...] = y


def _s5_rider_body(*refs, n_in, s5_body, rider_prologue, rider_body):
    s5_in, rider_in = refs[:n_in], refs[n_in:n_in + 6]
    outs, scratch = refs[n_in + 6:n_in + 9], refs[n_in + 9:]
    rider_prologue(*rider_in, outs[2], scratch[4])
    s5_body(*s5_in, *outs[:2], *scratch[:4])
    rider_body(*rider_in, outs[2], scratch[4])


def _s5_call(u, init, lb_re, lb_im, w_re, w_im, c_re, c_im, finish_args, *, tile, reverse, rider=None):
    rows = u.shape[0]
    nb = init.shape[0]
    n_tiles = rows // tile
    finish = finish_args is not None
    row_map = (lambda i: (n_tiles - 1 - i, 0)) if reverse else (lambda i: (i, 0))
    row_spec = pl.BlockSpec((tile, SSM_WIDTH), row_map)
    in_specs = [row_spec, _const_spec((nb, 2 * N_STATES)),
                _const_spec((1, N_STATES)), _const_spec((1, N_STATES)),
                _const_spec((SCAN_BLOCKS, LANES, SCAN_COLS)), _const_spec((SCAN_BLOCKS, LANES, SCAN_COLS)),
                _const_spec((SCAN_BLOCKS, LANES, SCAN_COLS)), _const_spec((SCAN_BLOCKS, LANES, SCAN_COLS))]
    args = [u, init, lb_re, lb_im, w_re, w_im, c_re, c_im]
    if finish:
        in_specs += [row_spec, _const_spec((1, SSM_WIDTH)), _const_spec((SSM_WIDTH, SSM_WIDTH))]
        args += list(finish_args)
    body = functools.partial(_s5_body, reverse=reverse, steps=tile // nb, finish=finish)
    out_specs = [row_spec, pl.BlockSpec((nb, 2 * N_STATES), lambda i: (0, 0))]
    out_shape = [jax.ShapeDtypeStruct((rows, SSM_WIDTH), BF16 if finish else F32),
                 jax.ShapeDtypeStruct((nb, 2 * N_STATES), F32)]
    scratch = [pltpu.VMEM((SCAN_BLOCKS, tile, SCAN_COLS), F32),
               pltpu.VMEM((SCAN_BLOCKS, tile, SCAN_COLS), F32),
               pltpu.VMEM((nb, N_STATES), F32), pltpu.VMEM((nb, N_STATES), F32)]
    if rider is not None:
        assert rider["steps"] == n_tiles, "one rider unit per scan tile"
        body = functools.partial(_s5_rider_body, n_in=len(args), s5_body=body,
                                 rider_prologue=rider["prologue"], rider_body=rider["body"])
        in_specs = in_specs + rider["in_specs"]
        args = args + rider["args"]
        out_specs.append(rider["out_spec"])
        out_shape.append(rider["out_shape"])
        scratch.append(rider["scratch"])
    return pl.pallas_call(
        body,
        grid=(n_tiles,),
        in_specs=in_specs,
        out_specs=out_specs,
        out_shape=out_shape,
        scratch_shapes=scratch,
        compiler_params=_params(1),
        name="s5_bwd" if reverse else "s5_fwd",
    )(*args)


NA_Q = NA_Q_ROWS * GRID_W
NA_K = NA_K_ROWS * GRID_W
NA_KEYS = NA_K + NA_META_PAD
NA_RING = LANES


def _block_key_row(blk, rows):
    return min(max(blk * NA_Q_ROWS - WIN_H // 2, 0), rows - NA_K_ROWS)


def _na_assemble_bias(ring_ref, bias, n_blocks, rows):
    kh = min(WIN_H, rows)
    masked = jnp.full((GRID_W, GRID_W), MASK_VALUE, F32)
    meta = jnp.where(lax.broadcasted_iota(jnp.int32, (GRID_W, GRID_W), 1) < N_META, 0.0, MASK_VALUE)
    cq = lax.broadcasted_iota(jnp.int32, (GRID_W, GRID_W), 0)
    ck = lax.broadcasted_iota(jnp.int32, (GRID_W, GRID_W), 1)
    col_start = jnp.clip(cq - WIN_W // 2, 0, GRID_W - WIN_W)
    col_valid = (ck >= col_start) & (ck < col_start + WIN_W)
    col_tiles = {}

    def col_tile(head, off):
        if (head, off) not in col_tiles:
            ring = jnp.broadcast_to(ring_ref[head, off:off + 1, :], (GRID_W, NA_RING))
            toeplitz = pltpu.roll(ring, 0, 1, stride=1, stride_axis=0)[:, :GRID_W]
            col_tiles[head, off] = jnp.where(col_valid, toeplitz, MASK_VALUE)
        return col_tiles[head, off]

    for kind, blk in enumerate((0, 1, n_blocks - 1)):
        for head in range(2):
            for qr in range(NA_Q_ROWS):
                r = blk * NA_Q_ROWS + qr
                row_start = min(max(r - kh // 2, 0), rows - kh)
                tiles = []
                for slot in range(NA_K_ROWS):
                    kr = _block_key_row(blk, rows) + slot
                    valid = row_start <= kr < row_start + kh
                    tiles.append(col_tile(head, kr - r + WIN_H - 1) if valid else masked)
                tiles.append(meta)
                for c in range(0, len(tiles), 2):
                    bias[kind, head, qr * GRID_W:(qr + 1) * GRID_W, c * GRID_W:(c + 2) * GRID_W] = (
                        jnp.concatenate(tiles[c:c + 2], axis=1))


def _na_prologue(q_ref, k_ref, v_ref, km_ref, vm_ref, t_ref, o_ref, bias, *, n_blocks, rows, batch):
    @pl.when(pl.program_id(0) % batch == 0)
    def _():
        _na_assemble_bias(t_ref, bias, n_blocks, rows)


def _na_body(q_ref, k_ref, v_ref, km_ref, vm_ref, t_ref, o_ref, bias, *, n_blocks, rows, batch):
    first_head = lax.broadcasted_iota(jnp.int32, (1, LANES), 1) < NA_HEAD_DIM
    km = km_ref[...]
    vm = vm_ref[...]
    for i in range(n_blocks):
        k0 = _block_key_row(i, rows) * GRID_W
        q0 = i * NA_Q
        kind = 0 if i == 0 else 2 if i == n_blocks - 1 else 1
        q = q_ref[q0:q0 + NA_Q, :]
        zero = jnp.zeros_like(q)
        qs = jnp.concatenate([jnp.where(first_head, q, zero), jnp.where(first_head, zero, q)], axis=0)
        keys = jnp.concatenate([k_ref[k0:k0 + NA_K, :], km], axis=0)
        vals = jnp.concatenate([v_ref[k0:k0 + NA_K, :], vm], axis=0)
        s = _dot_nt(qs, keys) + bias[kind].reshape(2 * NA_Q, NA_KEYS)
        m = jnp.max(s, axis=-1, keepdims=True)
        p = jnp.exp(s - m)
        l = jnp.sum(p, axis=-1, keepdims=True)
        o = _dot(p.astype(BF16), vals) / l
        o_ref[q0:q0 + NA_Q, :] = jnp.where(first_head, o[:NA_Q], o[NA_Q:]).astype(BF16)


def _na_rider(q, k, v, km, vm, col_bias, *, first_pair, n_pairs, batch, seq):
    rows = seq // GRID_W
    assert NA_META_PAD == GRID_W and (NA_K_ROWS + 1) % 2 == 0
    pair = lambda i: first_pair + i // batch
    seq_spec = pl.BlockSpec((seq, LANES), lambda i: (i % batch, pair(i)))
    meta_spec = pl.BlockSpec((NA_META_PAD, LANES), lambda i: (0, pair(i)))
    table_spec = pl.BlockSpec((2, 2 * WIN_H - 1, NA_RING), lambda i: (pair(i), 0, 0))
    return dict(
        steps=n_pairs * batch,
        prologue=functools.partial(_na_prologue, n_blocks=rows // NA_Q_ROWS, rows=rows, batch=batch),
        body=functools.partial(_na_body, n_blocks=rows // NA_Q_ROWS, rows=rows, batch=batch),
        args=[q, k, v, km, vm, col_bias],
        in_specs=[seq_spec, seq_spec, seq_spec, meta_spec, meta_spec, table_spec],
        out_spec=pl.BlockSpec((seq, LANES), lambda i: (i % batch, i // batch)),
        out_shape=jax.ShapeDtypeStruct((batch * seq, n_pairs * LANES), BF16),
        scratch=pltpu.VMEM((3, 2, NA_Q, NA_KEYS), F32))


def _na_col_ring(rpb):
    n_head, n_off, _ = rpb.shape
    ring = jnp.zeros((n_head, n_off, NA_RING), F32)
    ring = ring.at[..., :WIN_W].set(rpb[..., WIN_W - 1:].astype(F32))
    return ring.at[..., NA_RING - (WIN_W - 1):].set(rpb[..., :WIN_W - 1].astype(F32))


def _merge_body(h_ref, z_ref, a_lo_ref, a_hi_ref, g_ref, w_gate_ref, w_ssm_ref, w_na_ref, w_out_ref, o_ref):
    h = h_ref[...]
    hn = _rms_norm(h, g_ref[...]).astype(BF16)
    gates = _dot(hn, w_gate_ref[...])
    y_ssm = _dot(z_ref[...], w_ssm_ref[...])
    y_na = (_dot(a_lo_ref[...], w_na_ref[:NA_WIDTH // 2, :])
            + _dot(a_hi_ref[...], w_na_ref[NA_WIDTH // 2:, :]))
    merged = (jax.nn.sigmoid(gates[:, :D_MODEL]) * y_ssm
              + jax.nn.sigmoid(gates[:, D_MODEL:]) * y_na)
    o_ref[...] = h + _dot(merged.astype(BF16), w_out_ref[...])


def _merge_call(h1, z, a_lo, a_hi, g, w_gate, w_ssm, w_na, w_out):
    rows = h1.shape[0]
    row_spec = lambda width: pl.BlockSpec((WIDE_ROW_TILE, width), lambda i: (i, 0))
    return pl.pallas_call(
        _merge_body,
        grid=(rows // WIDE_ROW_TILE,),
        in_specs=[row_spec(D_MODEL), row_spec(SSM_WIDTH), row_spec(NA_WIDTH // 2), row_spec(NA_WIDTH // 2),
                  _const_spec((1, D_MODEL)), _const_spec((D_MODEL, 2 * D_MODEL)),
                  _const_spec((SSM_WIDTH, D_MODEL)), _const_spec((NA_WIDTH, D_MODEL)),
                  _const_spec((D_MODEL, D_MODEL))],
        out_specs=row_spec(D_MODEL),
        out_shape=jax.ShapeDtypeStruct((rows, D_MODEL), F32),
        compiler_params=_params(1),
        name="merge",
    )(h1, z, a_lo, a_hi, g, w_gate, w_ssm, w_na, w_out)


def kernel(x, meta_tokens, norm_ffn1, w_ffn1_in, w_ffn1_out, norm_mix, w_in, ssm_a_re_fwd, ssm_a_im_fwd, ssm_log_dt_fwd, ssm_a_re_bwd, ssm_a_im_bwd, ssm_log_dt_bwd, ssm_b_re, ssm_b_im, ssm_c_re, ssm_c_im, ssm_d, w_glu, na_rpb, w_branch_ssm, w_branch_na, w_out, norm_ffn2, w_ffn2_in, w_ffn2_out, norm_final):
    batch, seq, d_model = x.shape
    assert d_model == D_MODEL and norm_ffn1.shape[0] == 1, "single-layer block of width 1024"
    assert seq % ROW_TILE == 0 and seq % (GRID_W * NA_Q_ROWS) == 0 and batch % 8 == 0
    rows = seq // GRID_W
    assert rows >= NA_K_ROWS + NA_Q_ROWS

    row = lambda a: a.reshape(1, -1).astype(F32)
    g1, gm, g2, gf = row(norm_ffn1[0]), row(norm_mix[0]), row(norm_ffn2[0]), row(norm_final)
    w1_in, w1_out = w_ffn1_in[0].astype(BF16), w_ffn1_out[0].astype(BF16)
    w_uqkv = w_in[0][:, :UQKV_WIDTH].astype(BF16)
    whole = lambda a: (a[0].astype(F32), 0, a.shape[-1])
    later_weights = [whole(w_ffn2_in), whole(w_ffn2_out), (w_in[0].astype(F32), 1, UQKV_WIDTH),
                     whole(w_branch_ssm), whole(w_branch_na), whole(w_out), whole(w_glu)]
    assert w_in.shape[-1] == 2 * UQKV_WIDTH, "the two gates are the second column block of w_in"

    n_rows = batch * seq
    tok_spec = pl.BlockSpec((ROW_TILE, D_MODEL), lambda i: (i, 0))
    (h1, u, q, k, v, u_meta, _, k_meta, v_meta,
     w2_in, w2_out, w_gate, w_ssm, w_na, w_mix_out, w_glu_bf) = _ffn_proj_call(
        x.reshape(n_rows, D_MODEL), meta_tokens.astype(F32), g1, w1_in, w1_out, gm, w_uqkv,
        later_weights)
    u = u.reshape(batch, seq, SSM_WIDTH).transpose(1, 0, 2).reshape(n_rows, SSM_WIDTH)

    states = lambda a: a[0].reshape(1, N_STATES).astype(F32)
    per_state = lambda a: jnp.repeat(a[0].astype(F32), SSM_STATE).reshape(1, N_STATES)
    b_re_t = ssm_b_re[0].reshape(N_STATES, SSM_GROUP).T.astype(F32)
    b_im_t = ssm_b_im[0].reshape(N_STATES, SSM_GROUP).T.astype(F32)
    lbf_re, lbf_im, wf_re, wf_im = _disc_call(
        states(ssm_a_re_fwd), states(ssm_a_im_fwd), per_state(ssm_log_dt_fwd), b_re_t, b_im_t)
    lbb_re, lbb_im, wb_re, wb_im = _disc_call(
        states(ssm_a_re_bwd), states(ssm_a_im_bwd), per_state(ssm_log_dt_bwd), b_re_t, b_im_t)
    c_re, c_im = _cslab_call(ssm_c_re[0].reshape(SSM_WIDTH, SSM_STATE).astype(F32),
                             ssm_c_im[0].reshape(SSM_WIDTH, SSM_STATE).astype(F32))

    zero_state = jnp.zeros((batch, 2 * N_STATES), F32)
    u_meta_rows = jnp.repeat(u_meta, batch, axis=0)
    _, state_meta = _s5_call(u_meta_rows, zero_state, lbf_re, lbf_im, wf_re, wf_im, c_re, c_im,
                             None, tile=N_META * batch, reverse=False)
    pad_meta = lambda a: jnp.pad(a, ((0, NA_META_PAD - N_META), (0, 0)))
    half = NA_HEADS // 4
    riders = [_na_rider(q, k, v, pad_meta(k_meta), pad_meta(v_meta), _na_col_ring(na_rpb[0]),
                        first_pair=p0, n_pairs=half, batch=batch, seq=seq) for p0 in (0, half)]
    y_fwd, _, att_lo = _s5_call(u, state_meta, lbf_re, lbf_im, wf_re, wf_im, c_re, c_im,
                                None, tile=SCAN_TILE, reverse=False, rider=riders[0])
    z, _, att_hi = _s5_call(u, zero_state, lbb_re, lbb_im, wb_re, wb_im, c_re, c_im,
                            (y_fwd, row(ssm_d[0]), w_glu_bf), tile=SCAN_TILE,
                            reverse=True, rider=riders[1])

    z = z.reshape(seq, batch, SSM_WIDTH).transpose(1, 0, 2).reshape(n_rows, SSM_WIDTH)
    h2 = _merge_call(h1, z, att_lo, att_hi, gm, w_gate,
                     w_ssm, w_na, w_mix_out)

    out = _ffn_call(h2, g2, w2_in, w2_out, gf, grid=(n_rows // ROW_TILE,), x_spec=tok_spec,
                    o_spec=tok_spec, out_shape=(n_rows, D_MODEL), final_norm=True)
    return out.reshape(batch, seq, D_MODEL)
```

```python
import functools
import math

import jax
import jax.numpy as jnp
from jax import lax
from jax.experimental import pallas as pl
from jax.experimental.pallas import tpu as pltpu

D_MODEL = 1024
N_META = 16
GRID_W = 64
D_FF = 2816
SSM_WIDTH = 512
SSM_GROUP = 16
SSM_GROUPS = 32
SSM_STATE = 64
N_STATES = SSM_GROUPS * SSM_STATE
NA_HEAD_DIM = 64
NA_WIDTH = 512
NA_HEADS = 8
WIN_H = 8
WIN_W = 16
UQKV_WIDTH = SSM_WIDTH + 3 * NA_WIDTH
RMS_EPS = 1e-6
A_RE_MAX = -1e-4
MASK_VALUE = -1e30

LANES = 128
BF16_ROWS = 16
VMEM_LIMIT = 56 * 1024 * 1024

ROW_TILE = 512
WIDE_ROW_TILE = 1024
SCAN_COLS = 256
SCAN_TILE = 1024
NA_Q_ROWS = 4
NA_K_ROWS = NA_Q_ROWS + WIN_H - 1
NA_META_PAD = GRID_W

BF16 = jnp.bfloat16
F32 = jnp.float32


def _const_spec(shape):
    zeros = (0,) * len(shape)
    return pl.BlockSpec(shape, lambda *_: zeros, pipeline_mode=pl.Buffered(1))


def _params(n_grid_dims):
    return pltpu.CompilerParams(
        dimension_semantics=("arbitrary",) * n_grid_dims,
        vmem_limit_bytes=VMEM_LIMIT)


def _rms_scale(x):
    return lax.rsqrt(jnp.mean(x * x, axis=-1, keepdims=True) + RMS_EPS)


def _rms_norm(x, g):
    return x * _rms_scale(x) * g


def _dot(a, b):
    return jnp.dot(a, b, preferred_element_type=F32)


def _dot_nt(a, b):
    return lax.dot_general(a, b, (((1,), (1,)), ((), ())), preferred_element_type=F32)


def _ffn_rows(x, g_ref, w_in_ref, w_out_ref):
    r = _rms_scale(x)
    xg = (x * g_ref[...]).astype(BF16)
    gate = _dot(xg, w_in_ref[:, :D_FF]) * r
    up = _dot(xg, w_in_ref[:, D_FF:]) * r
    act = (jax.nn.silu(gate) * up).astype(BF16)
    return x + 0.5 * _dot(act, w_out_ref[...])


def _ffn_body(x_ref, g_ref, w_in_ref, w_out_ref, gf_ref, o_ref, *, final_norm):
    y = _ffn_rows(x_ref[...], g_ref, w_in_ref, w_out_ref)
    if final_norm:
        y = _rms_norm(y, gf_ref[...])
    o_ref[...] = y


def _proj_rows(h, g_ref, w_ref, u_ref, q_ref, k_ref, v_ref):
    p = _dot((h * g_ref[...]).astype(BF16), w_ref[...]) * _rms_scale(h)
    u_ref[...] = p[:, :SSM_WIDTH]
    q_ref[...] = (p[:, SSM_WIDTH:SSM_WIDTH + NA_WIDTH] * (NA_HEAD_DIM ** -0.5)).astype(BF16)
    k_ref[...] = p[:, SSM_WIDTH + NA_WIDTH:SSM_WIDTH + 2 * NA_WIDTH].astype(BF16)
    v_ref[...] = p[:, SSM_WIDTH + 2 * NA_WIDTH:].astype(BF16)


N_FFN_PROJ_IN = 7
N_FFN_PROJ_OUT = 9


def _ffn_proj_body(*refs):
    x_ref, xm_ref, g_ref, w_in_ref, w_out_ref, gp_ref, wp_ref = refs[:N_FFN_PROJ_IN]
    n_cast = (len(refs) - N_FFN_PROJ_IN - N_FFN_PROJ_OUT) // 2
    cast_in = refs[N_FFN_PROJ_IN:N_FFN_PROJ_IN + n_cast]
    outs = refs[N_FFN_PROJ_IN + n_cast:]
    h_ref, u_ref, q_ref, k_ref, v_ref, um_ref, qm_ref, km_ref, vm_ref = outs[:N_FFN_PROJ_OUT]
    cast_out = outs[N_FFN_PROJ_OUT:]

    @pl.when(pl.program_id(0) == 0)
    def _():
        hm = _ffn_rows(xm_ref[...], g_ref, w_in_ref, w_out_ref)
        _proj_rows(hm, gp_ref, wp_ref, um_ref, qm_ref, km_ref, vm_ref)

    h = _ffn_rows(x_ref[...], g_ref, w_in_ref, w_out_ref)
    h_ref[...] = h
    _proj_rows(h, gp_ref, wp_ref, u_ref, q_ref, k_ref, v_ref)
    for src, dst in zip(cast_in, cast_out):
        dst[...] = src[...].astype(BF16)


def _ffn_proj_call(x, x_meta, g, w_in, w_out, g_proj, w_proj, to_cast):
    rows = x.shape[0]
    steps = rows // ROW_TILE
    row_spec = lambda width: pl.BlockSpec((ROW_TILE, width), lambda i: (i, 0))
    meta_spec = lambda width: pl.BlockSpec((N_META, width), lambda i: (0, 0))
    wide = lambda n, dt: jax.ShapeDtypeStruct((n, SSM_WIDTH), dt)

    def cast_specs(a, col_block, n_cols):
        n = a.shape[0]
        chunk = min(c for c in range(BF16_ROWS, n + 1, BF16_ROWS) if n % c == 0 and n // c <= steps)
        last = n // chunk - 1
        return (pl.BlockSpec((chunk, n_cols), lambda i: (jnp.minimum(i, last), col_block)),
                pl.BlockSpec((chunk, n_cols), lambda i: (jnp.minimum(i, last), 0)))

    cast_in_specs, cast_out_specs = zip(*[cast_specs(*c) for c in to_cast])
    cast_shapes = [jax.ShapeDtypeStruct((a.shape[0], n_cols), BF16) for a, _, n_cols in to_cast]
    return pl.pallas_call(
        _ffn_proj_body,
        grid=(steps,),
        in_specs=[row_spec(D_MODEL), meta_spec(D_MODEL), _const_spec((1, D_MODEL)),
                  _const_spec((D_MODEL, 2 * D_FF)), _const_spec((D_FF, D_MODEL)),
                  _const_spec((1, D_MODEL)), _const_spec((D_MODEL, UQKV_WIDTH))] + list(cast_in_specs),
        out_specs=[row_spec(D_MODEL)] + [row_spec(SSM_WIDTH)] * 4 + [meta_spec(SSM_WIDTH)] * 4
        + list(cast_out_specs),
        out_shape=[jax.ShapeDtypeStruct((rows, D_MODEL), F32), wide(rows, F32)] + [wide(rows, BF16)] * 3
        + [wide(N_META, F32)] + [wide(N_META, BF16)] * 3 + cast_shapes,
        compiler_params=_params(1),
        name="ffn_proj",
    )(x, x_meta, g, w_in, w_out, g_proj, w_proj, *[a for a, _, _ in to_cast])


def _ffn_call(x, g, w_in, w_out, gf, *, grid, x_spec, o_spec, out_shape, final_norm):
    return pl.pallas_call(
        functools.partial(_ffn_body, final_norm=final_norm),
        grid=grid,
        in_specs=[x_spec, _const_spec((1, D_MODEL)), _const_spec((D_MODEL, 2 * D_FF)),
                  _const_spec((D_FF, D_MODEL)), _const_spec((1, D_MODEL))],
        out_specs=o_spec,
        out_shape=jax.ShapeDtypeStruct(out_shape, F32),
        compiler_params=_params(len(grid)),
        name="ffn",
    )(x, g, w_in, w_out, gf)


def _same_group(shape, ch_axis, ch0, state0):
    ch = lax.broadcasted_iota(jnp.int32, shape, ch_axis) + ch0
    st = lax.broadcasted_iota(jnp.int32, shape, 1 - ch_axis) + state0
    return (ch >> int(math.log2(SSM_GROUP))) == (st >> int(math.log2(SSM_STATE)))


def _disc_body(a_re_ref, a_im_ref, log_dt_ref, b_re_ref, b_im_ref,
               lb_re_ref, lb_im_ref, w_re_ref, w_im_ref):
    dt = jnp.exp(log_dt_ref[...])
    a_re = jnp.minimum(a_re_ref[...], A_RE_MAX)
    a_im = a_im_ref[...]
    mag = jnp.exp(a_re * dt)
    lb_re = mag * jnp.cos(a_im * dt)
    lb_im = mag * jnp.sin(a_im * dt)
    den = a_re * a_re + a_im * a_im
    nr = lb_re - 1.0
    ni = lb_im
    f_re = (nr * a_re + ni * a_im) / den
    f_im = (ni * a_re - nr * a_im) / den
    lb_re_ref[...] = lb_re
    lb_im_ref[...] = lb_im
    bb_re = f_re * b_re_ref[...] - f_im * b_im_ref[...]
    bb_im = f_re * b_im_ref[...] + f_im * b_re_ref[...]
    for j in range(SCAN_BLOCKS):
        same = _same_group((LANES, SCAN_COLS), 0, _lane_tile(j) * LANES, j * SCAN_COLS)
        for bb, w_ref in ((bb_re, w_re_ref), (bb_im, w_im_ref)):
            rows = jnp.concatenate([bb[:, j * SCAN_COLS:(j + 1) * SCAN_COLS]] * (LANES // SSM_GROUP), axis=0)
            w_ref[j] = jnp.where(same, rows, 0.0).astype(BF16)


def _disc_call(a_re, a_im, log_dt, b_re_t, b_im_t):
    row = jax.ShapeDtypeStruct((1, N_STATES), F32)
    slabs = jax.ShapeDtypeStruct((SCAN_BLOCKS, LANES, SCAN_COLS), BF16)
    return pl.pallas_call(_disc_body, out_shape=[row, row, slabs, slabs], name="disc")(
        a_re, a_im, log_dt, b_re_t, b_im_t)


def _cslab_body(c_re_ref, c_im_ref, ct_re_ref, ct_im_ref):
    for j in range(SCAN_BLOCKS):
        same = _same_group((LANES, SCAN_COLS), 0, _lane_tile(j) * LANES, j * SCAN_COLS)
        for c_ref, ct_ref, sign in ((c_re_ref, ct_re_ref, 1.0), (c_im_ref, ct_im_ref, -1.0)):
            c = c_ref[_lane_tile(j) * LANES:(_lane_tile(j) + 1) * LANES, :] * sign
            cols = jnp.concatenate([c] * (SCAN_COLS // SSM_STATE), axis=1)
            ct_ref[j] = jnp.where(same, cols, 0.0).astype(BF16)


def _cslab_call(c_re, c_im):
    slabs = jax.ShapeDtypeStruct((SCAN_BLOCKS, LANES, SCAN_COLS), BF16)
    return pl.pallas_call(_cslab_body, out_shape=[slabs, slabs], name="cslab")(c_re, c_im)


SCAN_BLOCKS = N_STATES // SCAN_COLS
BLOCKS_PER_LANE_TILE = LANES * SSM_STATE // (SSM_GROUP * SCAN_COLS)


def _lane_tile(j):
    return j // BLOCKS_PER_LANE_TILE


def _s5_body(*refs, reverse, steps, finish):
    if finish:
        (u_ref, init_ref, lb_re_ref, lb_im_ref, w_re_ref, w_im_ref, c_re_ref, c_im_ref,
         yf_ref, d_ref, w_glu_ref, o_ref, fin_ref, bu_re, bu_im, h_re, h_im) = refs
    else:
        (u_ref, init_ref, lb_re_ref, lb_im_ref, w_re_ref, w_im_ref, c_re_ref, c_im_ref,
         o_ref, fin_ref, bu_re, bu_im, h_re, h_im) = refs
    nb = init_ref.shape[0]

    @pl.when(pl.program_id(0) == 0)
    def _():
        h_re[...] = init_ref[:, :N_STATES]
        h_im[...] = init_ref[:, N_STATES:]

    def scan_order(ref):
        if not reverse:
            return ref[...]
        return jnp.concatenate(
            [ref[(steps - 1 - s) * nb:(steps - s) * nb, :] for s in range(steps)], axis=0)

    u = scan_order(u_ref)
    ub = u.astype(BF16)
    y_parts = []
    for j in range(SCAN_BLOCKS):
        cols = slice(j * SCAN_COLS, (j + 1) * SCAN_COLS)
        us = ub[:, _lane_tile(j) * LANES:(_lane_tile(j) + 1) * LANES]
        bu_re[j] = _dot(us, w_re_ref[j])
        bu_im[j] = _dot(us, w_im_ref[j])
        lr = jnp.broadcast_to(lb_re_ref[:, cols], (nb, SCAN_COLS))
        li = jnp.broadcast_to(lb_im_ref[:, cols], (nb, SCAN_COLS))
        hr, hi = h_re[:, cols], h_im[:, cols]
        for s in range(steps):
            rows = slice(s * nb, (s + 1) * nb)
            hr, hi = (lr * hr - li * hi + bu_re[j, rows, :],
                      lr * hi + li * hr + bu_im[j, rows, :])
            bu_re[j, rows, :] = hr
            bu_im[j, rows, :] = hi
        h_re[:, cols] = hr
        h_im[:, cols] = hi
        part = (_dot_nt(bu_re[j].astype(BF16), c_re_ref[j])
                + _dot_nt(bu_im[j].astype(BF16), c_im_ref[j]))
        if j % BLOCKS_PER_LANE_TILE == 0:
            y_parts.append(part)
        else:
            y_parts[-1] = y_parts[-1] + part
    y = jnp.concatenate(y_parts, axis=-1)
    fin_ref[:, :N_STATES] = h_re[...]
    fin_ref[:, N_STATES:] = h_im[...]
    if finish:
        y = y + scan_order(yf_ref) + d_ref[...] * u
        z = jax.nn.gelu(y)
        y = (z * jax.nn.sigmoid(_dot(z.astype(BF16), w_glu_ref[...]))).astype(BF16)
    if reverse:
        for s in range(steps):
            o_ref[(steps - 1 - s) * nb:(steps - s) * nb, :] = y[s * nb:(s + 1) * nb, :]
    else:
        o_ref[...] = y


def _s5_rider_body(*refs, n_in, s5_body, rider_prologue, rider_body):
    s5_in, rider_in = refs[:n_in], refs[n_in:n_in + 6]
    outs, scratch = refs[n_in + 6:n_in + 9], refs[n_in + 9:]
    rider_prologue(*rider_in, outs[2], scratch[4])
    s5_body(*s5_in, *outs[:2], *scratch[:4])
    rider_body(*rider_in, outs[2], scratch[4])


def _s5_call(u, init, lb_re, lb_im, w_re, w_im, c_re, c_im, finish_args, *, tile, reverse, rider=None):
    rows = u.shape[0]
    nb = init.shape[0]
    n_tiles = rows // tile
    finish = finish_args is not None
    row_map = (lambda i: (n_tiles - 1 - i, 0)) if reverse else (lambda i: (i, 0))
    row_spec = pl.BlockSpec((tile, SSM_WIDTH), row_map)
    in_specs = [row_spec, _const_spec((nb, 2 * N_STATES)),
                _const_spec((1, N_STATES)), _const_spec((1, N_STATES)),
                _const_spec((SCAN_BLOCKS, LANES, SCAN_COLS)), _const_spec((SCAN_BLOCKS, LANES, SCAN_COLS)),
                _const_spec((SCAN_BLOCKS, LANES, SCAN_COLS)), _const_spec((SCAN_BLOCKS, LANES, SCAN_COLS))]
    args = [u, init, lb_re, lb_im, w_re, w_im, c_re, c_im]
    if finish:
        in_specs += [row_spec, _const_spec((1, SSM_WIDTH)), _const_spec((SSM_WIDTH, SSM_WIDTH))]
        args += list(finish_args)
    body = functools.partial(_s5_body, reverse=reverse, steps=tile // nb, finish=finish)
    out_specs = [row_spec, pl.BlockSpec((nb, 2 * N_STATES), lambda i: (0, 0))]
    out_shape = [jax.ShapeDtypeStruct((rows, SSM_WIDTH), BF16 if finish else F32),
                 jax.ShapeDtypeStruct((nb, 2 * N_STATES), F32)]
    scratch = [pltpu.VMEM((SCAN_BLOCKS, tile, SCAN_COLS), F32),
               pltpu.VMEM((SCAN_BLOCKS, tile, SCAN_COLS), F32),
               pltpu.VMEM((nb, N_STATES), F32), pltpu.VMEM((nb, N_STATES), F32)]
    if rider is not None:
        assert rider["steps"] == n_tiles, "one rider unit per scan tile"
        body = functools.partial(_s5_rider_body, n_in=len(args), s5_body=body,
                                 rider_prologue=rider["prologue"], rider_body=rider["body"])
        in_specs = in_specs + rider["in_specs"]
        args = args + rider["args"]
        out_specs.append(rider["out_spec"])
        out_shape.append(rider["out_shape"])
        scratch.append(rider["scratch"])
    return pl.pallas_call(
        body,
        grid=(n_tiles,),
        in_specs=in_specs,
        out_specs=out_specs,
        out_shape=out_shape,
        scratch_shapes=scratch,
        compiler_params=_params(1),
        name="s5_bwd" if reverse else "s5_fwd",
    )(*args)


NA_Q = NA_Q_ROWS * GRID_W
NA_K = NA_K_ROWS * GRID_W
NA_KEYS = NA_K + NA_META_PAD
NA_RING = LANES


def _block_key_row(blk, rows):
    return min(max(blk * NA_Q_ROWS - WIN_H // 2, 0), rows - NA_K_ROWS)


def _na_assemble_bias(ring_ref, bias, n_blocks, rows):
    kh = min(WIN_H, rows)
    masked = jnp.full((GRID_W, GRID_W), MASK_VALUE, F32)
    meta = jnp.where(lax.broadcasted_iota(jnp.int32, (GRID_W, GRID_W), 1) < N_META, 0.0, MASK_VALUE)
    cq = lax.broadcasted_iota(jnp.int32, (GRID_W, GRID_W), 0)
    ck = lax.broadcasted_iota(jnp.int32, (GRID_W, GRID_W), 1)
    col_start = jnp.clip(cq - WIN_W // 2, 0, GRID_W - WIN_W)
    col_valid = (ck >= col_start) & (ck < col_start + WIN_W)
    col_tiles = {}

    def col_tile(head, off):
        if (head, off) not in col_tiles:
            ring = jnp.broadcast_to(ring_ref[head, off:off + 1, :], (GRID_W, NA_RING))
            toeplitz = pltpu.roll(ring, 0, 1, stride=1, stride_axis=0)[:, :GRID_W]
            col_tiles[head, off] = jnp.where(col_valid, toeplitz, MASK_VALUE)
        return col_tiles[head, off]

    for kind, blk in enumerate((0, 1, n_blocks - 1)):
        for head in range(2):
            for qr in range(NA_Q_ROWS):
                r = blk * NA_Q_ROWS + qr
                row_start = min(max(r - kh // 2, 0), rows - kh)
                tiles = []
                for slot in range(NA_K_ROWS):
                    kr = _block_key_row(blk, rows) + slot
                    valid = row_start <= kr < row_start + kh
                    tiles.append(col_tile(head, kr - r + WIN_H - 1) if valid else masked)
                tiles.append(meta)
                for c in range(0, len(tiles), 2):
                    bias[kind, head, qr * GRID_W:(qr + 1) * GRID_W, c * GRID_W:(c + 2) * GRID_W] = (
                        jnp.concatenate(tiles[c:c + 2], axis=1))


def _na_prologue(q_ref, k_ref, v_ref, km_ref, vm_ref, t_ref, o_ref, bias, *, n_blocks, rows, batch):
    @pl.when(pl.program_id(0) % batch == 0)
    def _():
        _na_assemble_bias(t_ref, bias, n_blocks, rows)


def _na_body(q_ref, k_ref, v_ref, km_ref, vm_ref, t_ref, o_ref, bias, *, n_blocks, rows, batch):
    first_head = lax.broadcasted_iota(jnp.int32, (1, LANES), 1) < NA_HEAD_DIM
    km = km_ref[...]
    vm = vm_ref[...]
    for i in range(n_blocks):
        k0 = _block_key_row(i, rows) * GRID_W
        q0 = i * NA_Q
        kind = 0 if i == 0 else 2 if i == n_blocks - 1 else 1
        q = q_ref[q0:q0 + NA_Q, :]
        zero = jnp.zeros_like(q)
        qs = jnp.concatenate([jnp.where(first_head, q, zero), jnp.where(first_head, zero, q)], axis=0)
        keys = jnp.concatenate([k_ref[k0:k0 + NA_K, :], km], axis=0)
        vals = jnp.concatenate([v_ref[k0:k0 + NA_K, :], vm], axis=0)
        s = _dot_nt(qs, keys) + bias[kind].reshape(2 * NA_Q, NA_KEYS)
        m = jnp.max(s, axis=-1, keepdims=True)
        p = jnp.exp(s - m)
        l = jnp.sum(p, axis=-1, keepdims=True)
        o = _dot(p.astype(BF16), vals) / l
        o_ref[q0:q0 + NA_Q, :] = jnp.where(first_head, o[:NA_Q], o[NA_Q:]).astype(BF16)


def _na_rider(q, k, v, km, vm, col_bias, *, first_pair, n_pairs, batch, seq):
    rows = seq // GRID_W
    assert NA_META_PAD == GRID_W and (NA_K_ROWS + 1) % 2 == 0
    pair = lambda i: first_pair + i // batch
    seq_spec = pl.BlockSpec((seq, LANES), lambda i: (i % batch, pair(i)))
    meta_spec = pl.BlockSpec((NA_META_PAD, LANES), lambda i: (0, pair(i)))
    table_spec = pl.BlockSpec((2, 2 * WIN_H - 1, NA_RING), lambda i: (pair(i), 0, 0))
    return dict(
        steps=n_pairs * batch,
        prologue=functools.partial(_na_prologue, n_blocks=rows // NA_Q_ROWS, rows=rows, batch=batch),
        body=functools.partial(_na_body, n_blocks=rows // NA_Q_ROWS, rows=rows, batch=batch),
        args=[q, k, v, km, vm, col_bias],
        in_specs=[seq_spec, seq_spec, seq_spec, meta_spec, meta_spec, table_spec],
        out_spec=pl.BlockSpec((seq, LANES), lambda i: (i % batch, i // batch)),
        out_shape=jax.ShapeDtypeStruct((batch * seq, n_pairs * LANES), BF16),
        scratch=pltpu.VMEM((3, 2, NA_Q, NA_KEYS), F32))


def _na_col_ring(rpb):
    n_head, n_off, _ = rpb.shape
    ring = jnp.zeros((n_head, n_off, NA_RING), F32)
    ring = ring.at[..., :WIN_W].set(rpb[..., WIN_W - 1:].astype(F32))
    return ring.at[..., NA_RING - (WIN_W - 1):].set(rpb[..., :WIN_W - 1].astype(F32))


def _merge_body(h_ref, z_ref, a_lo_ref, a_hi_ref, g_ref, w_gate_ref, w_ssm_ref, w_na_ref, w_out_ref, o_ref):
    h = h_ref[...]
    hn = _rms_norm(h, g_ref[...]).astype(BF16)
    gates = _dot(hn, w_gate_ref[...])
    y_ssm = _dot(z_ref[...], w_ssm_ref[...])
    y_na = (_dot(a_lo_ref[...], w_na_ref[:NA_WIDTH // 2, :])
            + _dot(a_hi_ref[...], w_na_ref[NA_WIDTH // 2:, :]))
    merged = (jax.nn.sigmoid(gates[:, :D_MODEL]) * y_ssm
              + jax.nn.sigmoid(gates[:, D_MODEL:]) * y_na)
    o_ref[...] = h + _dot(merged.astype(BF16), w_out_ref[...])


def _merge_call(h1, z, a_lo, a_hi, g, w_gate, w_ssm, w_na, w_out):
    rows = h1.shape[0]
    row_spec = lambda width: pl.BlockSpec((WIDE_ROW_TILE, width), lambda i: (i, 0))
    return pl.pallas_call(
        _merge_body,
        grid=(rows // WIDE_ROW_TILE,),
        in_specs=[row_spec(D_MODEL), row_spec(SSM_WIDTH), row_spec(NA_WIDTH // 2), row_spec(NA_WIDTH // 2),
                  _const_spec((1, D_MODEL)), _const_spec((D_MODEL, 2 * D_MODEL)),
                  _const_spec((SSM_WIDTH, D_MODEL)), _const_spec((NA_WIDTH, D_MODEL)),
                  _const_spec((D_MODEL, D_MODEL))],
        out_specs=row_spec(D_MODEL),
        out_shape=jax.ShapeDtypeStruct((rows, D_MODEL), F32),
        compiler_params=_params(1),
        name="merge",
    )(h1, z, a_lo, a_hi, g, w_gate, w_ssm, w_na, w_out)


def kernel(x, meta_tokens, norm_ffn1, w_ffn1_in, w_ffn1_out, norm_mix, w_in, ssm_a_re_fwd, ssm_a_im_fwd, ssm_log_dt_fwd, ssm_a_re_bwd, ssm_a_im_bwd, ssm_log_dt_bwd, ssm_b_re, ssm_b_im, ssm_c_re, ssm_c_im, ssm_d, w_glu, na_rpb, w_branch_ssm, w_branch_na, w_out, norm_ffn2, w_ffn2_in, w_ffn2_out, norm_final):
    batch, seq, d_model = x.shape
    assert d_model == D_MODEL and norm_ffn1.shape[0] == 1, "single-layer block of width 1024"
    assert seq % ROW_TILE == 0 and seq % (GRID_W * NA_Q_ROWS) == 0 and batch % 8 == 0
    rows = seq // GRID_W
    assert rows >= NA_K_ROWS + NA_Q_ROWS

    row = lambda a: a.reshape(1, -1).astype(F32)
    g1, gm, g2, gf = row(norm_ffn1[0]), row(norm_mix[0]), row(norm_ffn2[0]), row(norm_final)
    w1_in, w1_out = w_ffn1_in[0].astype(BF16), w_ffn1_out[0].astype(BF16)
    w_uqkv = w_in[0][:, :UQKV_WIDTH].astype(BF16)
    whole = lambda a: (a[0].astype(F32), 0, a.shape[-1])
    later_weights = [whole(w_ffn2_in), whole(w_ffn2_out), (w_in[0].astype(F32), 1, UQKV_WIDTH),
                     whole(w_branch_ssm), whole(w_branch_na), whole(w_out), whole(w_glu)]
    assert w_in.shape[-1] == 2 * UQKV_WIDTH, "the two gates are the second column block of w_in"

    n_rows = batch * seq
    tok_spec = pl.BlockSpec((ROW_TILE, D_MODEL), lambda i: (i, 0))
    (h1, u, q, k, v, u_meta, _, k_meta, v_meta,
     w2_in, w2_out, w_gate, w_ssm, w_na, w_mix_out, w_glu_bf) = _ffn_proj_call(
        x.reshape(n_rows, D_MODEL), meta_tokens.astype(F32), g1, w1_in, w1_out, gm, w_uqkv,
        later_weights)
    u = u.reshape(batch, seq, SSM_WIDTH).transpose(1, 0, 2).reshape(n_rows, SSM_WIDTH)

    states = lambda a: a[0].reshape(1, N_STATES).astype(F32)
    per_state = lambda a: jnp.repeat(a[0].astype(F32), SSM_STATE).reshape(1, N_STATES)
    b_re_t = ssm_b_re[0].reshape(N_STATES, SSM_GROUP).T.astype(F32)
    b_im_t = ssm_b_im[0].reshape(N_STATES, SSM_GROUP).T.astype(F32)
    disc_fwd = (states(ssm_a_re_fwd), states(ssm_a_im_fwd), per_state(ssm_log_dt_fwd), b_re_t, b_im_t)
    disc_bwd = (states(ssm_a_re_bwd), states(ssm_a_im_bwd), per_state(ssm_log_dt_bwd), b_re_t, b_im_t)
    c_parts = (ssm_c_re[0].reshape(SSM_WIDTH, SSM_STATE).astype(F32),
               ssm_c_im[0].reshape(SSM_WIDTH, SSM_STATE).astype(F32))
    u_meta, disc_fwd, disc_bwd, c_parts = lax.optimization_barrier((u_meta, disc_fwd, disc_bwd, c_parts))
    lbf_re, lbf_im, wf_re, wf_im = _disc_call(*disc_fwd)
    lbb_re, lbb_im, wb_re, wb_im = _disc_call(*disc_bwd)
    c_re, c_im = _cslab_call(*c_parts)

    zero_state = jnp.zeros((batch, 2 * N_STATES), F32)
    u_meta_rows = jnp.repeat(u_meta, batch, axis=0)
    _, state_meta = _s5_call(u_meta_rows, zero_state, lbf_re, lbf_im, wf_re, wf_im, c_re, c_im,
                             None, tile=N_META * batch, reverse=False)
    pad_meta = lambda a: jnp.pad(a, ((0, NA_META_PAD - N_META), (0, 0)))
    half = NA_HEADS // 4
    riders = [_na_rider(q, k, v, pad_meta(k_meta), pad_meta(v_meta), _na_col_ring(na_rpb[0]),
                        first_pair=p0, n_pairs=half, batch=batch, seq=seq) for p0 in (0, half)]
    y_fwd, _, att_lo = _s5_call(u, state_meta, lbf_re, lbf_im, wf_re, wf_im, c_re, c_im,
                                None, tile=SCAN_TILE, reverse=False, rider=riders[0])
    z, _, att_hi = _s5_call(u, zero_state, lbb_re, lbb_im, wb_re, wb_im, c_re, c_im,
                            (y_fwd, row(ssm_d[0]), w_glu_bf), tile=SCAN_TILE,
                            reverse=True, rider=riders[1])

    z = z.reshape(seq, batch, SSM_WIDTH).transpose(1, 0, 2).reshape(n_rows, SSM_WIDTH)
    h2 = _merge_call(h1, z, att_lo, att_hi, gm, w_gate,
                     w_ssm, w_na, w_mix_out)

    out = _ffn_call(h2, g2, w2_in, w2_out, gf, grid=(n_rows // ROW_TILE,), x_spec=tok_spec,
                    o_spec=tok_spec, out_shape=(n_rows, D_MODEL), final_norm=True)
    return out.reshape(batch, seq, D_MODEL)
```

```python
import functools
import math

import jax
import jax.numpy as jnp
from jax import lax
from jax.experimental import pallas as pl
from jax.experimental.pallas import tpu as pltpu

D_MODEL = 1024
N_META = 16
GRID_W = 64
D_FF = 2816
SSM_WIDTH = 512
SSM_GROUP = 16
SSM_GROUPS = 32
SSM_STATE = 64
N_STATES = SSM_GROUPS * SSM_STATE
NA_HEAD_DIM = 64
NA_WIDTH = 512
NA_HEADS = 8
WIN_H = 8
WIN_W = 16
UQKV_WIDTH = SSM_WIDTH + 3 * NA_WIDTH
RMS_EPS = 1e-6
A_RE_MAX = -1e-4
MASK_VALUE = -1e30

LANES = 128
BF16_ROWS = 16
VMEM_LIMIT = 56 * 1024 * 1024

ROW_TILE = 512
FFN2_ROW_TILE = 1024
FFN2_HIDDEN_CHUNKS = ((0, 1536), (1536, D_FF))
WIDE_ROW_TILE = 1024
SCAN_COLS = 256
SCAN_TILE = 1024
NA_Q_ROWS = 4
NA_K_ROWS = NA_Q_ROWS + WIN_H - 1
NA_META_PAD = GRID_W

BF16 = jnp.bfloat16
F32 = jnp.float32


def _const_spec(shape):
    zeros = (0,) * len(shape)
    return pl.BlockSpec(shape, lambda *_: zeros, pipeline_mode=pl.Buffered(1))


def _params(n_grid_dims):
    return pltpu.CompilerParams(
        dimension_semantics=("arbitrary",) * n_grid_dims,
        vmem_limit_bytes=VMEM_LIMIT)


def _rms_scale(x):
    return lax.rsqrt(jnp.mean(x * x, axis=-1, keepdims=True) + RMS_EPS)


def _rms_norm(x, g):
    return x * _rms_scale(x) * g


def _dot(a, b):
    return jnp.dot(a, b, preferred_element_type=F32)


def _dot_nt(a, b):
    return lax.dot_general(a, b, (((1,), (1,)), ((), ())), preferred_element_type=F32)


def _ffn_rows(x, g_ref, w_in_ref, w_out_ref, hidden_chunks=((0, D_FF),)):
    r = _rms_scale(x)
    xg = (x * g_ref[...]).astype(BF16)
    acc = None
    for lo, hi in hidden_chunks:
        gate = _dot(xg, w_in_ref[:, lo:hi]) * r
        up = _dot(xg, w_in_ref[:, D_FF + lo:D_FF + hi]) * r
        act = (jax.nn.silu(gate) * up).astype(BF16)
        part = _dot(act, w_out_ref[lo:hi, :])
        acc = part if acc is None else acc + part
    return x + 0.5 * acc


def _ffn_body(x_ref, g_ref, w_in_ref, w_out_ref, gf_ref, o_ref, *, final_norm):
    y = _ffn_rows(x_ref[...], g_ref, w_in_ref, w_out_ref, FFN2_HIDDEN_CHUNKS)
    if final_norm:
        y = _rms_norm(y, gf_ref[...])
    o_ref[...] = y


def _proj_rows(h, g_ref, w_ref, u_ref, q_ref, k_ref, v_ref):
    p = _dot((h * g_ref[...]).astype(BF16), w_ref[...]) * _rms_scale(h)
    u_ref[...] = p[:, :SSM_WIDTH]
    q_ref[...] = (p[:, SSM_WIDTH:SSM_WIDTH + NA_WIDTH] * (NA_HEAD_DIM ** -0.5)).astype(BF16)
    k_ref[...] = p[:, SSM_WIDTH + NA_WIDTH:SSM_WIDTH + 2 * NA_WIDTH].astype(BF16)
    v_ref[...] = p[:, SSM_WIDTH + 2 * NA_WIDTH:].astype(BF16)


N_FFN_PROJ_IN = 7
N_FFN_PROJ_OUT = 9


def _ffn_proj_body(*refs):
    x_ref, xm_ref, g_ref, w_in_ref, w_out_ref, gp_ref, wp_ref = refs[:N_FFN_PROJ_IN]
    n_cast = (len(refs) - N_FFN_PROJ_IN - N_FFN_PROJ_OUT) // 2
    cast_in = refs[N_FFN_PROJ_IN:N_FFN_PROJ_IN + n_cast]
    outs = refs[N_FFN_PROJ_IN + n_cast:]
    h_ref, u_ref, q_ref, k_ref, v_ref, um_ref, qm_ref, km_ref, vm_ref = outs[:N_FFN_PROJ_OUT]
    cast_out = outs[N_FFN_PROJ_OUT:]

    @pl.when(pl.program_id(0) == 0)
    def _():
        hm = _ffn_rows(xm_ref[...], g_ref, w_in_ref, w_out_ref)
        _proj_rows(hm, gp_ref, wp_ref, um_ref, qm_ref, km_ref, vm_ref)

    h = _ffn_rows(x_ref[...], g_ref, w_in_ref, w_out_ref)
    h_ref[...] = h
    _proj_rows(h, gp_ref, wp_ref, u_ref, q_ref, k_ref, v_ref)
    for src, dst in zip(cast_in, cast_out):
        dst[...] = src[...].astype(BF16)


def _ffn_proj_call(x, x_meta, g, w_in, w_out, g_proj, w_proj, to_cast):
    rows = x.shape[0]
    steps = rows // ROW_TILE
    row_spec = lambda width: pl.BlockSpec((ROW_TILE, width), lambda i: (i, 0))
    meta_spec = lambda width: pl.BlockSpec((N_META, width), lambda i: (0, 0))
    wide = lambda n, dt: jax.ShapeDtypeStruct((n, SSM_WIDTH), dt)

    def cast_specs(a, col_block, n_cols):
        n = a.shape[0]
        chunk = min(c for c in range(BF16_ROWS, n + 1, BF16_ROWS) if n % c == 0 and n // c <= steps)
        last = n // chunk - 1
        return (pl.BlockSpec((chunk, n_cols), lambda i: (jnp.minimum(i, last), col_block)),
                pl.BlockSpec((chunk, n_cols), lambda i: (jnp.minimum(i, last), 0)))

    cast_in_specs, cast_out_specs = zip(*[cast_specs(*c) for c in to_cast])
    cast_shapes = [jax.ShapeDtypeStruct((a.shape[0], n_cols), BF16) for a, _, n_cols in to_cast]
    return pl.pallas_call(
        _ffn_proj_body,
        grid=(steps,),
        in_specs=[row_spec(D_MODEL), meta_spec(D_MODEL), _const_spec((1, D_MODEL)),
                  _const_spec((D_MODEL, 2 * D_FF)), _const_spec((D_FF, D_MODEL)),
                  _const_spec((1, D_MODEL)), _const_spec((D_MODEL, UQKV_WIDTH))] + list(cast_in_specs),
        out_specs=[row_spec(D_MODEL)] + [row_spec(SSM_WIDTH)] * 4 + [meta_spec(SSM_WIDTH)] * 4
        + list(cast_out_specs),
        out_shape=[jax.ShapeDtypeStruct((rows, D_MODEL), F32), wide(rows, F32)] + [wide(rows, BF16)] * 3
        + [wide(N_META, F32)] + [wide(N_META, BF16)] * 3 + cast_shapes,
        compiler_params=_params(1),
        name="ffn_proj",
    )(x, x_meta, g, w_in, w_out, g_proj, w_proj, *[a for a, _, _ in to_cast])


def _ffn_call(x, g, w_in, w_out, gf, *, grid, x_spec, o_spec, out_shape, final_norm):
    return pl.pallas_call(
        functools.partial(_ffn_body, final_norm=final_norm),
        grid=grid,
        in_specs=[x_spec, _const_spec((1, D_MODEL)), _const_spec((D_MODEL, 2 * D_FF)),
                  _const_spec((D_FF, D_MODEL)), _const_spec((1, D_MODEL))],
        out_specs=o_spec,
        out_shape=jax.ShapeDtypeStruct(out_shape, F32),
        compiler_params=_params(len(grid)),
        name="ffn",
    )(x, g, w_in, w_out, gf)


def _same_group(shape, ch_axis, ch0, state0):
    ch = lax.broadcasted_iota(jnp.int32, shape, ch_axis) + ch0
    st = lax.broadcasted_iota(jnp.int32, shape, 1 - ch_axis) + state0
    return (ch >> int(math.log2(SSM_GROUP))) == (st >> int(math.log2(SSM_STATE)))


def _disc_body(a_re_ref, a_im_ref, log_dt_ref, b_re_ref, b_im_ref,
               lb_re_ref, lb_im_ref, w_re_ref, w_im_ref):
    dt = jnp.exp(log_dt_ref[...])
    a_re = jnp.minimum(a_re_ref[...], A_RE_MAX)
    a_im = a_im_ref[...]
    mag = jnp.exp(a_re * dt)
    lb_re = mag * jnp.cos(a_im * dt)
    lb_im = mag * jnp.sin(a_im * dt)
    den = a_re * a_re + a_im * a_im
    nr = lb_re - 1.0
    ni = lb_im
    f_re = (nr * a_re + ni * a_im) / den
    f_im = (ni * a_re - nr * a_im) / den
    lb_re_ref[...] = lb_re
    lb_im_ref[...] = lb_im
    bb_re = f_re * b_re_ref[...] - f_im * b_im_ref[...]
    bb_im = f_re * b_im_ref[...] + f_im * b_re_ref[...]
    for j in range(SCAN_BLOCKS):
        same = _same_group((LANES, SCAN_COLS), 0, _lane_tile(j) * LANES, j * SCAN_COLS)
        for bb, w_ref in ((bb_re, w_re_ref), (bb_im, w_im_ref)):
            rows = jnp.concatenate([bb[:, j * SCAN_COLS:(j + 1) * SCAN_COLS]] * (LANES // SSM_GROUP), axis=0)
            w_ref[j] = jnp.where(same, rows, 0.0).astype(BF16)


def _disc_call(a_re, a_im, log_dt, b_re_t, b_im_t):
    row = jax.ShapeDtypeStruct((1, N_STATES), F32)
    slabs = jax.ShapeDtypeStruct((SCAN_BLOCKS, LANES, SCAN_COLS), BF16)
    return pl.pallas_call(_disc_body, out_shape=[row, row, slabs, slabs], name="disc")(
        a_re, a_im, log_dt, b_re_t, b_im_t)


def _cslab_body(c_re_ref, c_im_ref, ct_re_ref, ct_im_ref):
    for j in range(SCAN_BLOCKS):
        same = _same_group((LANES, SCAN_COLS), 0, _lane_tile(j) * LANES, j * SCAN_COLS)
        for c_ref, ct_ref, sign in ((c_re_ref, ct_re_ref, 1.0), (c_im_ref, ct_im_ref, -1.0)):
            c = c_ref[_lane_tile(j) * LANES:(_lane_tile(j) + 1) * LANES, :] * sign
            cols = jnp.concatenate([c] * (SCAN_COLS // SSM_STATE), axis=1)
            ct_ref[j] = jnp.where(same, cols, 0.0).astype(BF16)


def _cslab_call(c_re, c_im):
    slabs = jax.ShapeDtypeStruct((SCAN_BLOCKS, LANES, SCAN_COLS), BF16)
    return pl.pallas_call(_cslab_body, out_shape=[slabs, slabs], name="cslab")(c_re, c_im)


SCAN_BLOCKS = N_STATES // SCAN_COLS
BLOCKS_PER_LANE_TILE = LANES * SSM_STATE // (SSM_GROUP * SCAN_COLS)


def _lane_tile(j):
    return j // BLOCKS_PER_LANE_TILE


def _s5_body(*refs, reverse, steps, finish):
    if finish:
        (u_ref, init_ref, lb_re_ref, lb_im_ref, w_re_ref, w_im_ref, c_re_ref, c_im_ref,
         yf_ref, d_ref, w_glu_ref, o_ref, fin_ref, bu_re, bu_im, h_re, h_im) = refs
    else:
        (u_ref, init_ref, lb_re_ref, lb_im_ref, w_re_ref, w_im_ref, c_re_ref, c_im_ref,
         o_ref, fin_ref, bu_re, bu_im, h_re, h_im) = refs
    nb = init_ref.shape[0]

    @pl.when(pl.program_id(0) == 0)
    def _():
        h_re[...] = init_ref[:, :N_STATES]
        h_im[...] = init_ref[:, N_STATES:]

    def scan_order(ref):
        if not reverse:
            return ref[...]
        return jnp.concatenate(
            [ref[(steps - 1 - s) * nb:(steps - s) * nb, :] for s in range(steps)], axis=0)

    u = scan_order(u_ref)
    ub = u.astype(BF16)
    y_parts = []
    for j in range(SCAN_BLOCKS):
        cols = slice(j * SCAN_COLS, (j + 1) * SCAN_COLS)
        us = ub[:, _lane_tile(j) * LANES:(_lane_tile(j) + 1) * LANES]
        bu_re[j] = _dot(us, w_re_ref[j])
        bu_im[j] = _dot(us, w_im_ref[j])
        lr = jnp.broadcast_to(lb_re_ref[:, cols], (nb, SCAN_COLS))
        li = jnp.broadcast_to(lb_im_ref[:, cols], (nb, SCAN_COLS))
        hr, hi = h_re[:, cols], h_im[:, cols]
        for s in range(steps):
            rows = slice(s * nb, (s + 1) * nb)
            hr, hi = (lr * hr - li * hi + bu_re[j, rows, :],
                      lr * hi + li * hr + bu_im[j, rows, :])
            bu_re[j, rows, :] = hr
            bu_im[j, rows, :] = hi
        h_re[:, cols] = hr
        h_im[:, cols] = hi
        part = (_dot_nt(bu_re[j].astype(BF16), c_re_ref[j])
                + _dot_nt(bu_im[j].astype(BF16), c_im_ref[j]))
        if j % BLOCKS_PER_LANE_TILE == 0:
            y_parts.append(part)
        else:
            y_parts[-1] = y_parts[-1] + part
    y = jnp.concatenate(y_parts, axis=-1)
    fin_ref[:, :N_STATES] = h_re[...]
    fin_ref[:, N_STATES:] = h_im[...]
    if finish:
        y = y + scan_order(yf_ref) + d_ref[...] * u
        z = jax.nn.gelu(y)
        y = (z * jax.nn.sigmoid(_dot(z.astype(BF16), w_glu_ref[...]))).astype(BF16)
    if reverse:
        for s in range(steps):
            o_ref[(steps - 1 - s) * nb:(steps - s) * nb, :] = y[s * nb:(s + 1) * nb, :]
    else:
        o_ref[...] = y


def _s5_rider_body(*refs, n_in, s5_body, rider_prologue, rider_body):
    s5_in, rider_in = refs[:n_in], refs[n_in:n_in + 6]
    outs, scratch = refs[n_in + 6:n_in + 9], refs[n_in + 9:]
    rider_prologue(*rider_in, outs[2], scratch[4])
    s5_body(*s5_in, *outs[:2], *scratch[:4])
    rider_body(*rider_in, outs[2], scratch[4])


def _s5_call(u, init, lb_re, lb_im, w_re, w_im, c_re, c_im, finish_args, *, tile, reverse, rider=None):
    rows = u.shape[0]
    nb = init.shape[0]
    n_tiles = rows // tile
    finish = finish_args is not None
    row_map = (lambda i: (n_tiles - 1 - i, 0)) if reverse else (lambda i: (i, 0))
    row_spec = pl.BlockSpec((tile, SSM_WIDTH), row_map)
    in_specs = [row_spec, _const_spec((nb, 2 * N_STATES)),
                _const_spec((1, N_STATES)), _const_spec((1, N_STATES)),
                _const_spec((SCAN_BLOCKS, LANES, SCAN_COLS)), _const_spec((SCAN_BLOCKS, LANES, SCAN_COLS)),
                _const_spec((SCAN_BLOCKS, LANES, SCAN_COLS)), _const_spec((SCAN_BLOCKS, LANES, SCAN_COLS))]
    args = [u, init, lb_re, lb_im, w_re, w_im, c_re, c_im]
    if finish:
        in_specs += [row_spec, _const_spec((1, SSM_WIDTH)), _const_spec((SSM_WIDTH, SSM_WIDTH))]
        args += list(finish_args)
    body = functools.partial(_s5_body, reverse=reverse, steps=tile // nb, finish=finish)
    out_specs = [row_spec, pl.BlockSpec((nb, 2 * N_STATES), lambda i: (0, 0))]
    out_shape = [jax.ShapeDtypeStruct((rows, SSM_WIDTH), BF16 if finish else F32),
                 jax.ShapeDtypeStruct((nb, 2 * N_STATES), F32)]
    scratch = [pltpu.VMEM((SCAN_BLOCKS, tile, SCAN_COLS), F32),
               pltpu.VMEM((SCAN_BLOCKS, tile, SCAN_COLS), F32),
               pltpu.VMEM((nb, N_STATES), F32), pltpu.VMEM((nb, N_STATES), F32)]
    if rider is not None:
        assert rider["steps"] == n_tiles, "one rider unit per scan tile"
        body = functools.partial(_s5_rider_body, n_in=len(args), s5_body=body,
                                 rider_prologue=rider["prologue"], rider_body=rider["body"])
        in_specs = in_specs + rider["in_specs"]
        args = args + rider["args"]
        out_specs.append(rider["out_spec"])
        out_shape.append(rider["out_shape"])
        scratch.append(rider["scratch"])
    return pl.pallas_call(
        body,
        grid=(n_tiles,),
        in_specs=in_specs,
        out_specs=out_specs,
        out_shape=out_shape,
        scratch_shapes=scratch,
        compiler_params=_params(1),
        name="s5_bwd" if reverse else "s5_fwd",
    )(*args)


NA_Q = NA_Q_ROWS * GRID_W
NA_K = NA_K_ROWS * GRID_W
NA_KEYS = NA_K + NA_META_PAD
NA_RING = LANES


def _block_key_row(blk, rows):
    return min(max(blk * NA_Q_ROWS - WIN_H // 2, 0), rows - NA_K_ROWS)


def _na_assemble_bias(ring_ref, bias, n_blocks, rows):
    kh = min(WIN_H, rows)
    masked = jnp.full((GRID_W, GRID_W), MASK_VALUE, F32)
    meta = jnp.where(lax.broadcasted_iota(jnp.int32, (GRID_W, GRID_W), 1) < N_META, 0.0, MASK_VALUE)
    cq = lax.broadcasted_iota(jnp.int32, (GRID_W, GRID_W), 0)
    ck = lax.broadcasted_iota(jnp.int32, (GRID_W, GRID_W), 1)
    col_start = jnp.clip(cq - WIN_W // 2, 0, GRID_W - WIN_W)
    col_valid = (ck >= col_start) & (ck < col_start + WIN_W)
    col_tiles = {}

    def col_tile(head, off):
        if (head, off) not in col_tiles:
            ring = jnp.broadcast_to(ring_ref[head, off:off + 1, :], (GRID_W, NA_RING))
            toeplitz = pltpu.roll(ring, 0, 1, stride=1, stride_axis=0)[:, :GRID_W]
            col_tiles[head, off] = jnp.where(col_valid, toeplitz, MASK_VALUE)
        return col_tiles[head, off]

    for kind, blk in enumerate((0, 1, n_blocks - 1)):
        for head in range(2):
            for qr in range(NA_Q_ROWS):
                r = blk * NA_Q_ROWS + qr
                row_start = min(max(r - kh // 2, 0), rows - kh)
                tiles = []
                for slot in range(NA_K_ROWS):
                    kr = _block_key_row(blk, rows) + slot
                    valid = row_start <= kr < row_start + kh
                    tiles.append(col_tile(head, kr - r + WIN_H - 1) if valid else masked)
                tiles.append(meta)
                for c in range(0, len(tiles), 2):
                    bias[kind, head, qr * GRID_W:(qr + 1) * GRID_W, c * GRID_W:(c + 2) * GRID_W] = (
                        jnp.concatenate(tiles[c:c + 2], axis=1))


def _na_prologue(q_ref, k_ref, v_ref, km_ref, vm_ref, t_ref, o_ref, bias, *, n_blocks, rows, batch):
    @pl.when(pl.program_id(0) % batch == 0)
    def _():
        _na_assemble_bias(t_ref, bias, n_blocks, rows)


def _na_body(q_ref, k_ref, v_ref, km_ref, vm_ref, t_ref, o_ref, bias, *, n_blocks, rows, batch):
    first_head = lax.broadcasted_iota(jnp.int32, (1, LANES), 1) < NA_HEAD_DIM
    km = km_ref[...]
    vm = vm_ref[...]
    for i in range(n_blocks):
        k0 = _block_key_row(i, rows) * GRID_W
        q0 = i * NA_Q
        kind = 0 if i == 0 else 2 if i == n_blocks - 1 else 1
        q = q_ref[q0:q0 + NA_Q, :]
        zero = jnp.zeros_like(q)
        qs = jnp.concatenate([jnp.where(first_head, q, zero), jnp.where(first_head, zero, q)], axis=0)
        keys = jnp.concatenate([k_ref[k0:k0 + NA_K, :], km], axis=0)
        vals = jnp.concatenate([v_ref[k0:k0 + NA_K, :], vm], axis=0)
        s = _dot_nt(qs, keys) + bias[kind].reshape(2 * NA_Q, NA_KEYS)
        m = jnp.max(s, axis=-1, keepdims=True)
        p = jnp.exp(s - m)
        l = jnp.sum(p, axis=-1, keepdims=True)
        o = _dot(p.astype(BF16), vals) / l
        o_ref[q0:q0 + NA_Q, :] = jnp.where(first_head, o[:NA_Q], o[NA_Q:]).astype(BF16)


def _na_rider(q, k, v, km, vm, col_bias, *, first_pair, n_pairs, batch, seq):
    rows = seq // GRID_W
    assert NA_META_PAD == GRID_W and (NA_K_ROWS + 1) % 2 == 0
    pair = lambda i: first_pair + i // batch
    seq_spec = pl.BlockSpec((seq, LANES), lambda i: (i % batch, pair(i)))
    meta_spec = pl.BlockSpec((NA_META_PAD, LANES), lambda i: (0, pair(i)))
    table_spec = pl.BlockSpec((2, 2 * WIN_H - 1, NA_RING), lambda i: (pair(i), 0, 0))
    return dict(
        steps=n_pairs * batch,
        prologue=functools.partial(_na_prologue, n_blocks=rows // NA_Q_ROWS, rows=rows, batch=batch),
        body=functools.partial(_na_body, n_blocks=rows // NA_Q_ROWS, rows=rows, batch=batch),
        args=[q, k, v, km, vm, col_bias],
        in_specs=[seq_spec, seq_spec, seq_spec, meta_spec, meta_spec, table_spec],
        out_spec=pl.BlockSpec((seq, LANES), lambda i: (i % batch, i // batch)),
        out_shape=jax.ShapeDtypeStruct((batch * seq, n_pairs * LANES), BF16),
        scratch=pltpu.VMEM((3, 2, NA_Q, NA_KEYS), F32))


def _na_col_ring(rpb):
    n_head, n_off, _ = rpb.shape
    ring = jnp.zeros((n_head, n_off, NA_RING), F32)
    ring = ring.at[..., :WIN_W].set(rpb[..., WIN_W - 1:].astype(F32))
    return ring.at[..., NA_RING - (WIN_W - 1):].set(rpb[..., :WIN_W - 1].astype(F32))


def _merge_body(h_ref, z_ref, a_lo_ref, a_hi_ref, g_ref, w_gate_ref, w_ssm_ref, w_na_ref, w_out_ref, o_ref):
    h = h_ref[...]
    hn = _rms_norm(h, g_ref[...]).astype(BF16)
    gates = _dot(hn, w_gate_ref[...])
    y_ssm = _dot(z_ref[...], w_ssm_ref[...])
    y_na = (_dot(a_lo_ref[...], w_na_ref[:NA_WIDTH // 2, :])
            + _dot(a_hi_ref[...], w_na_ref[NA_WIDTH // 2:, :]))
    merged = (jax.nn.sigmoid(gates[:, :D_MODEL]) * y_ssm
              + jax.nn.sigmoid(gates[:, D_MODEL:]) * y_na)
    o_ref[...] = h + _dot(merged.astype(BF16), w_out_ref[...])


def _merge_call(h1, z, a_lo, a_hi, g, w_gate, w_ssm, w_na, w_out):
    rows = h1.shape[0]
    row_spec = lambda width: pl.BlockSpec((WIDE_ROW_TILE, width), lambda i: (i, 0))
    return pl.pallas_call(
        _merge_body,
        grid=(rows // WIDE_ROW_TILE,),
        in_specs=[row_spec(D_MODEL), row_spec(SSM_WIDTH), row_spec(NA_WIDTH // 2), row_spec(NA_WIDTH // 2),
                  _const_spec((1, D_MODEL)), _const_spec((D_MODEL, 2 * D_MODEL)),
                  _const_spec((SSM_WIDTH, D_MODEL)), _const_spec((NA_WIDTH, D_MODEL)),
                  _const_spec((D_MODEL, D_MODEL))],
        out_specs=row_spec(D_MODEL),
        out_shape=jax.ShapeDtypeStruct((rows, D_MODEL), F32),
        compiler_params=_params(1),
        name="merge",
    )(h1, z, a_lo, a_hi, g, w_gate, w_ssm, w_na, w_out)


def kernel(x, meta_tokens, norm_ffn1, w_ffn1_in, w_ffn1_out, norm_mix, w_in, ssm_a_re_fwd, ssm_a_im_fwd, ssm_log_dt_fwd, ssm_a_re_bwd, ssm_a_im_bwd, ssm_log_dt_bwd, ssm_b_re, ssm_b_im, ssm_c_re, ssm_c_im, ssm_d, w_glu, na_rpb, w_branch_ssm, w_branch_na, w_out, norm_ffn2, w_ffn2_in, w_ffn2_out, norm_final):
    batch, seq, d_model = x.shape
    assert d_model == D_MODEL and norm_ffn1.shape[0] == 1, "single-layer block of width 1024"
    assert seq % ROW_TILE == 0 and seq % (GRID_W * NA_Q_ROWS) == 0 and batch % 8 == 0
    rows = seq // GRID_W
    assert rows >= NA_K_ROWS + NA_Q_ROWS

    row = lambda a: a.reshape(1, -1).astype(F32)
    g1, gm, g2, gf = row(norm_ffn1[0]), row(norm_mix[0]), row(norm_ffn2[0]), row(norm_final)
    w1_in, w1_out = w_ffn1_in[0].astype(BF16), w_ffn1_out[0].astype(BF16)
    w_uqkv = w_in[0][:, :UQKV_WIDTH].astype(BF16)
    whole = lambda a: (a[0].astype(F32), 0, a.shape[-1])
    later_weights = [whole(w_ffn2_in), whole(w_ffn2_out), (w_in[0].astype(F32), 1, UQKV_WIDTH),
                     whole(w_branch_ssm), whole(w_branch_na), whole(w_out), whole(w_glu)]
    assert w_in.shape[-1] == 2 * UQKV_WIDTH, "the two gates are the second column block of w_in"

    n_rows = batch * seq
    tok_spec = pl.BlockSpec((FFN2_ROW_TILE, D_MODEL), lambda i: (i, 0))
    (h1, u, q, k, v, u_meta, _, k_meta, v_meta,
     w2_in, w2_out, w_gate, w_ssm, w_na, w_mix_out, w_glu_bf) = _ffn_proj_call(
        x.reshape(n_rows, D_MODEL), meta_tokens.astype(F32), g1, w1_in, w1_out, gm, w_uqkv,
        later_weights)
    u = u.reshape(batch, seq, SSM_WIDTH).transpose(1, 0, 2).reshape(n_rows, SSM_WIDTH)

    states = lambda a: a[0].reshape(1, N_STATES).astype(F32)
    per_state = lambda a: jnp.repeat(a[0].astype(F32), SSM_STATE).reshape(1, N_STATES)
    b_re_t = ssm_b_re[0].reshape(N_STATES, SSM_GROUP).T.astype(F32)
    b_im_t = ssm_b_im[0].reshape(N_STATES, SSM_GROUP).T.astype(F32)
    lbf_re, lbf_im, wf_re, wf_im = _disc_call(
        states(ssm_a_re_fwd), states(ssm_a_im_fwd), per_state(ssm_log_dt_fwd), b_re_t, b_im_t)
    lbb_re, lbb_im, wb_re, wb_im = _disc_call(
        states(ssm_a_re_bwd), states(ssm_a_im_bwd), per_state(ssm_log_dt_bwd), b_re_t, b_im_t)
    c_re, c_im = _cslab_call(ssm_c_re[0].reshape(SSM_WIDTH, SSM_STATE).astype(F32),
                             ssm_c_im[0].reshape(SSM_WIDTH, SSM_STATE).astype(F32))

    zero_state = jnp.zeros((batch, 2 * N_STATES), F32)
    u_meta_rows = jnp.repeat(u_meta, batch, axis=0)
    _, state_meta = _s5_call(u_meta_rows, zero_state, lbf_re, lbf_im, wf_re, wf_im, c_re, c_im,
                             None, tile=N_META * batch, reverse=False)
    pad_meta = lambda a: jnp.pad(a, ((0, NA_META_PAD - N_META), (0, 0)))
    half = NA_HEADS // 4
    riders = [_na_rider(q, k, v, pad_meta(k_meta), pad_meta(v_meta), _na_col_ring(na_rpb[0]),
                        first_pair=p0, n_pairs=half, batch=batch, seq=seq) for p0 in (0, half)]
    y_fwd, _, att_lo = _s5_call(u, state_meta, lbf_re, lbf_im, wf_re, wf_im, c_re, c_im,
                                None, tile=SCAN_TILE, reverse=False, rider=riders[0])
    z, _, att_hi = _s5_call(u, zero_state, lbb_re, lbb_im, wb_re, wb_im, c_re, c_im,
                            (y_fwd, row(ssm_d[0]), w_glu_bf), tile=SCAN_TILE,
                            reverse=True, rider=riders[1])

    z = z.reshape(seq, batch, SSM_WIDTH).transpose(1, 0, 2).reshape(n_rows, SSM_WIDTH)
    h2 = _merge_call(h1, z, att_lo, att_hi, gm, w_gate,
                     w_ssm, w_na, w_mix_out)

    out = _ffn_call(h2, g2, w2_in, w2_out, gf, grid=(n_rows // FFN2_ROW_TILE,), x_spec=tok_spec,
                    o_spec=tok_spec, out_shape=(n_rows, D_MODEL), final_norm=True)
    return out.reshape(batch, seq, D_MODEL)
```

```python
import functools
import math

import jax
import jax.numpy as jnp
from jax import lax
from jax.experimental import pallas as pl
from jax.experimental.pallas import tpu as pltpu

D_MODEL = 1024
N_META = 16
GRID_W = 64
D_FF = 2816
SSM_WIDTH = 512
SSM_GROUP = 16
SSM_GROUPS = 32
SSM_STATE = 64
N_STATES = SSM_GROUPS * SSM_STATE
NA_HEAD_DIM = 64
NA_WIDTH = 512
NA_HEADS = 8
WIN_H = 8
WIN_W = 16
UQKV_WIDTH = SSM_WIDTH + 3 * NA_WIDTH
RMS_EPS = 1e-6
A_RE_MAX = -1e-4
MASK_VALUE = -1e30

LANES = 128
BF16_ROWS = 16
VMEM_LIMIT = 56 * 1024 * 1024

ROW_TILE = 512
FFN2_ROW_TILE = 1024
FFN2_HIDDEN_CHUNKS = ((0, 1536), (1536, D_FF))
WIDE_ROW_TILE = 1024
SCAN_COLS = 256
SCAN_TILE = 1024
NA_Q_ROWS = 4
NA_K_ROWS = NA_Q_ROWS + WIN_H - 1
NA_META_PAD = GRID_W

BF16 = jnp.bfloat16
F32 = jnp.float32


def _const_spec(shape):
    zeros = (0,) * len(shape)
    return pl.BlockSpec(shape, lambda *_: zeros, pipeline_mode=pl.Buffered(1))


def _params(n_grid_dims):
    return pltpu.CompilerParams(
        dimension_semantics=("arbitrary",) * n_grid_dims,
        vmem_limit_bytes=VMEM_LIMIT)


def _rms_scale(x):
    return lax.rsqrt(jnp.mean(x * x, axis=-1, keepdims=True) + RMS_EPS)


def _rms_norm(x, g):
    return x * _rms_scale(x) * g


def _dot(a, b):
    return jnp.dot(a, b, preferred_element_type=F32)


def _dot_nt(a, b):
    return lax.dot_general(a, b, (((1,), (1,)), ((), ())), preferred_element_type=F32)


def _ffn_rows(x, g_ref, w_in_ref, w_out_ref, hidden_chunks=((0, D_FF),)):
    r = _rms_scale(x)
    xg = (x * g_ref[...]).astype(BF16)
    acc = None
    for lo, hi in hidden_chunks:
        gate = _dot(xg, w_in_ref[:, lo:hi]) * r
        up = _dot(xg, w_in_ref[:, D_FF + lo:D_FF + hi]) * r
        act = (jax.nn.silu(gate) * up).astype(BF16)
        part = _dot(act, w_out_ref[lo:hi, :])
        acc = part if acc is None else acc + part
    return x + 0.5 * acc


def _ffn_body(x_ref, g_ref, w_in_ref, w_out_ref, gf_ref, o_ref, *, final_norm):
    y = _ffn_rows(x_ref[...], g_ref, w_in_ref, w_out_ref, FFN2_HIDDEN_CHUNKS)
    if final_norm:
        y = _rms_norm(y, gf_ref[...])
    o_ref[...] = y


def _proj_rows(h, g_ref, w_ref, u_ref, q_ref, k_ref, v_ref):
    p = _dot((h * g_ref[...]).astype(BF16), w_ref[...]) * _rms_scale(h)
    u_ref[...] = p[:, :SSM_WIDTH]
    q_ref[...] = (p[:, SSM_WIDTH:SSM_WIDTH + NA_WIDTH] * (NA_HEAD_DIM ** -0.5)).astype(BF16)
    k_ref[...] = p[:, SSM_WIDTH + NA_WIDTH:SSM_WIDTH + 2 * NA_WIDTH].astype(BF16)
    v_ref[...] = p[:, SSM_WIDTH + 2 * NA_WIDTH:].astype(BF16)


N_FFN_PROJ_IN = 7
N_FFN_PROJ_OUT = 9


def _ffn_proj_body(*refs):
    x_ref, xm_ref, g_ref, w_in_ref, w_out_ref, gp_ref, wp_ref = refs[:N_FFN_PROJ_IN]
    n_cast = (len(refs) - N_FFN_PROJ_IN - N_FFN_PROJ_OUT) // 2
    cast_in = refs[N_FFN_PROJ_IN:N_FFN_PROJ_IN + n_cast]
    outs = refs[N_FFN_PROJ_IN + n_cast:]
    h_ref, u_ref, q_ref, k_ref, v_ref, um_ref, qm_ref, km_ref, vm_ref = outs[:N_FFN_PROJ_OUT]
    cast_out = outs[N_FFN_PROJ_OUT:]

    @pl.when(pl.program_id(0) == 0)
    def _():
        hm = _ffn_rows(xm_ref[...], g_ref, w_in_ref, w_out_ref)
        _proj_rows(hm, gp_ref, wp_ref, um_ref, qm_ref, km_ref, vm_ref)

    h = _ffn_rows(x_ref[...], g_ref, w_in_ref, w_out_ref)
    h_ref[...] = h
    _proj_rows(h, gp_ref, wp_ref, u_ref, q_ref, k_ref, v_ref)
    for src, dst in zip(cast_in, cast_out):
        dst[...] = src[...].astype(BF16)


def _ffn_proj_call(x, x_meta, g, w_in, w_out, g_proj, w_proj, to_cast):
    rows = x.shape[0]
    steps = rows // ROW_TILE
    row_spec = lambda width: pl.BlockSpec((ROW_TILE, width), lambda i: (i, 0))
    meta_spec = lambda width: pl.BlockSpec((N_META, width), lambda i: (0, 0))
    wide = lambda n, dt: jax.ShapeDtypeStruct((n, SSM_WIDTH), dt)

    def cast_specs(a, col_block, n_cols):
        n = a.shape[0]
        chunk = min(c for c in range(BF16_ROWS, n + 1, BF16_ROWS) if n % c == 0 and n // c <= steps)
        last = n // chunk - 1
        return (pl.BlockSpec((chunk, n_cols), lambda i: (jnp.minimum(i, last), col_block)),
                pl.BlockSpec((chunk, n_cols), lambda i: (jnp.minimum(i, last), 0)))

    cast_in_specs, cast_out_specs = zip(*[cast_specs(*c) for c in to_cast])
    cast_shapes = [jax.ShapeDtypeStruct((a.shape[0], n_cols), BF16) for a, _, n_cols in to_cast]
    return pl.pallas_call(
        _ffn_proj_body,
        grid=(steps,),
        in_specs=[row_spec(D_MODEL), meta_spec(D_MODEL), _const_spec((1, D_MODEL)),
                  _const_spec((D_MODEL, 2 * D_FF)), _const_spec((D_FF, D_MODEL)),
                  _const_spec((1, D_MODEL)), _const_spec((D_MODEL, UQKV_WIDTH))] + list(cast_in_specs),
        out_specs=[row_spec(D_MODEL)] + [row_spec(SSM_WIDTH)] * 4 + [meta_spec(SSM_WIDTH)] * 4
        + list(cast_out_specs),
        out_shape=[jax.ShapeDtypeStruct((rows, D_MODEL), F32), wide(rows, F32)] + [wide(rows, BF16)] * 3
        + [wide(N_META, F32)] + [wide(N_META, BF16)] * 3 + cast_shapes,
        compiler_params=_params(1),
        name="ffn_proj",
    )(x, x_meta, g, w_in, w_out, g_proj, w_proj, *[a for a, _, _ in to_cast])


def _ffn_call(x, g, w_in, w_out, gf, *, grid, x_spec, o_spec, out_shape, final_norm):
    return pl.pallas_call(
        functools.partial(_ffn_body, final_norm=final_norm),
        grid=grid,
        in_specs=[x_spec, _const_spec((1, D_MODEL)), _const_spec((D_MODEL, 2 * D_FF)),
                  _const_spec((D_FF, D_MODEL)), _const_spec((1, D_MODEL))],
        out_specs=o_spec,
        out_shape=jax.ShapeDtypeStruct(out_shape, F32),
        compiler_params=_params(len(grid)),
        name="ffn",
    )(x, g, w_in, w_out, gf)


def _same_group(shape, ch_axis, ch0, state0):
    ch = lax.broadcasted_iota(jnp.int32, shape, ch_axis) + ch0
    st = lax.broadcasted_iota(jnp.int32, shape, 1 - ch_axis) + state0
    return (ch >> int(math.log2(SSM_GROUP))) == (st >> int(math.log2(SSM_STATE)))


def _disc_body(a_re_ref, a_im_ref, log_dt_ref, b_re_ref, b_im_ref,
               lb_re_ref, lb_im_ref, w_re_ref, w_im_ref):
    dt = jnp.exp(log_dt_ref[...])
    a_re = jnp.minimum(a_re_ref[...], A_RE_MAX)
    a_im = a_im_ref[...]
    mag = jnp.exp(a_re * dt)
    lb_re = mag * jnp.cos(a_im * dt)
    lb_im = mag * jnp.sin(a_im * dt)
    den = a_re * a_re + a_im * a_im
    nr = lb_re - 1.0
    ni = lb_im
    f_re = (nr * a_re + ni * a_im) / den
    f_im = (ni * a_re - nr * a_im) / den
    lb_re_ref[...] = lb_re
    lb_im_ref[...] = lb_im
    bb_re = f_re * b_re_ref[...] - f_im * b_im_ref[...]
    bb_im = f_re * b_im_ref[...] + f_im * b_re_ref[...]
    for j in range(SCAN_BLOCKS):
        same = _same_group((LANES, SCAN_COLS), 0, _lane_tile(j) * LANES, j * SCAN_COLS)
        for bb, w_ref in ((bb_re, w_re_ref), (bb_im, w_im_ref)):
            rows = jnp.concatenate([bb[:, j * SCAN_COLS:(j + 1) * SCAN_COLS]] * (LANES // SSM_GROUP), axis=0)
            w_ref[j] = jnp.where(same, rows, 0.0).astype(BF16)


def _cslab_body(c_re_ref, c_im_ref, ct_re_ref, ct_im_ref):
    for j in range(SCAN_BLOCKS):
        same = _same_group((LANES, SCAN_COLS), 0, _lane_tile(j) * LANES, j * SCAN_COLS)
        for c_ref, ct_ref, sign in ((c_re_ref, ct_re_ref, 1.0), (c_im_ref, ct_im_ref, -1.0)):
            c = c_ref[_lane_tile(j) * LANES:(_lane_tile(j) + 1) * LANES, :] * sign
            cols = jnp.concatenate([c] * (SCAN_COLS // SSM_STATE), axis=1)
            ct_ref[j] = jnp.where(same, cols, 0.0).astype(BF16)


SCAN_BLOCKS = N_STATES // SCAN_COLS
BLOCKS_PER_LANE_TILE = LANES * SSM_STATE // (SSM_GROUP * SCAN_COLS)


def _lane_tile(j):
    return j // BLOCKS_PER_LANE_TILE


def _s5_body(*refs, reverse, steps, finish):
    if finish:
        (u_ref, init_ref, lb_re_ref, lb_im_ref, w_re_ref, w_im_ref, c_re_ref, c_im_ref,
         yf_ref, d_ref, w_glu_ref, o_ref, fin_ref, bu_re, bu_im, h_re, h_im) = refs
    else:
        (u_ref, init_ref, lb_re_ref, lb_im_ref, w_re_ref, w_im_ref, c_re_ref, c_im_ref,
         o_ref, fin_ref, bu_re, bu_im, h_re, h_im) = refs
    nb = init_ref.shape[0]

    @pl.when(pl.program_id(0) == 0)
    def _():
        h_re[...] = init_ref[:, :N_STATES]
        h_im[...] = init_ref[:, N_STATES:]

    def scan_order(ref):
        if not reverse:
            return ref[...]
        return jnp.concatenate(
            [ref[(steps - 1 - s) * nb:(steps - s) * nb, :] for s in range(steps)], axis=0)

    u = scan_order(u_ref)
    ub = u.astype(BF16)
    y_parts = []
    for j in range(SCAN_BLOCKS):
        cols = slice(j * SCAN_COLS, (j + 1) * SCAN_COLS)
        us = ub[:, _lane_tile(j) * LANES:(_lane_tile(j) + 1) * LANES]
        bu_re[j] = _dot(us, w_re_ref[j])
        bu_im[j] = _dot(us, w_im_ref[j])
        lr = jnp.broadcast_to(lb_re_ref[:, cols], (nb, SCAN_COLS))
        li = jnp.broadcast_to(lb_im_ref[:, cols], (nb, SCAN_COLS))
        hr, hi = h_re[:, cols], h_im[:, cols]
        for s in range(steps):
            rows = slice(s * nb, (s + 1) * nb)
            hr, hi = (lr * hr - li * hi + bu_re[j, rows, :],
                      lr * hi + li * hr + bu_im[j, rows, :])
            bu_re[j, rows, :] = hr
            bu_im[j, rows, :] = hi
        h_re[:, cols] = hr
        h_im[:, cols] = hi
        part = (_dot_nt(bu_re[j].astype(BF16), c_re_ref[j])
                + _dot_nt(bu_im[j].astype(BF16), c_im_ref[j]))
        if j % BLOCKS_PER_LANE_TILE == 0:
            y_parts.append(part)
        else:
            y_parts[-1] = y_parts[-1] + part
    y = jnp.concatenate(y_parts, axis=-1)
    fin_ref[:, :N_STATES] = h_re[...]
    fin_ref[:, N_STATES:] = h_im[...]
    if finish:
        y = y + scan_order(yf_ref) + d_ref[...] * u
        z = jax.nn.gelu(y)
        y = (z * jax.nn.sigmoid(_dot(z.astype(BF16), w_glu_ref[...]))).astype(BF16)
    if reverse:
        for s in range(steps):
            o_ref[(steps - 1 - s) * nb:(steps - s) * nb, :] = y[s * nb:(s + 1) * nb, :]
    else:
        o_ref[...] = y


def _s5_rider_body(*refs, n_in, s5_body, rider_prologue, rider_body):
    s5_in, rider_in = refs[:n_in], refs[n_in:n_in + 6]
    outs, scratch = refs[n_in + 6:n_in + 9], refs[n_in + 9:]
    rider_prologue(*rider_in, outs[2], scratch[4])
    s5_body(*s5_in, *outs[:2], *scratch[:4])
    rider_body(*rider_in, outs[2], scratch[4])


def _s5_call(u, init, lb_re, lb_im, w_re, w_im, c_re, c_im, finish_args, *, tile, reverse, rider=None):
    rows = u.shape[0]
    nb = init.shape[0]
    n_tiles = rows // tile
    finish = finish_args is not None
    row_map = (lambda i: (n_tiles - 1 - i, 0)) if reverse else (lambda i: (i, 0))
    row_spec = pl.BlockSpec((tile, SSM_WIDTH), row_map)
    in_specs = [row_spec, _const_spec((nb, 2 * N_STATES)),
                _const_spec((1, N_STATES)), _const_spec((1, N_STATES)),
                _const_spec((SCAN_BLOCKS, LANES, SCAN_COLS)), _const_spec((SCAN_BLOCKS, LANES, SCAN_COLS)),
                _const_spec((SCAN_BLOCKS, LANES, SCAN_COLS)), _const_spec((SCAN_BLOCKS, LANES, SCAN_COLS))]
    args = [u, init, lb_re, lb_im, w_re, w_im, c_re, c_im]
    if finish:
        in_specs += [row_spec, _const_spec((1, SSM_WIDTH)), _const_spec((SSM_WIDTH, SSM_WIDTH))]
        args += list(finish_args)
    body = functools.partial(_s5_body, reverse=reverse, steps=tile // nb, finish=finish)
    out_specs = [row_spec, pl.BlockSpec((nb, 2 * N_STATES), lambda i: (0, 0))]
    out_shape = [jax.ShapeDtypeStruct((rows, SSM_WIDTH), BF16 if finish else F32),
                 jax.ShapeDtypeStruct((nb, 2 * N_STATES), F32)]
    scratch = [pltpu.VMEM((SCAN_BLOCKS, tile, SCAN_COLS), F32),
               pltpu.VMEM((SCAN_BLOCKS, tile, SCAN_COLS), F32),
               pltpu.VMEM((nb, N_STATES), F32), pltpu.VMEM((nb, N_STATES), F32)]
    if rider is not None:
        assert rider["steps"] == n_tiles, "one rider unit per scan tile"
        body = functools.partial(_s5_rider_body, n_in=len(args), s5_body=body,
                                 rider_prologue=rider["prologue"], rider_body=rider["body"])
        in_specs = in_specs + rider["in_specs"]
        args = args + rider["args"]
        out_specs.append(rider["out_spec"])
        out_shape.append(rider["out_shape"])
        scratch.append(rider["scratch"])
    return pl.pallas_call(
        body,
        grid=(n_tiles,),
        in_specs=in_specs,
        out_specs=out_specs,
        out_shape=out_shape,
        scratch_shapes=scratch,
        compiler_params=_params(1),
        name="s5_bwd" if reverse else "s5_fwd",
    )(*args)


def _s5_setup_body(af_re_ref, af_im_ref, dtf_ref, ab_re_ref, ab_im_ref, dtb_ref, b_re_ref, b_im_ref,
                   c_re_ref, c_im_ref, u_ref, init_ref,
                   lbf_re_ref, lbf_im_ref, wf_re_ref, wf_im_ref, lbb_re_ref, lbb_im_ref, wb_re_ref, wb_im_ref,
                   ct_re_ref, ct_im_ref, y_ref, fin_ref, bu_re, bu_im, h_re, h_im, *, steps):
    _disc_body(af_re_ref, af_im_ref, dtf_ref, b_re_ref, b_im_ref, lbf_re_ref, lbf_im_ref, wf_re_ref, wf_im_ref)
    _disc_body(ab_re_ref, ab_im_ref, dtb_ref, b_re_ref, b_im_ref, lbb_re_ref, lbb_im_ref, wb_re_ref, wb_im_ref)
    _cslab_body(c_re_ref, c_im_ref, ct_re_ref, ct_im_ref)
    _s5_body(u_ref, init_ref, lbf_re_ref, lbf_im_ref, wf_re_ref, wf_im_ref, ct_re_ref, ct_im_ref,
             y_ref, fin_ref, bu_re, bu_im, h_re, h_im, reverse=False, steps=steps, finish=False)


def _s5_setup_call(disc_fwd, disc_bwd, b_re_t, b_im_t, c_re, c_im, u_meta_rows, init):
    rows, nb = u_meta_rows.shape[0], init.shape[0]
    args = [*disc_fwd, *disc_bwd, b_re_t, b_im_t, c_re, c_im, u_meta_rows, init]
    row = jax.ShapeDtypeStruct((1, N_STATES), F32)
    slabs = jax.ShapeDtypeStruct((SCAN_BLOCKS, LANES, SCAN_COLS), BF16)
    out_shape = [row, row, slabs, slabs, row, row, slabs, slabs, slabs, slabs,
                 jax.ShapeDtypeStruct((rows, SSM_WIDTH), F32), jax.ShapeDtypeStruct((nb, 2 * N_STATES), F32)]
    whole = lambda s: pl.BlockSpec(s.shape, lambda i, n=len(s.shape): (0,) * n)
    outs = pl.pallas_call(
        functools.partial(_s5_setup_body, steps=rows // nb),
        grid=(1,),
        in_specs=[whole(a) for a in args],
        out_specs=[whole(s) for s in out_shape],
        out_shape=out_shape,
        scratch_shapes=[pltpu.VMEM((SCAN_BLOCKS, rows, SCAN_COLS), F32),
                        pltpu.VMEM((SCAN_BLOCKS, rows, SCAN_COLS), F32),
                        pltpu.VMEM((nb, N_STATES), F32), pltpu.VMEM((nb, N_STATES), F32)],
        compiler_params=_params(1),
        name="s5_setup",
    )(*args)
    return outs[:4], outs[4:8], outs[8:10], outs[11]


NA_Q = NA_Q_ROWS * GRID_W
NA_K = NA_K_ROWS * GRID_W
NA_KEYS = NA_K + NA_META_PAD
NA_RING = LANES


def _block_key_row(blk, rows):
    return min(max(blk * NA_Q_ROWS - WIN_H // 2, 0), rows - NA_K_ROWS)


def _na_assemble_bias(ring_ref, bias, n_blocks, rows):
    kh = min(WIN_H, rows)
    masked = jnp.full((GRID_W, GRID_W), MASK_VALUE, F32)
    meta = jnp.where(lax.broadcasted_iota(jnp.int32, (GRID_W, GRID_W), 1) < N_META, 0.0, MASK_VALUE)
    cq = lax.broadcasted_iota(jnp.int32, (GRID_W, GRID_W), 0)
    ck = lax.broadcasted_iota(jnp.int32, (GRID_W, GRID_W), 1)
    col_start = jnp.clip(cq - WIN_W // 2, 0, GRID_W - WIN_W)
    col_valid = (ck >= col_start) & (ck < col_start + WIN_W)
    col_tiles = {}

    def col_tile(head, off):
        if (head, off) not in col_tiles:
            ring = jnp.broadcast_to(ring_ref[head, off:off + 1, :], (GRID_W, NA_RING))
            toeplitz = pltpu.roll(ring, 0, 1, stride=1, stride_axis=0)[:, :GRID_W]
            col_tiles[head, off] = jnp.where(col_valid, toeplitz, MASK_VALUE)
        return col_tiles[head, off]

    for kind, blk in enumerate((0, 1, n_blocks - 1)):
        for head in range(2):
            for qr in range(NA_Q_ROWS):
                r = blk * NA_Q_ROWS + qr
                row_start = min(max(r - kh // 2, 0), rows - kh)
                tiles = []
                for slot in range(NA_K_ROWS):
                    kr = _block_key_row(blk, rows) + slot
                    valid = row_start <= kr < row_start + kh
                    tiles.append(col_tile(head, kr - r + WIN_H - 1) if valid else masked)
                tiles.append(meta)
                for c in range(0, len(tiles), 2):
                    bias[kind, head, qr * GRID_W:(qr + 1) * GRID_W, c * GRID_W:(c + 2) * GRID_W] = (
                        jnp.concatenate(tiles[c:c + 2], axis=1))


def _na_prologue(q_ref, k_ref, v_ref, km_ref, vm_ref, t_ref, o_ref, bias, *, n_blocks, rows, batch):
    @pl.when(pl.program_id(0) % batch == 0)
    def _():
        _na_assemble_bias(t_ref, bias, n_blocks, rows)


def _na_body(q_ref, k_ref, v_ref, km_ref, vm_ref, t_ref, o_ref, bias, *, n_blocks, rows, batch):
    first_head = lax.broadcasted_iota(jnp.int32, (1, LANES), 1) < NA_HEAD_DIM
    km = km_ref[...]
    vm = vm_ref[...]
    for i in range(n_blocks):
        k0 = _block_key_row(i, rows) * GRID_W
        q0 = i * NA_Q
        kind = 0 if i == 0 else 2 if i == n_blocks - 1 else 1
        q = q_ref[q0:q0 + NA_Q, :]
        zero = jnp.zeros_like(q)
        qs = jnp.concatenate([jnp.where(first_head, q, zero), jnp.where(first_head, zero, q)], axis=0)
        keys = jnp.concatenate([k_ref[k0:k0 + NA_K, :], km], axis=0)
        vals = jnp.concatenate([v_ref[k0:k0 + NA_K, :], vm], axis=0)
        s = _dot_nt(qs, keys) + bias[kind].reshape(2 * NA_Q, NA_KEYS)
        m = jnp.max(s, axis=-1, keepdims=True)
        p = jnp.exp(s - m)
        l = jnp.sum(p, axis=-1, keepdims=True)
        o = _dot(p.astype(BF16), vals) / l
        o_ref[q0:q0 + NA_Q, :] = jnp.where(first_head, o[:NA_Q], o[NA_Q:]).astype(BF16)


def _na_rider(q, k, v, km, vm, col_bias, *, first_pair, n_pairs, batch, seq):
    rows = seq // GRID_W
    assert NA_META_PAD == GRID_W and (NA_K_ROWS + 1) % 2 == 0
    pair = lambda i: first_pair + i // batch
    seq_spec = pl.BlockSpec((seq, LANES), lambda i: (i % batch, pair(i)))
    meta_spec = pl.BlockSpec((NA_META_PAD, LANES), lambda i: (0, pair(i)))
    table_spec = pl.BlockSpec((2, 2 * WIN_H - 1, NA_RING), lambda i: (pair(i), 0, 0))
    return dict(
        steps=n_pairs * batch,
        prologue=functools.partial(_na_prologue, n_blocks=rows // NA_Q_ROWS, rows=rows, batch=batch),
        body=functools.partial(_na_body, n_blocks=rows // NA_Q_ROWS, rows=rows, batch=batch),
        args=[q, k, v, km, vm, col_bias],
        in_specs=[seq_spec, seq_spec, seq_spec, meta_spec, meta_spec, table_spec],
        out_spec=pl.BlockSpec((seq, LANES), lambda i: (i % batch, i // batch)),
        out_shape=jax.ShapeDtypeStruct((batch * seq, n_pairs * LANES), BF16),
        scratch=pltpu.VMEM((3, 2, NA_Q, NA_KEYS), F32))


def _na_col_ring(rpb):
    n_head, n_off, _ = rpb.shape
    ring = jnp.zeros((n_head, n_off, NA_RING), F32)
    ring = ring.at[..., :WIN_W].set(rpb[..., WIN_W - 1:].astype(F32))
    return ring.at[..., NA_RING - (WIN_W - 1):].set(rpb[..., :WIN_W - 1].astype(F32))


def _merge_body(h_ref, z_ref, a_lo_ref, a_hi_ref, g_ref, w_gate_ref, w_ssm_ref, w_na_ref, w_out_ref, o_ref):
    h = h_ref[...]
    hn = _rms_norm(h, g_ref[...]).astype(BF16)
    gates = _dot(hn, w_gate_ref[...])
    y_ssm = _dot(z_ref[...], w_ssm_ref[...])
    y_na = (_dot(a_lo_ref[...], w_na_ref[:NA_WIDTH // 2, :])
            + _dot(a_hi_ref[...], w_na_ref[NA_WIDTH // 2:, :]))
    merged = (jax.nn.sigmoid(gates[:, :D_MODEL]) * y_ssm
              + jax.nn.sigmoid(gates[:, D_MODEL:]) * y_na)
    o_ref[...] = h + _dot(merged.astype(BF16), w_out_ref[...])


def _merge_call(h1, z, a_lo, a_hi, g, w_gate, w_ssm, w_na, w_out):
    rows = h1.shape[0]
    row_spec = lambda width: pl.BlockSpec((WIDE_ROW_TILE, width), lambda i: (i, 0))
    return pl.pallas_call(
        _merge_body,
        grid=(rows // WIDE_ROW_TILE,),
        in_specs=[row_spec(D_MODEL), row_spec(SSM_WIDTH), row_spec(NA_WIDTH // 2), row_spec(NA_WIDTH // 2),
                  _const_spec((1, D_MODEL)), _const_spec((D_MODEL, 2 * D_MODEL)),
                  _const_spec((SSM_WIDTH, D_MODEL)), _const_spec((NA_WIDTH, D_MODEL)),
                  _const_spec((D_MODEL, D_MODEL))],
        out_specs=row_spec(D_MODEL),
        out_shape=jax.ShapeDtypeStruct((rows, D_MODEL), F32),
        compiler_params=_params(1),
        name="merge",
    )(h1, z, a_lo, a_hi, g, w_gate, w_ssm, w_na, w_out)


def kernel(x, meta_tokens, norm_ffn1, w_ffn1_in, w_ffn1_out, norm_mix, w_in, ssm_a_re_fwd, ssm_a_im_fwd, ssm_log_dt_fwd, ssm_a_re_bwd, ssm_a_im_bwd, ssm_log_dt_bwd, ssm_b_re, ssm_b_im, ssm_c_re, ssm_c_im, ssm_d, w_glu, na_rpb, w_branch_ssm, w_branch_na, w_out, norm_ffn2, w_ffn2_in, w_ffn2_out, norm_final):
    batch, seq, d_model = x.shape
    assert d_model == D_MODEL and norm_ffn1.shape[0] == 1, "single-layer block of width 1024"
    assert seq % ROW_TILE == 0 and seq % (GRID_W * NA_Q_ROWS) == 0 and batch % 8 == 0
    rows = seq // GRID_W
    assert rows >= NA_K_ROWS + NA_Q_ROWS

    row = lambda a: a.reshape(1, -1).astype(F32)
    g1, gm, g2, gf = row(norm_ffn1[0]), row(norm_mix[0]), row(norm_ffn2[0]), row(norm_final)
    w1_in, w1_out = w_ffn1_in[0].astype(BF16), w_ffn1_out[0].astype(BF16)
    w_uqkv = w_in[0][:, :UQKV_WIDTH].astype(BF16)
    whole = lambda a: (a[0].astype(F32), 0, a.shape[-1])
    later_weights = [whole(w_ffn2_in), whole(w_ffn2_out), (w_in[0].astype(F32), 1, UQKV_WIDTH),
                     whole(w_branch_ssm), whole(w_branch_na), whole(w_out), whole(w_glu)]
    assert w_in.shape[-1] == 2 * UQKV_WIDTH, "the two gates are the second column block of w_in"

    n_rows = batch * seq
    tok_spec = pl.BlockSpec((FFN2_ROW_TILE, D_MODEL), lambda i: (i, 0))
    (h1, u, q, k, v, u_meta, _, k_meta, v_meta,
     w2_in, w2_out, w_gate, w_ssm, w_na, w_mix_out, w_glu_bf) = _ffn_proj_call(
        x.reshape(n_rows, D_MODEL), meta_tokens.astype(F32), g1, w1_in, w1_out, gm, w_uqkv,
        later_weights)
    u = u.reshape(batch, seq, SSM_WIDTH).transpose(1, 0, 2).reshape(n_rows, SSM_WIDTH)

    states = lambda a: a[0].reshape(1, N_STATES).astype(F32)
    per_state = lambda a: jnp.repeat(a[0].astype(F32), SSM_STATE).reshape(1, N_STATES)
    b_re_t = ssm_b_re[0].reshape(N_STATES, SSM_GROUP).T.astype(F32)
    b_im_t = ssm_b_im[0].reshape(N_STATES, SSM_GROUP).T.astype(F32)
    zero_state = jnp.zeros((batch, 2 * N_STATES), F32)
    u_meta_rows = jnp.repeat(u_meta, batch, axis=0)
    (lbf_re, lbf_im, wf_re, wf_im), (lbb_re, lbb_im, wb_re, wb_im), (c_re, c_im), state_meta = _s5_setup_call(
        (states(ssm_a_re_fwd), states(ssm_a_im_fwd), per_state(ssm_log_dt_fwd)),
        (states(ssm_a_re_bwd), states(ssm_a_im_bwd), per_state(ssm_log_dt_bwd)),
        b_re_t, b_im_t,
        ssm_c_re[0].reshape(SSM_WIDTH, SSM_STATE).astype(F32),
        ssm_c_im[0].reshape(SSM_WIDTH, SSM_STATE).astype(F32),
        u_meta_rows, zero_state)
    pad_meta = lambda a: jnp.pad(a, ((0, NA_META_PAD - N_META), (0, 0)))
    half = NA_HEADS // 4
    riders = [_na_rider(q, k, v, pad_meta(k_meta), pad_meta(v_meta), _na_col_ring(na_rpb[0]),
                        first_pair=p0, n_pairs=half, batch=batch, seq=seq) for p0 in (0, half)]
    y_fwd, _, att_lo = _s5_call(u, state_meta, lbf_re, lbf_im, wf_re, wf_im, c_re, c_im,
                                None, tile=SCAN_TILE, reverse=False, rider=riders[0])
    z, _, att_hi = _s5_call(u, zero_state, lbb_re, lbb_im, wb_re, wb_im, c_re, c_im,
                            (y_fwd, row(ssm_d[0]), w_glu_bf), tile=SCAN_TILE,
                            reverse=True, rider=riders[1])

    z = z.reshape(seq, batch, SSM_WIDTH).transpose(1, 0, 2).reshape(n_rows, SSM_WIDTH)
    h2 = _merge_call(h1, z, att_lo, att_hi, gm, w_gate,
                     w_ssm, w_na, w_mix_out)

    out = _ffn_call(h2, g2, w2_in, w2_out, gf, grid=(n_rows // FFN2_ROW_TILE,), x_spec=tok_spec,
                    o_spec=tok_spec, out_shape=(n_rows, D_MODEL), final_norm=True)
    return out.reshape(batch, seq, D_MODEL)
```

```python
import functools
import math

import jax
import jax.numpy as jnp
from jax import lax
from jax.experimental import pallas as pl
from jax.experimental.pallas import tpu as pltpu

D_MODEL = 1024
N_META = 16
GRID_W = 64
D_FF = 2816
SSM_WIDTH = 512
SSM_GROUP = 16
SSM_GROUPS = 32
SSM_STATE = 64
N_STATES = SSM_GROUPS * SSM_STATE
NA_HEAD_DIM = 64
NA_WIDTH = 512
NA_HEADS = 8
WIN_H = 8
WIN_W = 16
UQKV_WIDTH = SSM_WIDTH + 3 * NA_WIDTH
RMS_EPS = 1e-6
A_RE_MAX = -1e-4
MASK_VALUE = -1e30

LANES = 128
BF16_ROWS = 16
VMEM_LIMIT = 56 * 1024 * 1024

ROW_TILE = 512
FFN2_ROW_TILE = 1024
FFN2_HIDDEN_CHUNKS = ((0, 1536), (1536, D_FF))
WIDE_ROW_TILE = 1024
SCAN_COLS = 256
SCAN_TILE = 1024
NA_Q_ROWS = 4
NA_K_ROWS = NA_Q_ROWS + WIN_H - 1
NA_META_PAD = GRID_W

BF16 = jnp.bfloat16
F32 = jnp.float32


def _const_spec(shape):
    zeros = (0,) * len(shape)
    return pl.BlockSpec(shape, lambda *_: zeros, pipeline_mode=pl.Buffered(1))


def _params(n_grid_dims):
    return pltpu.CompilerParams(
        dimension_semantics=("arbitrary",) * n_grid_dims,
        vmem_limit_bytes=VMEM_LIMIT)


def _rms_scale(x):
    return lax.rsqrt(jnp.mean(x * x, axis=-1, keepdims=True) + RMS_EPS)


def _rms_norm(x, g):
    return x * _rms_scale(x) * g


def _dot(a, b):
    return jnp.dot(a, b, preferred_element_type=F32)


def _dot_nt(a, b):
    return lax.dot_general(a, b, (((1,), (1,)), ((), ())), preferred_element_type=F32)


def _ffn_rows(x, g_ref, w_in_ref, w_out_ref, hidden_chunks=((0, D_FF),)):
    r = _rms_scale(x)
    xg = (x * g_ref[...]).astype(BF16)
    acc = None
    for lo, hi in hidden_chunks:
        gate = _dot(xg, w_in_ref[:, lo:hi]) * r
        up = _dot(xg, w_in_ref[:, D_FF + lo:D_FF + hi]) * r
        act = (jax.nn.silu(gate) * up).astype(BF16)
        part = _dot(act, w_out_ref[lo:hi, :])
        acc = part if acc is None else acc + part
    return x + 0.5 * acc


def _ffn_body(x_ref, g_ref, w_in_ref, w_out_ref, gf_ref, o_ref, *, final_norm):
    y = _ffn_rows(x_ref[...], g_ref, w_in_ref, w_out_ref, FFN2_HIDDEN_CHUNKS)
    if final_norm:
        y = _rms_norm(y, gf_ref[...])
    o_ref[...] = y


def _proj_rows(h, g_ref, w_ref, u_ref, q_ref, k_ref, v_ref, time_major_u=False):
    p = _dot((h * g_ref[...]).astype(BF16), w_ref[...]) * _rms_scale(h)
    u = p[:, :SSM_WIDTH]
    if time_major_u:
        u = jnp.swapaxes(u.reshape(q_ref.shape[0], -1, SSM_WIDTH), 0, 1)
    u_ref[...] = u.reshape(u_ref.shape)
    q = (p[:, SSM_WIDTH:SSM_WIDTH + NA_WIDTH] * (NA_HEAD_DIM ** -0.5)).astype(BF16)
    q_ref[...] = q.reshape(q_ref.shape)
    k_ref[...] = p[:, SSM_WIDTH + NA_WIDTH:SSM_WIDTH + 2 * NA_WIDTH].astype(BF16).reshape(k_ref.shape)
    v_ref[...] = p[:, SSM_WIDTH + 2 * NA_WIDTH:].astype(BF16).reshape(v_ref.shape)


N_FFN_PROJ_IN = 7
N_FFN_PROJ_OUT = 9


def _ffn_proj_body(*refs):
    x_ref, xm_ref, g_ref, w_in_ref, w_out_ref, gp_ref, wp_ref = refs[:N_FFN_PROJ_IN]
    n_cast = (len(refs) - N_FFN_PROJ_IN - N_FFN_PROJ_OUT) // 2
    cast_in = refs[N_FFN_PROJ_IN:N_FFN_PROJ_IN + n_cast]
    outs = refs[N_FFN_PROJ_IN + n_cast:]
    h_ref, u_ref, q_ref, k_ref, v_ref, um_ref, qm_ref, km_ref, vm_ref = outs[:N_FFN_PROJ_OUT]
    cast_out = outs[N_FFN_PROJ_OUT:]

    @pl.when(pl.program_id(0) == 0)
    def _():
        hm = _ffn_rows(xm_ref[...], g_ref, w_in_ref, w_out_ref)
        _proj_rows(hm, gp_ref, wp_ref, um_ref, qm_ref, km_ref, vm_ref)

    h = _ffn_rows(x_ref[...].reshape(-1, D_MODEL), g_ref, w_in_ref, w_out_ref)
    h_ref[...] = h.reshape(h_ref.shape)
    _proj_rows(h, gp_ref, wp_ref, u_ref, q_ref, k_ref, v_ref, time_major_u=True)
    for src, dst in zip(cast_in, cast_out):
        dst[...] = src[...].astype(BF16)


def _ffn_proj_call(x, x_meta, g, w_in, w_out, g_proj, w_proj, to_cast):
    batch, seq, _ = x.shape
    rows = batch * seq
    steps = rows // ROW_TILE
    times = ROW_TILE // batch
    tile_spec = lambda width: pl.BlockSpec((batch, times, width), lambda i: (0, i, 0))
    meta_spec = lambda width: pl.BlockSpec((N_META, width), lambda i: (0, 0))
    wide = lambda n, dt: jax.ShapeDtypeStruct((n, SSM_WIDTH), dt)
    tall = lambda width, dt: jax.ShapeDtypeStruct((batch, seq, width), dt)

    def cast_specs(a, col_block, n_cols):
        n = a.shape[0]
        chunk = min(c for c in range(BF16_ROWS, n + 1, BF16_ROWS) if n % c == 0 and n // c <= steps)
        last = n // chunk - 1
        return (pl.BlockSpec((chunk, n_cols), lambda i: (jnp.minimum(i, last), col_block)),
                pl.BlockSpec((chunk, n_cols), lambda i: (jnp.minimum(i, last), 0)))

    cast_in_specs, cast_out_specs = zip(*[cast_specs(*c) for c in to_cast])
    cast_shapes = [jax.ShapeDtypeStruct((a.shape[0], n_cols), BF16) for a, _, n_cols in to_cast]
    return pl.pallas_call(
        _ffn_proj_body,
        grid=(steps,),
        in_specs=[tile_spec(D_MODEL), meta_spec(D_MODEL), _const_spec((1, D_MODEL)),
                  _const_spec((D_MODEL, 2 * D_FF)), _const_spec((D_FF, D_MODEL)),
                  _const_spec((1, D_MODEL)), _const_spec((D_MODEL, UQKV_WIDTH))] + list(cast_in_specs),
        out_specs=[tile_spec(D_MODEL), pl.BlockSpec((ROW_TILE, SSM_WIDTH), lambda i: (i, 0))]
        + [tile_spec(SSM_WIDTH)] * 3 + [meta_spec(SSM_WIDTH)] * 4 + list(cast_out_specs),
        out_shape=[tall(D_MODEL, F32), wide(rows, F32)] + [tall(SSM_WIDTH, BF16)] * 3
        + [wide(N_META, F32)] + [wide(N_META, BF16)] * 3 + cast_shapes,
        compiler_params=_params(1),
        name="ffn_proj",
    )(x, x_meta, g, w_in, w_out, g_proj, w_proj, *[a for a, _, _ in to_cast])


def _ffn_call(x, g, w_in, w_out, gf, *, grid, x_spec, o_spec, out_shape, final_norm):
    return pl.pallas_call(
        functools.partial(_ffn_body, final_norm=final_norm),
        grid=grid,
        in_specs=[x_spec, _const_spec((1, D_MODEL)), _const_spec((D_MODEL, 2 * D_FF)),
                  _const_spec((D_FF, D_MODEL)), _const_spec((1, D_MODEL))],
        out_specs=o_spec,
        out_shape=jax.ShapeDtypeStruct(out_shape, F32),
        compiler_params=_params(len(grid)),
        name="ffn",
    )(x, g, w_in, w_out, gf)


def _same_group(shape, ch_axis, ch0, state0):
    ch = lax.broadcasted_iota(jnp.int32, shape, ch_axis) + ch0
    st = lax.broadcasted_iota(jnp.int32, shape, 1 - ch_axis) + state0
    return (ch >> int(math.log2(SSM_GROUP))) == (st >> int(math.log2(SSM_STATE)))


def _disc_body(a_re_ref, a_im_ref, log_dt_ref, b_re_ref, b_im_ref,
               lb_re_ref, lb_im_ref, w_re_ref, w_im_ref):
    dt = jnp.exp(log_dt_ref[...])
    a_re = jnp.minimum(a_re_ref[...], A_RE_MAX)
    a_im = a_im_ref[...]
    mag = jnp.exp(a_re * dt)
    lb_re = mag * jnp.cos(a_im * dt)
    lb_im = mag * jnp.sin(a_im * dt)
    den = a_re * a_re + a_im * a_im
    nr = lb_re - 1.0
    ni = lb_im
    f_re = (nr * a_re + ni * a_im) / den
    f_im = (ni * a_re - nr * a_im) / den
    lb_re_ref[...] = lb_re
    lb_im_ref[...] = lb_im
    bb_re = f_re * b_re_ref[...] - f_im * b_im_ref[...]
    bb_im = f_re * b_im_ref[...] + f_im * b_re_ref[...]
    for j in range(SCAN_BLOCKS):
        same = _same_group((LANES, SCAN_COLS), 0, _lane_tile(j) * LANES, j * SCAN_COLS)
        for bb, w_ref in ((bb_re, w_re_ref), (bb_im, w_im_ref)):
            rows = jnp.concatenate([bb[:, j * SCAN_COLS:(j + 1) * SCAN_COLS]] * (LANES // SSM_GROUP), axis=0)
            w_ref[j] = jnp.where(same, rows, 0.0).astype(BF16)


def _cslab_body(c_re_ref, c_im_ref, ct_re_ref, ct_im_ref):
    for j in range(SCAN_BLOCKS):
        same = _same_group((LANES, SCAN_COLS), 0, _lane_tile(j) * LANES, j * SCAN_COLS)
        for c_ref, ct_ref, sign in ((c_re_ref, ct_re_ref, 1.0), (c_im_ref, ct_im_ref, -1.0)):
            c = c_ref[_lane_tile(j) * LANES:(_lane_tile(j) + 1) * LANES, :] * sign
            cols = jnp.concatenate([c] * (SCAN_COLS // SSM_STATE), axis=1)
            ct_ref[j] = jnp.where(same, cols, 0.0).astype(BF16)


SCAN_BLOCKS = N_STATES // SCAN_COLS
BLOCKS_PER_LANE_TILE = LANES * SSM_STATE // (SSM_GROUP * SCAN_COLS)


def _lane_tile(j):
    return j // BLOCKS_PER_LANE_TILE


def _s5_body(*refs, reverse, steps, finish):
    if finish:
        (u_ref, init_ref, lb_re_ref, lb_im_ref, w_re_ref, w_im_ref, c_re_ref, c_im_ref,
         yf_ref, d_ref, w_glu_ref, o_ref, fin_ref, bu_re, bu_im, h_re, h_im) = refs
    else:
        (u_ref, init_ref, lb_re_ref, lb_im_ref, w_re_ref, w_im_ref, c_re_ref, c_im_ref,
         o_ref, fin_ref, bu_re, bu_im, h_re, h_im) = refs
    nb = init_ref.shape[0]

    @pl.when(pl.program_id(0) == 0)
    def _():
        h_re[...] = init_ref[:, :N_STATES]
        h_im[...] = init_ref[:, N_STATES:]

    def scan_order(ref):
        if not reverse:
            return ref[...]
        return jnp.concatenate(
            [ref[(steps - 1 - s) * nb:(steps - s) * nb, :] for s in range(steps)], axis=0)

    u = scan_order(u_ref)
    ub = u.astype(BF16)
    y_parts = []
    for j in range(SCAN_BLOCKS):
        cols = slice(j * SCAN_COLS, (j + 1) * SCAN_COLS)
        us = ub[:, _lane_tile(j) * LANES:(_lane_tile(j) + 1) * LANES]
        bu_re[j] = _dot(us, w_re_ref[j])
        bu_im[j] = _dot(us, w_im_ref[j])
        lr = jnp.broadcast_to(lb_re_ref[:, cols], (nb, SCAN_COLS))
        li = jnp.broadcast_to(lb_im_ref[:, cols], (nb, SCAN_COLS))
        hr, hi = h_re[:, cols], h_im[:, cols]
        for s in range(steps):
            rows = slice(s * nb, (s + 1) * nb)
            hr, hi = (lr * hr - li * hi + bu_re[j, rows, :],
                      lr * hi + li * hr + bu_im[j, rows, :])
            bu_re[j, rows, :] = hr
            bu_im[j, rows, :] = hi
        h_re[:, cols] = hr
        h_im[:, cols] = hi
        part = (_dot_nt(bu_re[j].astype(BF16), c_re_ref[j])
                + _dot_nt(bu_im[j].astype(BF16), c_im_ref[j]))
        if j % BLOCKS_PER_LANE_TILE == 0:
            y_parts.append(part)
        else:
            y_parts[-1] = y_parts[-1] + part
    y = jnp.concatenate(y_parts, axis=-1)
    fin_ref[:, :N_STATES] = h_re[...]
    fin_ref[:, N_STATES:] = h_im[...]
    if finish:
        y = y + scan_order(yf_ref) + d_ref[...] * u
        z = jax.nn.gelu(y)
        y = (z * jax.nn.sigmoid(_dot(z.astype(BF16), w_glu_ref[...]))).astype(BF16)
    if reverse:
        for s in range(steps):
            o_ref[(steps - 1 - s) * nb:(steps - s) * nb, :] = y[s * nb:(s + 1) * nb, :]
    else:
        o_ref[...] = y


def _s5_rider_body(*refs, n_in, s5_body, rider_prologue, rider_body):
    s5_in, rider_in = refs[:n_in], refs[n_in:n_in + 6]
    outs, scratch = refs[n_in + 6:n_in + 9], refs[n_in + 9:]
    rider_prologue(*rider_in, outs[2], scratch[4])
    s5_body(*s5_in, *outs[:2], *scratch[:4])
    rider_body(*rider_in, outs[2], scratch[4])


def _s5_call(u, init, lb_re, lb_im, w_re, w_im, c_re, c_im, finish_args, *, tile, reverse, rider=None):
    rows = u.shape[0]
    nb = init.shape[0]
    n_tiles = rows // tile
    finish = finish_args is not None
    row_map = (lambda i: (n_tiles - 1 - i, 0)) if reverse else (lambda i: (i, 0))
    row_spec = pl.BlockSpec((tile, SSM_WIDTH), row_map)
    in_specs = [row_spec, _const_spec((nb, 2 * N_STATES)),
                _const_spec((1, N_STATES)), _const_spec((1, N_STATES)),
                _const_spec((SCAN_BLOCKS, LANES, SCAN_COLS)), _const_spec((SCAN_BLOCKS, LANES, SCAN_COLS)),
                _const_spec((SCAN_BLOCKS, LANES, SCAN_COLS)), _const_spec((SCAN_BLOCKS, LANES, SCAN_COLS))]
    args = [u, init, lb_re, lb_im, w_re, w_im, c_re, c_im]
    if finish:
        in_specs += [row_spec, _const_spec((1, SSM_WIDTH)), _const_spec((SSM_WIDTH, SSM_WIDTH))]
        args += list(finish_args)
    body = functools.partial(_s5_body, reverse=reverse, steps=tile // nb, finish=finish)
    out_specs = [row_spec, pl.BlockSpec((nb, 2 * N_STATES), lambda i: (0, 0))]
    out_shape = [jax.ShapeDtypeStruct((rows, SSM_WIDTH), BF16 if finish else F32),
                 jax.ShapeDtypeStruct((nb, 2 * N_STATES), F32)]
    scratch = [pltpu.VMEM((SCAN_BLOCKS, tile, SCAN_COLS), F32),
               pltpu.VMEM((SCAN_BLOCKS, tile, SCAN_COLS), F32),
               pltpu.VMEM((nb, N_STATES), F32), pltpu.VMEM((nb, N_STATES), F32)]
    if rider is not None:
        assert rider["steps"] == n_tiles, "one rider unit per scan tile"
        body = functools.partial(_s5_rider_body, n_in=len(args), s5_body=body,
                                 rider_prologue=rider["prologue"], rider_body=rider["body"])
        in_specs = in_specs + rider["in_specs"]
        args = args + rider["args"]
        out_specs.append(rider["out_spec"])
        out_shape.append(rider["out_shape"])
        scratch.append(rider["scratch"])
    return pl.pallas_call(
        body,
        grid=(n_tiles,),
        in_specs=in_specs,
        out_specs=out_specs,
        out_shape=out_shape,
        scratch_shapes=scratch,
        compiler_params=_params(1),
        name="s5_bwd" if reverse else "s5_fwd",
    )(*args)


def _s5_setup_body(af_re_ref, af_im_ref, dtf_ref, ab_re_ref, ab_im_ref, dtb_ref, b_re_ref, b_im_ref,
                   c_re_ref, c_im_ref, u_ref, init_ref,
                   lbf_re_ref, lbf_im_ref, wf_re_ref, wf_im_ref, lbb_re_ref, lbb_im_ref, wb_re_ref, wb_im_ref,
                   ct_re_ref, ct_im_ref, y_ref, fin_ref, bu_re, bu_im, h_re, h_im, *, steps):
    _disc_body(af_re_ref, af_im_ref, dtf_ref, b_re_ref, b_im_ref, lbf_re_ref, lbf_im_ref, wf_re_ref, wf_im_ref)
    _disc_body(ab_re_ref, ab_im_ref, dtb_ref, b_re_ref, b_im_ref, lbb_re_ref, lbb_im_ref, wb_re_ref, wb_im_ref)
    _cslab_body(c_re_ref, c_im_ref, ct_re_ref, ct_im_ref)
    _s5_body(u_ref, init_ref, lbf_re_ref, lbf_im_ref, wf_re_ref, wf_im_ref, ct_re_ref, ct_im_ref,
             y_ref, fin_ref, bu_re, bu_im, h_re, h_im, reverse=False, steps=steps, finish=False)


def _s5_setup_call(disc_fwd, disc_bwd, b_re_t, b_im_t, c_re, c_im, u_meta_rows, init):
    rows, nb = u_meta_rows.shape[0], init.shape[0]
    args = [*disc_fwd, *disc_bwd, b_re_t, b_im_t, c_re, c_im, u_meta_rows, init]
    row = jax.ShapeDtypeStruct((1, N_STATES), F32)
    slabs = jax.ShapeDtypeStruct((SCAN_BLOCKS, LANES, SCAN_COLS), BF16)
    out_shape = [row, row, slabs, slabs, row, row, slabs, slabs, slabs, slabs,
                 jax.ShapeDtypeStruct((rows, SSM_WIDTH), F32), jax.ShapeDtypeStruct((nb, 2 * N_STATES), F32)]
    whole = lambda s: pl.BlockSpec(s.shape, lambda i, n=len(s.shape): (0,) * n)
    outs = pl.pallas_call(
        functools.partial(_s5_setup_body, steps=rows // nb),
        grid=(1,),
        in_specs=[whole(a) for a in args],
        out_specs=[whole(s) for s in out_shape],
        out_shape=out_shape,
        scratch_shapes=[pltpu.VMEM((SCAN_BLOCKS, rows, SCAN_COLS), F32),
                        pltpu.VMEM((SCAN_BLOCKS, rows, SCAN_COLS), F32),
                        pltpu.VMEM((nb, N_STATES), F32), pltpu.VMEM((nb, N_STATES), F32)],
        compiler_params=_params(1),
        name="s5_setup",
    )(*args)
    return outs[:4], outs[4:8], outs[8:10], outs[11]


NA_Q = NA_Q_ROWS * GRID_W
NA_K = NA_K_ROWS * GRID_W
NA_KEYS = NA_K + NA_META_PAD
NA_RING = LANES


def _block_key_row(blk, rows):
    return min(max(blk * NA_Q_ROWS - WIN_H // 2, 0), rows - NA_K_ROWS)


def _na_assemble_bias(ring_ref, bias, n_blocks, rows):
    kh = min(WIN_H, rows)
    masked = jnp.full((GRID_W, GRID_W), MASK_VALUE, F32)
    meta = jnp.where(lax.broadcasted_iota(jnp.int32, (GRID_W, GRID_W), 1) < N_META, 0.0, MASK_VALUE)
    cq = lax.broadcasted_iota(jnp.int32, (GRID_W, GRID_W), 0)
    ck = lax.broadcasted_iota(jnp.int32, (GRID_W, GRID_W), 1)
    col_start = jnp.clip(cq - WIN_W // 2, 0, GRID_W - WIN_W)
    col_valid = (ck >= col_start) & (ck < col_start + WIN_W)
    col_tiles = {}

    def col_tile(head, off):
        if (head, off) not in col_tiles:
            ring = jnp.broadcast_to(ring_ref[head, off:off + 1, :], (GRID_W, NA_RING))
            toeplitz = pltpu.roll(ring, 0, 1, stride=1, stride_axis=0)[:, :GRID_W]
            col_tiles[head, off] = jnp.where(col_valid, toeplitz, MASK_VALUE)
        return col_tiles[head, off]

    for kind, blk in enumerate((0, 1, n_blocks - 1)):
        for head in range(2):
            for qr in range(NA_Q_ROWS):
                r = blk * NA_Q_ROWS + qr
                row_start = min(max(r - kh // 2, 0), rows - kh)
                tiles = []
                for slot in range(NA_K_ROWS):
                    kr = _block_key_row(blk, rows) + slot
                    valid = row_start <= kr < row_start + kh
                    tiles.append(col_tile(head, kr - r + WIN_H - 1) if valid else masked)
                tiles.append(meta)
                for c in range(0, len(tiles), 2):
                    bias[kind, head, qr * GRID_W:(qr + 1) * GRID_W, c * GRID_W:(c + 2) * GRID_W] = (
                        jnp.concatenate(tiles[c:c + 2], axis=1))


def _na_prologue(q_ref, k_ref, v_ref, km_ref, vm_ref, t_ref, o_ref, bias, *, n_blocks, rows, batch):
    @pl.when(pl.program_id(0) % batch == 0)
    def _():
        _na_assemble_bias(t_ref, bias, n_blocks, rows)


def _na_body(q_ref, k_ref, v_ref, km_ref, vm_ref, t_ref, o_ref, bias, *, n_blocks, rows, batch):
    first_head = lax.broadcasted_iota(jnp.int32, (1, LANES), 1) < NA_HEAD_DIM
    km = km_ref[...]
    vm = vm_ref[...]
    for i in range(n_blocks):
        k0 = _block_key_row(i, rows) * GRID_W
        q0 = i * NA_Q
        kind = 0 if i == 0 else 2 if i == n_blocks - 1 else 1
        q = q_ref[q0:q0 + NA_Q, :]
        zero = jnp.zeros_like(q)
        qs = jnp.concatenate([jnp.where(first_head, q, zero), jnp.where(first_head, zero, q)], axis=0)
        keys = jnp.concatenate([k_ref[k0:k0 + NA_K, :], km], axis=0)
        vals = jnp.concatenate([v_ref[k0:k0 + NA_K, :], vm], axis=0)
        s = _dot_nt(qs, keys) + bias[kind].reshape(2 * NA_Q, NA_KEYS)
        m = jnp.max(s, axis=-1, keepdims=True)
        p = jnp.exp(s - m)
        l = jnp.sum(p, axis=-1, keepdims=True)
        o = _dot(p.astype(BF16), vals) / l
        o_ref[q0:q0 + NA_Q, :] = jnp.where(first_head, o[:NA_Q], o[NA_Q:]).astype(BF16)


def _na_rider(q, k, v, km, vm, col_bias, *, first_pair, n_pairs, batch, seq):
    rows = seq // GRID_W
    assert NA_META_PAD == GRID_W and (NA_K_ROWS + 1) % 2 == 0
    pair = lambda i: first_pair + i // batch
    seq_spec = pl.BlockSpec((seq, LANES), lambda i: (i % batch, pair(i)))
    meta_spec = pl.BlockSpec((NA_META_PAD, LANES), lambda i: (0, pair(i)))
    table_spec = pl.BlockSpec((2, 2 * WIN_H - 1, NA_RING), lambda i: (pair(i), 0, 0))
    return dict(
        steps=n_pairs * batch,
        prologue=functools.partial(_na_prologue, n_blocks=rows // NA_Q_ROWS, rows=rows, batch=batch),
        body=functools.partial(_na_body, n_blocks=rows // NA_Q_ROWS, rows=rows, batch=batch),
        args=[q, k, v, km, vm, col_bias],
        in_specs=[seq_spec, seq_spec, seq_spec, meta_spec, meta_spec, table_spec],
        out_spec=pl.BlockSpec((seq, LANES), lambda i: (i % batch, i // batch)),
        out_shape=jax.ShapeDtypeStruct((batch * seq, n_pairs * LANES), BF16),
        scratch=pltpu.VMEM((3, 2, NA_Q, NA_KEYS), F32))


def _na_col_ring(rpb):
    n_head, n_off, _ = rpb.shape
    ring = jnp.zeros((n_head, n_off, NA_RING), F32)
    ring = ring.at[..., :WIN_W].set(rpb[..., WIN_W - 1:].astype(F32))
    return ring.at[..., NA_RING - (WIN_W - 1):].set(rpb[..., :WIN_W - 1].astype(F32))


def _merge_body(h_ref, z_ref, a_lo_ref, a_hi_ref, g_ref, w_gate_ref, w_ssm_ref, w_na_ref, w_out_ref, o_ref):
    nb, times, _ = h_ref.shape
    h = h_ref[...].reshape(nb * times, D_MODEL)
    hn = _rms_norm(h, g_ref[...]).astype(BF16)
    gates = _dot(hn, w_gate_ref[...])
    z = jnp.swapaxes(z_ref[...].astype(F32).reshape(times, nb, SSM_WIDTH), 0, 1)
    y_ssm = _dot(z.reshape(nb * times, SSM_WIDTH).astype(BF16), w_ssm_ref[...])
    y_na = (_dot(a_lo_ref[...].reshape(nb * times, NA_WIDTH // 2), w_na_ref[:NA_WIDTH // 2, :])
            + _dot(a_hi_ref[...].reshape(nb * times, NA_WIDTH // 2), w_na_ref[NA_WIDTH // 2:, :]))
    merged = (jax.nn.sigmoid(gates[:, :D_MODEL]) * y_ssm
              + jax.nn.sigmoid(gates[:, D_MODEL:]) * y_na)
    o_ref[...] = (h + _dot(merged.astype(BF16), w_out_ref[...])).reshape(o_ref.shape)


def _merge_call(h1, z, a_lo, a_hi, g, w_gate, w_ssm, w_na, w_out):
    batch, seq, _ = h1.shape
    times = WIDE_ROW_TILE // batch
    tile_spec = lambda width: pl.BlockSpec((batch, times, width), lambda i: (0, i, 0))
    return pl.pallas_call(
        _merge_body,
        grid=(seq // times,),
        in_specs=[tile_spec(D_MODEL), pl.BlockSpec((WIDE_ROW_TILE, SSM_WIDTH), lambda i: (i, 0)),
                  tile_spec(NA_WIDTH // 2), tile_spec(NA_WIDTH // 2),
                  _const_spec((1, D_MODEL)), _const_spec((D_MODEL, 2 * D_MODEL)),
                  _const_spec((SSM_WIDTH, D_MODEL)), _const_spec((NA_WIDTH, D_MODEL)),
                  _const_spec((D_MODEL, D_MODEL))],
        out_specs=tile_spec(D_MODEL),
        out_shape=jax.ShapeDtypeStruct((batch, seq, D_MODEL), F32),
        compiler_params=_params(1),
        name="merge",
    )(h1, z, a_lo, a_hi, g, w_gate, w_ssm, w_na, w_out)


def kernel(x, meta_tokens, norm_ffn1, w_ffn1_in, w_ffn1_out, norm_mix, w_in, ssm_a_re_fwd, ssm_a_im_fwd, ssm_log_dt_fwd, ssm_a_re_bwd, ssm_a_im_bwd, ssm_log_dt_bwd, ssm_b_re, ssm_b_im, ssm_c_re, ssm_c_im, ssm_d, w_glu, na_rpb, w_branch_ssm, w_branch_na, w_out, norm_ffn2, w_ffn2_in, w_ffn2_out, norm_final):
    batch, seq, d_model = x.shape
    assert d_model == D_MODEL and norm_ffn1.shape[0] == 1, "single-layer block of width 1024"
    assert seq % ROW_TILE == 0 and seq % (GRID_W * NA_Q_ROWS) == 0 and batch % 8 == 0
    rows = seq // GRID_W
    assert rows >= NA_K_ROWS + NA_Q_ROWS

    row = lambda a: a.reshape(1, -1).astype(F32)
    g1, gm, g2, gf = row(norm_ffn1[0]), row(norm_mix[0]), row(norm_ffn2[0]), row(norm_final)
    w1_in, w1_out = w_ffn1_in[0].astype(BF16), w_ffn1_out[0].astype(BF16)
    w_uqkv = w_in[0][:, :UQKV_WIDTH].astype(BF16)
    whole = lambda a: (a[0].astype(F32), 0, a.shape[-1])
    later_weights = [whole(w_ffn2_in), whole(w_ffn2_out), (w_in[0].astype(F32), 1, UQKV_WIDTH),
                     whole(w_branch_ssm), whole(w_branch_na), whole(w_out), whole(w_glu)]
    assert w_in.shape[-1] == 2 * UQKV_WIDTH, "the two gates are the second column block of w_in"

    n_rows = batch * seq
    tok_spec = pl.BlockSpec((FFN2_ROW_TILE, D_MODEL), lambda i: (i, 0))
    (h1, u, q, k, v, u_meta, _, k_meta, v_meta,
     w2_in, w2_out, w_gate, w_ssm, w_na, w_mix_out, w_glu_bf) = _ffn_proj_call(
        x, meta_tokens.astype(F32), g1, w1_in, w1_out, gm, w_uqkv, later_weights)
    q, k, v = (a.reshape(n_rows, NA_WIDTH) for a in (q, k, v))

    states = lambda a: a[0].reshape(1, N_STATES).astype(F32)
    per_state = lambda a: jnp.repeat(a[0].astype(F32), SSM_STATE).reshape(1, N_STATES)
    b_re_t = ssm_b_re[0].reshape(N_STATES, SSM_GROUP).T.astype(F32)
    b_im_t = ssm_b_im[0].reshape(N_STATES, SSM_GROUP).T.astype(F32)
    zero_state = jnp.zeros((batch, 2 * N_STATES), F32)
    u_meta_rows = jnp.repeat(u_meta, batch, axis=0)
    (lbf_re, lbf_im, wf_re, wf_im), (lbb_re, lbb_im, wb_re, wb_im), (c_re, c_im), state_meta = _s5_setup_call(
        (states(ssm_a_re_fwd), states(ssm_a_im_fwd), per_state(ssm_log_dt_fwd)),
        (states(ssm_a_re_bwd), states(ssm_a_im_bwd), per_state(ssm_log_dt_bwd)),
        b_re_t, b_im_t,
        ssm_c_re[0].reshape(SSM_WIDTH, SSM_STATE).astype(F32),
        ssm_c_im[0].reshape(SSM_WIDTH, SSM_STATE).astype(F32),
        u_meta_rows, zero_state)
    pad_meta = lambda a: jnp.pad(a, ((0, NA_META_PAD - N_META), (0, 0)))
    half = NA_HEADS // 4
    riders = [_na_rider(q, k, v, pad_meta(k_meta), pad_meta(v_meta), _na_col_ring(na_rpb[0]),
                        first_pair=p0, n_pairs=half, batch=batch, seq=seq) for p0 in (0, half)]
    y_fwd, _, att_lo = _s5_call(u, state_meta, lbf_re, lbf_im, wf_re, wf_im, c_re, c_im,
                                None, tile=SCAN_TILE, reverse=False, rider=riders[0])
    z, _, att_hi = _s5_call(u, zero_state, lbb_re, lbb_im, wb_re, wb_im, c_re, c_im,
                            (y_fwd, row(ssm_d[0]), w_glu_bf), tile=SCAN_TILE,
                            reverse=True, rider=riders[1])

    half_width = (batch, seq, NA_WIDTH // 2)
    h2 = _merge_call(h1, z, att_lo.reshape(half_width), att_hi.reshape(half_width), gm, w_gate,
                     w_ssm, w_na, w_mix_out).reshape(n_rows, D_MODEL)

    out = _ffn_call(h2, g2, w2_in, w2_out, gf, grid=(n_rows // FFN2_ROW_TILE,), x_spec=tok_spec,
                    o_spec=tok_spec, out_shape=(n_rows, D_MODEL), final_norm=True)
    return out.reshape(batch, seq, D_MODEL)
```

```python
import functools
import math

import jax
import jax.numpy as jnp
from jax import lax
from jax.experimental import pallas as pl
from jax.experimental.pallas import tpu as pltpu

D_MODEL = 1024
N_META = 16
GRID_W = 64
D_FF = 2816
SSM_WIDTH = 512
SSM_GROUP = 16
SSM_GROUPS = 32
SSM_STATE = 64
N_STATES = SSM_GROUPS * SSM_STATE
NA_HEAD_DIM = 64
NA_WIDTH = 512
NA_HEADS = 8
WIN_H = 8
WIN_W = 16
UQKV_WIDTH = SSM_WIDTH + 3 * NA_WIDTH
RMS_EPS = 1e-6
A_RE_MAX = -1e-4
MASK_VALUE = -1e30

LANES = 128
BF16_ROWS = 16
VMEM_LIMIT = 56 * 1024 * 1024

ROW_TILE = 512
FFN2_ROW_TILE = 1024
FFN2_HIDDEN_CHUNKS = ((0, 1536), (1536, D_FF))
WIDE_ROW_TILE = 1024
SCAN_COLS = 256
SCAN_TILE = 1024
NA_Q_ROWS = 4
NA_K_ROWS = NA_Q_ROWS + WIN_H - 1
NA_META_PAD = GRID_W

BF16 = jnp.bfloat16
F32 = jnp.float32


def _const_spec(shape):
    zeros = (0,) * len(shape)
    return pl.BlockSpec(shape, lambda *_: zeros, pipeline_mode=pl.Buffered(1))


def _params(n_grid_dims):
    return pltpu.CompilerParams(
        dimension_semantics=("arbitrary",) * n_grid_dims,
        vmem_limit_bytes=VMEM_LIMIT)


def _rms_scale(x):
    return lax.rsqrt(jnp.mean(x * x, axis=-1, keepdims=True) + RMS_EPS)


def _rms_norm(x, g):
    return x * _rms_scale(x) * g


def _dot(a, b):
    return jnp.dot(a, b, preferred_element_type=F32)


def _dot_nt(a, b):
    return lax.dot_general(a, b, (((1,), (1,)), ((), ())), preferred_element_type=F32)


def _ffn_rows(x, g_ref, w_in_ref, w_out_ref, hidden_chunks=((0, D_FF),)):
    r = _rms_scale(x)
    xg = (x * g_ref[...]).astype(BF16)
    acc = None
    for lo, hi in hidden_chunks:
        gate = _dot(xg, w_in_ref[:, lo:hi]) * r
        up = _dot(xg, w_in_ref[:, D_FF + lo:D_FF + hi]) * r
        act = (jax.nn.silu(gate) * up).astype(BF16)
        part = _dot(act, w_out_ref[lo:hi, :])
        acc = part if acc is None else acc + part
    return x + 0.5 * acc


def _ffn_body(x_ref, g_ref, w_in_ref, w_out_ref, gf_ref, o_ref, *, final_norm):
    y = _ffn_rows(x_ref[...], g_ref, w_in_ref, w_out_ref, FFN2_HIDDEN_CHUNKS)
    if final_norm:
        y = _rms_norm(y, gf_ref[...])
    o_ref[...] = y


def _proj_rows(h, g_ref, w_ref, u_ref, q_ref, k_ref, v_ref, time_major_u=False):
    p = _dot((h * g_ref[...]).astype(BF16), w_ref[...]) * _rms_scale(h)
    u = p[:, :SSM_WIDTH]
    if time_major_u:
        u = jnp.swapaxes(u.reshape(q_ref.shape[0], -1, SSM_WIDTH), 0, 1)
    u_ref[...] = u.reshape(u_ref.shape)
    q = (p[:, SSM_WIDTH:SSM_WIDTH + NA_WIDTH] * (NA_HEAD_DIM ** -0.5)).astype(BF16)
    q_ref[...] = q.reshape(q_ref.shape)
    k_ref[...] = p[:, SSM_WIDTH + NA_WIDTH:SSM_WIDTH + 2 * NA_WIDTH].astype(BF16).reshape(k_ref.shape)
    v_ref[...] = p[:, SSM_WIDTH + 2 * NA_WIDTH:].astype(BF16).reshape(v_ref.shape)


N_FFN_PROJ_IN = 7
N_FFN_PROJ_OUT = 9


def _ffn_proj_body(*refs):
    x_ref, xm_ref, g_ref, w_in_ref, w_out_ref, gp_ref, wp_ref = refs[:N_FFN_PROJ_IN]
    n_cast = (len(refs) - N_FFN_PROJ_IN - N_FFN_PROJ_OUT) // 2
    cast_in = refs[N_FFN_PROJ_IN:N_FFN_PROJ_IN + n_cast]
    outs = refs[N_FFN_PROJ_IN + n_cast:]
    h_ref, u_ref, q_ref, k_ref, v_ref, um_ref, qm_ref, km_ref, vm_ref = outs[:N_FFN_PROJ_OUT]
    cast_out = outs[N_FFN_PROJ_OUT:]

    @pl.when(pl.program_id(0) == 0)
    def _():
        hm = _ffn_rows(xm_ref[...], g_ref, w_in_ref, w_out_ref)
        _proj_rows(hm, gp_ref, wp_ref, um_ref, qm_ref, km_ref, vm_ref)

    h = _ffn_rows(x_ref[...].reshape(-1, D_MODEL), g_ref, w_in_ref, w_out_ref)
    h_ref[...] = h.reshape(h_ref.shape)
    _proj_rows(h, gp_ref, wp_ref, u_ref, q_ref, k_ref, v_ref, time_major_u=True)
    for src, dst in zip(cast_in, cast_out):
        dst[...] = src[...].astype(BF16)


def _ffn_proj_call(x, x_meta, g, w_in, w_out, g_proj, w_proj, to_cast):
    batch, seq, _ = x.shape
    rows = batch * seq
    steps = rows // ROW_TILE
    times = ROW_TILE // batch
    tile_spec = lambda width: pl.BlockSpec((batch, times, width), lambda i: (0, i, 0))
    meta_spec = lambda width: pl.BlockSpec((N_META, width), lambda i: (0, 0))
    wide = lambda n, dt: jax.ShapeDtypeStruct((n, SSM_WIDTH), dt)
    tall = lambda width, dt: jax.ShapeDtypeStruct((batch, seq, width), dt)

    def cast_specs(a, col_block, n_cols):
        n = a.shape[0]
        chunk = min(c for c in range(BF16_ROWS, n + 1, BF16_ROWS) if n % c == 0 and n // c <= steps)
        last = n // chunk - 1
        return (pl.BlockSpec((chunk, n_cols), lambda i: (jnp.minimum(i, last), col_block)),
                pl.BlockSpec((chunk, n_cols), lambda i: (jnp.minimum(i, last), 0)))

    cast_in_specs, cast_out_specs = zip(*[cast_specs(*c) for c in to_cast])
    cast_shapes = [jax.ShapeDtypeStruct((a.shape[0], n_cols), BF16) for a, _, n_cols in to_cast]
    return pl.pallas_call(
        _ffn_proj_body,
        grid=(steps,),
        in_specs=[tile_spec(D_MODEL), meta_spec(D_MODEL), _const_spec((1, D_MODEL)),
                  _const_spec((D_MODEL, 2 * D_FF)), _const_spec((D_FF, D_MODEL)),
                  _const_spec((1, D_MODEL)), _const_spec((D_MODEL, UQKV_WIDTH))] + list(cast_in_specs),
        out_specs=[tile_spec(D_MODEL), pl.BlockSpec((ROW_TILE, SSM_WIDTH), lambda i: (i, 0))]
        + [tile_spec(SSM_WIDTH)] * 3 + [meta_spec(SSM_WIDTH)] * 4 + list(cast_out_specs),
        out_shape=[tall(D_MODEL, F32), wide(rows, F32)] + [tall(SSM_WIDTH, BF16)] * 3
        + [wide(N_META, F32)] + [wide(N_META, BF16)] * 3 + cast_shapes,
        compiler_params=_params(1),
        name="ffn_proj",
    )(x, x_meta, g, w_in, w_out, g_proj, w_proj, *[a for a, _, _ in to_cast])


def _ffn_call(x, g, w_in, w_out, gf, *, grid, x_spec, o_spec, out_shape, final_norm):
    return pl.pallas_call(
        functools.partial(_ffn_body, final_norm=final_norm),
        grid=grid,
        in_specs=[x_spec, _const_spec((1, D_MODEL)), _const_spec((D_MODEL, 2 * D_FF)),
                  _const_spec((D_FF, D_MODEL)), _const_spec((1, D_MODEL))],
        out_specs=o_spec,
        out_shape=jax.ShapeDtypeStruct(out_shape, F32),
        compiler_params=_params(len(grid)),
        name="ffn",
    )(x, g, w_in, w_out, gf)


def _same_group(shape, ch_axis, ch0, state0):
    ch = lax.broadcasted_iota(jnp.int32, shape, ch_axis) + ch0
    st = lax.broadcasted_iota(jnp.int32, shape, 1 - ch_axis) + state0
    return (ch >> int(math.log2(SSM_GROUP))) == (st >> int(math.log2(SSM_STATE)))


def _disc_body(a_re_ref, a_im_ref, log_dt_ref, b_re_ref, b_im_ref,
               lb_re_ref, lb_im_ref, w_re_ref, w_im_ref):
    dt = jnp.exp(log_dt_ref[...])
    a_re = jnp.minimum(a_re_ref[...], A_RE_MAX)
    a_im = a_im_ref[...]
    mag = jnp.exp(a_re * dt)
    lb_re = mag * jnp.cos(a_im * dt)
    lb_im = mag * jnp.sin(a_im * dt)
    den = a_re * a_re + a_im * a_im
    nr = lb_re - 1.0
    ni = lb_im
    f_re = (nr * a_re + ni * a_im) / den
    f_im = (ni * a_re - nr * a_im) / den
    lb_re_ref[...] = lb_re
    lb_im_ref[...] = lb_im
    bb_re = f_re * b_re_ref[...] - f_im * b_im_ref[...]
    bb_im = f_re * b_im_ref[...] + f_im * b_re_ref[...]
    for j in range(SCAN_BLOCKS):
        same = _same_group((LANES, SCAN_COLS), 0, _lane_tile(j) * LANES, j * SCAN_COLS)
        for bb, w_ref in ((bb_re, w_re_ref), (bb_im, w_im_ref)):
            rows = jnp.concatenate([bb[:, j * SCAN_COLS:(j + 1) * SCAN_COLS]] * (LANES // SSM_GROUP), axis=0)
            w_ref[j] = jnp.where(same, rows, 0.0).astype(BF16)


def _cslab_body(c_re_ref, c_im_ref, ct_re_ref, ct_im_ref):
    for j in range(SCAN_BLOCKS):
        same = _same_group((LANES, SCAN_COLS), 0, _lane_tile(j) * LANES, j * SCAN_COLS)
        for c_ref, ct_ref, sign in ((c_re_ref, ct_re_ref, 1.0), (c_im_ref, ct_im_ref, -1.0)):
            c = c_ref[_lane_tile(j) * LANES:(_lane_tile(j) + 1) * LANES, :] * sign
            cols = jnp.concatenate([c] * (SCAN_COLS // SSM_STATE), axis=1)
            ct_ref[j] = jnp.where(same, cols, 0.0).astype(BF16)


SCAN_BLOCKS = N_STATES // SCAN_COLS
BLOCKS_PER_LANE_TILE = LANES * SSM_STATE // (SSM_GROUP * SCAN_COLS)


def _lane_tile(j):
    return j // BLOCKS_PER_LANE_TILE


def _s5_body(*refs, reverse, steps, finish, emit_state=False):
    if emit_state:
        fin_ref, refs = refs[-5], refs[:-5] + refs[-4:]
    if finish:
        (u_ref, init_ref, lb_re_ref, lb_im_ref, w_re_ref, w_im_ref, c_re_ref, c_im_ref,
         yf_ref, d_ref, w_glu_ref, o_ref, bu_re, bu_im, h_re, h_im) = refs
    else:
        (u_ref, init_ref, lb_re_ref, lb_im_ref, w_re_ref, w_im_ref, c_re_ref, c_im_ref,
         o_ref, bu_re, bu_im, h_re, h_im) = refs
    nb = init_ref.shape[0]

    @pl.when(pl.program_id(0) == 0)
    def _():
        h_re[...] = init_ref[:, :N_STATES]
        h_im[...] = init_ref[:, N_STATES:]

    def scan_order(ref):
        if not reverse:
            return ref[...]
        return jnp.concatenate(
            [ref[(steps - 1 - s) * nb:(steps - s) * nb, :] for s in range(steps)], axis=0)

    u = scan_order(u_ref)
    ub = u.astype(BF16)
    y_parts = []
    for j in range(SCAN_BLOCKS):
        cols = slice(j * SCAN_COLS, (j + 1) * SCAN_COLS)
        us = ub[:, _lane_tile(j) * LANES:(_lane_tile(j) + 1) * LANES]
        bu_re[j] = _dot(us, w_re_ref[j])
        bu_im[j] = _dot(us, w_im_ref[j])
        lr = jnp.broadcast_to(lb_re_ref[:, cols], (nb, SCAN_COLS))
        li = jnp.broadcast_to(lb_im_ref[:, cols], (nb, SCAN_COLS))
        hr, hi = h_re[:, cols], h_im[:, cols]
        for s in range(steps):
            rows = slice(s * nb, (s + 1) * nb)
            hr, hi = (lr * hr - li * hi + bu_re[j, rows, :],
                      lr * hi + li * hr + bu_im[j, rows, :])
            bu_re[j, rows, :] = hr
            bu_im[j, rows, :] = hi
        h_re[:, cols] = hr
        h_im[:, cols] = hi
        part = (_dot_nt(bu_re[j].astype(BF16), c_re_ref[j])
                + _dot_nt(bu_im[j].astype(BF16), c_im_ref[j]))
        if j % BLOCKS_PER_LANE_TILE == 0:
            y_parts.append(part)
        else:
            y_parts[-1] = y_parts[-1] + part
    y = jnp.concatenate(y_parts, axis=-1)
    if emit_state:
        fin_ref[:, :N_STATES] = h_re[...]
        fin_ref[:, N_STATES:] = h_im[...]
    if finish:
        y = y + scan_order(yf_ref) + d_ref[...] * u
        z = jax.nn.gelu(y)
        y = (z * jax.nn.sigmoid(_dot(z.astype(BF16), w_glu_ref[...]))).astype(BF16)
    if reverse:
        for s in range(steps):
            o_ref[(steps - 1 - s) * nb:(steps - s) * nb, :] = y[s * nb:(s + 1) * nb, :]
    else:
        o_ref[...] = y


def _s5_rider_body(*refs, n_in, s5_body, rider_prologue, rider_body):
    s5_in, rider_in = refs[:n_in], refs[n_in:n_in + 6]
    outs, scratch = refs[n_in + 6:n_in + 8], refs[n_in + 8:]
    rider_prologue(*rider_in, outs[1], scratch[4])
    s5_body(*s5_in, outs[0], *scratch[:4])
    rider_body(*rider_in, outs[1], scratch[4])


def _s5_call(u, init, lb_re, lb_im, w_re, w_im, c_re, c_im, finish_args, *, tile, reverse, rider=None):
    rows = u.shape[0]
    nb = init.shape[0]
    n_tiles = rows // tile
    finish = finish_args is not None
    row_map = (lambda i: (n_tiles - 1 - i, 0)) if reverse else (lambda i: (i, 0))
    row_spec = pl.BlockSpec((tile, SSM_WIDTH), row_map)
    in_specs = [row_spec, _const_spec((nb, 2 * N_STATES)),
                _const_spec((1, N_STATES)), _const_spec((1, N_STATES)),
                _const_spec((SCAN_BLOCKS, LANES, SCAN_COLS)), _const_spec((SCAN_BLOCKS, LANES, SCAN_COLS)),
                _const_spec((SCAN_BLOCKS, LANES, SCAN_COLS)), _const_spec((SCAN_BLOCKS, LANES, SCAN_COLS))]
    args = [u, init, lb_re, lb_im, w_re, w_im, c_re, c_im]
    if finish:
        in_specs += [row_spec, _const_spec((1, SSM_WIDTH)), _const_spec((SSM_WIDTH, SSM_WIDTH))]
        args += list(finish_args)
    body = functools.partial(_s5_body, reverse=reverse, steps=tile // nb, finish=finish)
    out_specs = [row_spec]
    out_shape = [jax.ShapeDtypeStruct((rows, SSM_WIDTH), BF16 if finish else F32)]
    scratch = [pltpu.VMEM((SCAN_BLOCKS, tile, SCAN_COLS), F32),
               pltpu.VMEM((SCAN_BLOCKS, tile, SCAN_COLS), F32),
               pltpu.VMEM((nb, N_STATES), F32), pltpu.VMEM((nb, N_STATES), F32)]
    if rider is not None:
        assert rider["steps"] == n_tiles, "one rider unit per scan tile"
        body = functools.partial(_s5_rider_body, n_in=len(args), s5_body=body,
                                 rider_prologue=rider["prologue"], rider_body=rider["body"])
        in_specs = in_specs + rider["in_specs"]
        args = args + rider["args"]
        out_specs.append(rider["out_spec"])
        out_shape.append(rider["out_shape"])
        scratch.append(rider["scratch"])
    return pl.pallas_call(
        body,
        grid=(n_tiles,),
        in_specs=in_specs,
        out_specs=out_specs,
        out_shape=out_shape,
        scratch_shapes=scratch,
        compiler_params=_params(1),
        name="s5_bwd" if reverse else "s5_fwd",
    )(*args)


def _s5_setup_body(af_re_ref, af_im_ref, dtf_ref, ab_re_ref, ab_im_ref, dtb_ref, b_re_ref, b_im_ref,
                   c_re_ref, c_im_ref, u_ref, init_ref,
                   lbf_re_ref, lbf_im_ref, wf_re_ref, wf_im_ref, lbb_re_ref, lbb_im_ref, wb_re_ref, wb_im_ref,
                   ct_re_ref, ct_im_ref, y_ref, fin_ref, bu_re, bu_im, h_re, h_im, *, steps):
    _disc_body(af_re_ref, af_im_ref, dtf_ref, b_re_ref, b_im_ref, lbf_re_ref, lbf_im_ref, wf_re_ref, wf_im_ref)
    _disc_body(ab_re_ref, ab_im_ref, dtb_ref, b_re_ref, b_im_ref, lbb_re_ref, lbb_im_ref, wb_re_ref, wb_im_ref)
    _cslab_body(c_re_ref, c_im_ref, ct_re_ref, ct_im_ref)
    _s5_body(u_ref, init_ref, lbf_re_ref, lbf_im_ref, wf_re_ref, wf_im_ref, ct_re_ref, ct_im_ref,
             y_ref, fin_ref, bu_re, bu_im, h_re, h_im, reverse=False, steps=steps, finish=False,
             emit_state=True)


def _s5_setup_call(disc_fwd, disc_bwd, b_re_t, b_im_t, c_re, c_im, u_meta_rows, init):
    rows, nb = u_meta_rows.shape[0], init.shape[0]
    args = [*disc_fwd, *disc_bwd, b_re_t, b_im_t, c_re, c_im, u_meta_rows, init]
    row = jax.ShapeDtypeStruct((1, N_STATES), F32)
    slabs = jax.ShapeDtypeStruct((SCAN_BLOCKS, LANES, SCAN_COLS), BF16)
    out_shape = [row, row, slabs, slabs, row, row, slabs, slabs, slabs, slabs,
                 jax.ShapeDtypeStruct((rows, SSM_WIDTH), F32), jax.ShapeDtypeStruct((nb, 2 * N_STATES), F32)]
    whole = lambda s: pl.BlockSpec(s.shape, lambda i, n=len(s.shape): (0,) * n)
    outs = pl.pallas_call(
        functools.partial(_s5_setup_body, steps=rows // nb),
        grid=(1,),
        in_specs=[whole(a) for a in args],
        out_specs=[whole(s) for s in out_shape],
        out_shape=out_shape,
        scratch_shapes=[pltpu.VMEM((SCAN_BLOCKS, rows, SCAN_COLS), F32),
                        pltpu.VMEM((SCAN_BLOCKS, rows, SCAN_COLS), F32),
                        pltpu.VMEM((nb, N_STATES), F32), pltpu.VMEM((nb, N_STATES), F32)],
        compiler_params=_params(1),
        name="s5_setup",
    )(*args)
    return outs[:4], outs[4:8], outs[8:10], outs[11]


NA_Q = NA_Q_ROWS * GRID_W
NA_K = NA_K_ROWS * GRID_W
NA_KEYS = NA_K + NA_META_PAD
NA_RING = LANES


def _block_key_row(blk, rows):
    return min(max(blk * NA_Q_ROWS - WIN_H // 2, 0), rows - NA_K_ROWS)


def _na_assemble_bias(ring_ref, bias, n_blocks, rows):
    kh = min(WIN_H, rows)
    masked = jnp.full((GRID_W, GRID_W), MASK_VALUE, F32)
    meta = jnp.where(lax.broadcasted_iota(jnp.int32, (GRID_W, GRID_W), 1) < N_META, 0.0, MASK_VALUE)
    cq = lax.broadcasted_iota(jnp.int32, (GRID_W, GRID_W), 0)
    ck = lax.broadcasted_iota(jnp.int32, (GRID_W, GRID_W), 1)
    col_start = jnp.clip(cq - WIN_W // 2, 0, GRID_W - WIN_W)
    col_valid = (ck >= col_start) & (ck < col_start + WIN_W)
    col_tiles = {}

    def col_tile(head, off):
        if (head, off) not in col_tiles:
            ring = jnp.broadcast_to(ring_ref[head, off:off + 1, :], (GRID_W, NA_RING))
            toeplitz = pltpu.roll(ring, 0, 1, stride=1, stride_axis=0)[:, :GRID_W]
            col_tiles[head, off] = jnp.where(col_valid, toeplitz, MASK_VALUE)
        return col_tiles[head, off]

    for kind, blk in enumerate((0, 1, n_blocks - 1)):
        for head in range(2):
            for qr in range(NA_Q_ROWS):
                r = blk * NA_Q_ROWS + qr
                row_start = min(max(r - kh // 2, 0), rows - kh)
                tiles = []
                for slot in range(NA_K_ROWS):
                    kr = _block_key_row(blk, rows) + slot
                    valid = row_start <= kr < row_start + kh
                    tiles.append(col_tile(head, kr - r + WIN_H - 1) if valid else masked)
                tiles.append(meta)
                for c in range(0, len(tiles), 2):
                    bias[kind, head, qr * GRID_W:(qr + 1) * GRID_W, c * GRID_W:(c + 2) * GRID_W] = (
                        jnp.concatenate(tiles[c:c + 2], axis=1))


def _na_prologue(q_ref, k_ref, v_ref, km_ref, vm_ref, t_ref, o_ref, bias, *, n_blocks, rows, batch):
    @pl.when(pl.program_id(0) % batch == 0)
    def _():
        _na_assemble_bias(t_ref, bias, n_blocks, rows)


def _na_body(q_ref, k_ref, v_ref, km_ref, vm_ref, t_ref, o_ref, bias, *, n_blocks, rows, batch):
    first_head = lax.broadcasted_iota(jnp.int32, (1, LANES), 1) < NA_HEAD_DIM
    km = km_ref[...]
    vm = vm_ref[...]
    for i in range(n_blocks):
        k0 = _block_key_row(i, rows) * GRID_W
        q0 = i * NA_Q
        kind = 0 if i == 0 else 2 if i == n_blocks - 1 else 1
        q = q_ref[q0:q0 + NA_Q, :]
        zero = jnp.zeros_like(q)
        qs = jnp.concatenate([jnp.where(first_head, q, zero), jnp.where(first_head, zero, q)], axis=0)
        keys = jnp.concatenate([k_ref[k0:k0 + NA_K, :], km], axis=0)
        vals = jnp.concatenate([v_ref[k0:k0 + NA_K, :], vm], axis=0)
        s = _dot_nt(qs, keys) + bias[kind].reshape(2 * NA_Q, NA_KEYS)
        m = jnp.max(s, axis=-1, keepdims=True)
        p = jnp.exp(s - m)
        l = jnp.sum(p, axis=-1, keepdims=True)
        o = _dot(p.astype(BF16), vals) / l
        o_ref[q0:q0 + NA_Q, :] = jnp.where(first_head, o[:NA_Q], o[NA_Q:]).astype(BF16)


def _na_rider(q, k, v, km, vm, col_bias, *, first_pair, n_pairs, batch, seq):
    rows = seq // GRID_W
    assert NA_META_PAD == GRID_W and (NA_K_ROWS + 1) % 2 == 0
    pair = lambda i: first_pair + i // batch
    seq_spec = pl.BlockSpec((seq, LANES), lambda i: (i % batch, pair(i)))
    meta_spec = pl.BlockSpec((NA_META_PAD, LANES), lambda i: (0, pair(i)))
    table_spec = pl.BlockSpec((2, 2 * WIN_H - 1, NA_RING), lambda i: (pair(i), 0, 0))
    return dict(
        steps=n_pairs * batch,
        prologue=functools.partial(_na_prologue, n_blocks=rows // NA_Q_ROWS, rows=rows, batch=batch),
        body=functools.partial(_na_body, n_blocks=rows // NA_Q_ROWS, rows=rows, batch=batch),
        args=[q, k, v, km, vm, col_bias],
        in_specs=[seq_spec, seq_spec, seq_spec, meta_spec, meta_spec, table_spec],
        out_spec=pl.BlockSpec((seq, LANES), lambda i: (i % batch, i // batch)),
        out_shape=jax.ShapeDtypeStruct((batch * seq, n_pairs * LANES), BF16),
        scratch=pltpu.VMEM((3, 2, NA_Q, NA_KEYS), F32))


def _na_col_ring(rpb):
    n_head, n_off, _ = rpb.shape
    ring = jnp.zeros((n_head, n_off, NA_RING), F32)
    ring = ring.at[..., :WIN_W].set(rpb[..., WIN_W - 1:].astype(F32))
    return ring.at[..., NA_RING - (WIN_W - 1):].set(rpb[..., :WIN_W - 1].astype(F32))


def _merge_body(h_ref, z_ref, a_lo_ref, a_hi_ref, g_ref, w_gate_ref, w_ssm_ref, w_na_ref, w_out_ref, o_ref):
    nb, times, _ = h_ref.shape
    h = h_ref[...].reshape(nb * times, D_MODEL)
    hn = _rms_norm(h, g_ref[...]).astype(BF16)
    gates = _dot(hn, w_gate_ref[...])
    z = jnp.swapaxes(z_ref[...].reshape(times, nb, SSM_WIDTH), 0, 1)
    y_ssm = _dot(z.reshape(nb * times, SSM_WIDTH), w_ssm_ref[...])
    y_na = (_dot(a_lo_ref[...].reshape(nb * times, NA_WIDTH // 2), w_na_ref[:NA_WIDTH // 2, :])
            + _dot(a_hi_ref[...].reshape(nb * times, NA_WIDTH // 2), w_na_ref[NA_WIDTH // 2:, :]))
    merged = (jax.nn.sigmoid(gates[:, :D_MODEL]) * y_ssm
              + jax.nn.sigmoid(gates[:, D_MODEL:]) * y_na)
    o_ref[...] = (h + _dot(merged.astype(BF16), w_out_ref[...])).reshape(o_ref.shape)


def _merge_call(h1, z, a_lo, a_hi, g, w_gate, w_ssm, w_na, w_out):
    batch, seq, _ = h1.shape
    times = WIDE_ROW_TILE // batch
    tile_spec = lambda width: pl.BlockSpec((batch, times, width), lambda i: (0, i, 0))
    return pl.pallas_call(
        _merge_body,
        grid=(seq // times,),
        in_specs=[tile_spec(D_MODEL), pl.BlockSpec((WIDE_ROW_TILE, SSM_WIDTH), lambda i: (i, 0)),
                  tile_spec(NA_WIDTH // 2), tile_spec(NA_WIDTH // 2),
                  _const_spec((1, D_MODEL)), _const_spec((D_MODEL, 2 * D_MODEL)),
                  _const_spec((SSM_WIDTH, D_MODEL)), _const_spec((NA_WIDTH, D_MODEL)),
                  _const_spec((D_MODEL, D_MODEL))],
        out_specs=tile_spec(D_MODEL),
        out_shape=jax.ShapeDtypeStruct((batch, seq, D_MODEL), F32),
        compiler_params=_params(1),
        name="merge",
    )(h1, z, a_lo, a_hi, g, w_gate, w_ssm, w_na, w_out)


def kernel(x, meta_tokens, norm_ffn1, w_ffn1_in, w_ffn1_out, norm_mix, w_in, ssm_a_re_fwd, ssm_a_im_fwd, ssm_log_dt_fwd, ssm_a_re_bwd, ssm_a_im_bwd, ssm_log_dt_bwd, ssm_b_re, ssm_b_im, ssm_c_re, ssm_c_im, ssm_d, w_glu, na_rpb, w_branch_ssm, w_branch_na, w_out, norm_ffn2, w_ffn2_in, w_ffn2_out, norm_final):
    batch, seq, d_model = x.shape
    assert d_model == D_MODEL and norm_ffn1.shape[0] == 1, "single-layer block of width 1024"
    assert seq % ROW_TILE == 0 and seq % (GRID_W * NA_Q_ROWS) == 0 and batch % 8 == 0
    rows = seq // GRID_W
    assert rows >= NA_K_ROWS + NA_Q_ROWS

    row = lambda a: a.reshape(1, -1).astype(F32)
    g1, gm, g2, gf = row(norm_ffn1[0]), row(norm_mix[0]), row(norm_ffn2[0]), row(norm_final)
    w1_in, w1_out = w_ffn1_in[0].astype(BF16), w_ffn1_out[0].astype(BF16)
    w_uqkv = w_in[0][:, :UQKV_WIDTH].astype(BF16)
    whole = lambda a: (a[0].astype(F32), 0, a.shape[-1])
    later_weights = [whole(w_ffn2_in), whole(w_ffn2_out), (w_in[0].astype(F32), 1, UQKV_WIDTH),
                     whole(w_branch_ssm), whole(w_branch_na), whole(w_out), whole(w_glu)]
    assert w_in.shape[-1] == 2 * UQKV_WIDTH, "the two gates are the second column block of w_in"

    n_rows = batch * seq
    tok_spec = pl.BlockSpec((FFN2_ROW_TILE, D_MODEL), lambda i: (i, 0))
    (h1, u, q, k, v, u_meta, _, k_meta, v_meta,
     w2_in, w2_out, w_gate, w_ssm, w_na, w_mix_out, w_glu_bf) = _ffn_proj_call(
        x, meta_tokens.astype(F32), g1, w1_in, w1_out, gm, w_uqkv, later_weights)
    q, k, v = (a.reshape(n_rows, NA_WIDTH) for a in (q, k, v))

    states = lambda a: a[0].reshape(1, N_STATES).astype(F32)
    per_state = lambda a: jnp.repeat(a[0].astype(F32), SSM_STATE).reshape(1, N_STATES)
    b_re_t = ssm_b_re[0].reshape(N_STATES, SSM_GROUP).T.astype(F32)
    b_im_t = ssm_b_im[0].reshape(N_STATES, SSM_GROUP).T.astype(F32)
    zero_state = jnp.zeros((batch, 2 * N_STATES), F32)
    u_meta_rows = jnp.repeat(u_meta, batch, axis=0)
    (lbf_re, lbf_im, wf_re, wf_im), (lbb_re, lbb_im, wb_re, wb_im), (c_re, c_im), state_meta = _s5_setup_call(
        (states(ssm_a_re_fwd), states(ssm_a_im_fwd), per_state(ssm_log_dt_fwd)),
        (states(ssm_a_re_bwd), states(ssm_a_im_bwd), per_state(ssm_log_dt_bwd)),
        b_re_t, b_im_t,
        ssm_c_re[0].reshape(SSM_WIDTH, SSM_STATE).astype(F32),
        ssm_c_im[0].reshape(SSM_WIDTH, SSM_STATE).astype(F32),
        u_meta_rows, zero_state)
    pad_meta = lambda a: jnp.pad(a, ((0, NA_META_PAD - N_META), (0, 0)))
    half = NA_HEADS // 4
    riders = [_na_rider(q, k, v, pad_meta(k_meta), pad_meta(v_meta), _na_col_ring(na_rpb[0]),
                        first_pair=p0, n_pairs=half, batch=batch, seq=seq) for p0 in (0, half)]
    y_fwd, att_lo = _s5_call(u, state_meta, lbf_re, lbf_im, wf_re, wf_im, c_re, c_im,
                                None, tile=SCAN_TILE, reverse=False, rider=riders[0])
    z, att_hi = _s5_call(u, zero_state, lbb_re, lbb_im, wb_re, wb_im, c_re, c_im,
                            (y_fwd, row(ssm_d[0]), w_glu_bf), tile=SCAN_TILE,
                            reverse=True, rider=riders[1])

    half_width = (batch, seq, NA_WIDTH // 2)
    h2 = _merge_call(h1, z, att_lo.reshape(half_width), att_hi.reshape(half_width), gm, w_gate,
                     w_ssm, w_na, w_mix_out).reshape(n_rows, D_MODEL)

    out = _ffn_call(h2, g2, w2_in, w2_out, gf, grid=(n_rows // FFN2_ROW_TILE,), x_spec=tok_spec,
                    o_spec=tok_spec, out_shape=(n_rows, D_MODEL), final_norm=True)
    return out.reshape(batch, seq, D_MODEL)
```
